```python
import math
import jax, jax.numpy as jnp
from jax import lax
import numpy as np

D_MODEL = 1024
BATCH = 8
SEQ = 4096
DEPTH = 1

MIX_WIDTH = D_MODEL
MLA_WIDTH = MIX_WIDTH // 2
RET_WIDTH = MIX_WIDTH - MLA_WIDTH
MLA_HEADS = 4
MLA_NOPE_DIM = 128
MLA_ROPE_DIM = 64
MLA_QK_DIM = MLA_NOPE_DIM + MLA_ROPE_DIM
MLA_V_DIM = MLA_WIDTH // MLA_HEADS
MLA_Q_RANK = 256
MLA_KV_RANK = 128
Q_BLOCK = 128
RET_HEADS = 4
RET_HEAD_DIM = RET_WIDTH // RET_HEADS
RET_CHUNK = 128
MEM_TOKENS = 256
CROSS_HEADS = 4
CROSS_HEAD_DIM = D_MODEL // CROSS_HEADS
N_GROUPS = 4
EXPERTS_PER_GROUP = 8
N_EXPERTS = N_GROUPS * EXPERTS_PER_GROUP
TOP_K = 2
EXPERT_FF = 256
ROPE_BASE = 10000.0
EPS = 1e-6
NEG_INF = -1e30
IN_SPLITS = (MLA_Q_RANK, MLA_KV_RANK, MLA_ROPE_DIM, RET_WIDTH, RET_WIDTH, RET_WIDTH, RET_WIDTH)
IN_COLS = sum(IN_SPLITS)

kernel_name = 'hybrid_mla_retention_hmoe_block'


def rms_norm(x, g):
    xf = x.astype(jnp.float32)
    y = xf * lax.rsqrt(jnp.mean(xf * xf, axis=-1, keepdims=True) + EPS)
    return (y * g.astype(jnp.float32)).astype(x.dtype)


def rope(x, positions):
    half = x.shape[-1] // 2
    inv_freq = ROPE_BASE ** (-jnp.arange(half, dtype=jnp.float32) / half)
    ang = positions.astype(jnp.float32)[:, :, None] * inv_freq
    cos = jnp.cos(ang)[:, :, None, :]
    sin = jnp.sin(ang)[:, :, None, :]
    xf = x.astype(jnp.float32)
    x1, x2 = xf[..., :half], xf[..., half:]
    return jnp.concatenate([x1 * cos - x2 * sin, x2 * cos + x1 * sin], axis=-1).astype(x.dtype)


def causal_block_attention(q, k, v):
    B, S, H, dq = q.shape
    dv = v.shape[-1]
    nb = S // Q_BLOCK
    scale = dq ** -0.5
    qb = q.reshape(B, nb, Q_BLOCK, H, dq).transpose(1, 0, 2, 3, 4)
    k_idx = jnp.arange(S)

    def one_block(args):
        q_blk, b = args
        s = jnp.einsum('bqhd,bkhd->bhqk', q_blk, k, preferred_element_type=jnp.float32) * scale
        q_idx = b * Q_BLOCK + jnp.arange(Q_BLOCK)
        s = jnp.where(k_idx[None, :] <= q_idx[:, None], s, NEG_INF)
        p = jax.nn.softmax(s, axis=-1).astype(v.dtype)
        return jnp.einsum('bhqk,bkhd->bqhd', p, v)

    out = lax.map(one_block, (qb, jnp.arange(nb)))
    return out.transpose(1, 0, 2, 3, 4).reshape(B, S, H, dv)


def retention_chunkwise(q, k, v):
    B, S, H, dk = q.shape
    dv = v.shape[-1]
    nc = S // RET_CHUNK

    def to_chunks(t):
        return t.astype(jnp.float32).reshape(B, nc, RET_CHUNK, H, t.shape[-1]).transpose(1, 0, 3, 2, 4)

    lg = jnp.log(1.0 - jnp.exp2(-5.0 - jnp.arange(H, dtype=jnp.float32)))
    idx = jnp.arange(RET_CHUNK, dtype=jnp.float32)
    diff = idx[:, None] - idx[None, :]
    intra = jnp.where(diff >= 0, jnp.exp(jnp.maximum(diff, 0.0)[None] * lg[:, None, None]), 0.0)
    q_decay = jnp.exp((idx + 1.0)[None] * lg[:, None])[:, :, None]
    k_decay = jnp.exp((RET_CHUNK - 1.0 - idx)[None] * lg[:, None])[:, :, None]
    chunk_decay = jnp.exp(RET_CHUNK * lg)[:, None, None]

    def step(state, qkv):
        qc, kc, vc = qkv
        scores = jnp.einsum('bhid,bhjd->bhij', qc, kc) * intra
        o = (jnp.einsum('bhij,bhjv->bhiv', scores, vc)
             + jnp.einsum('bhid,bhdv->bhiv', qc * q_decay, state))
        state = state * chunk_decay + jnp.einsum('bhjd,bhjv->bhdv', kc * k_decay, vc)
        return state, o

    state0 = jnp.zeros((B, H, dk, dv), jnp.float32)
    _, o = lax.scan(step, state0, (to_chunks(q), to_chunks(k), to_chunks(v)))
    return o.transpose(1, 0, 3, 2, 4).reshape(B, S, H, dv)


def mla_group(c_q, c_kv, k_pe, positions, q_norm_g, w_uq, kv_norm_g, w_ukv, q_qk_g, k_qk_g):
    B, S, _ = c_q.shape
    q = (rms_norm(c_q, q_norm_g) @ w_uq).reshape(B, S, MLA_HEADS, MLA_QK_DIM)
    kv = (rms_norm(c_kv, kv_norm_g) @ w_ukv).reshape(B, S, MLA_HEADS, MLA_NOPE_DIM + MLA_V_DIM)
    k_nope, v = kv[..., :MLA_NOPE_DIM], kv[..., MLA_NOPE_DIM:]
    k_rope = jnp.broadcast_to(k_pe[:, :, None, :], (B, S, MLA_HEADS, MLA_ROPE_DIM))
    k = jnp.concatenate([k_nope, k_rope], axis=-1)
    q = rms_norm(q, q_qk_g)
    k = rms_norm(k, k_qk_g)
    q = jnp.concatenate([q[..., :MLA_NOPE_DIM], rope(q[..., MLA_NOPE_DIM:], positions)], axis=-1)
    k = jnp.concatenate([k[..., :MLA_NOPE_DIM], rope(k[..., MLA_NOPE_DIM:], positions)], axis=-1)
    o = causal_block_attention(q, k, v)
    return o.reshape(B, S, MLA_WIDTH)


def retention_group(rq, rk, rv, rg, positions, gn_g):
    B, S, _ = rq.shape
    q = rope(rq.reshape(B, S, RET_HEADS, RET_HEAD_DIM), positions)
    k = rope(rk.reshape(B, S, RET_HEADS, RET_HEAD_DIM), positions) * (RET_HEAD_DIM ** -0.5)
    v = rv.reshape(B, S, RET_HEADS, RET_HEAD_DIM)
    o = retention_chunkwise(q, k, v)
    mu = jnp.mean(o, axis=-1, keepdims=True)
    var = jnp.mean(jnp.square(o - mu), axis=-1, keepdims=True)
    o = ((o - mu) * lax.rsqrt(var + EPS)).reshape(B, S, RET_WIDTH) * gn_g.astype(jnp.float32)
    return (o * jax.nn.silu(rg.astype(jnp.float32))).astype(rq.dtype)


def memory_cross_attention(h, m, w_q, w_kv, q_qk_g, k_qk_g, w_o):
    B, S, _ = h.shape
    M = m.shape[1]
    q = rms_norm((h @ w_q).reshape(B, S, CROSS_HEADS, CROSS_HEAD_DIM), q_qk_g)
    kv = m @ w_kv
    k = rms_norm(kv[..., :D_MODEL].reshape(B, M, CROSS_HEADS, CROSS_HEAD_DIM), k_qk_g)
    v = kv[..., D_MODEL:].reshape(B, M, CROSS_HEADS, CROSS_HEAD_DIM)
    s = jnp.einsum('bshd,bmhd->bhsm', q, k, preferred_element_type=jnp.float32) * (CROSS_HEAD_DIM ** -0.5)
    p = jax.nn.softmax(s, axis=-1).astype(v.dtype)
    o = jnp.einsum('bhsm,bmhd->bshd', p, v).reshape(B, S, D_MODEL)
    return o @ w_o


def hierarchical_moe(h, w_group, b_group, w_expert, b_expert, w_gate, w_up, w_down):
    def per_row(hr):
        S = hr.shape[0]
        g_logits = jnp.einsum('sd,dg->sg', hr, w_group, preferred_element_type=jnp.float32)
        g_prob = jax.nn.softmax(g_logits, axis=-1)
        g_sel = jnp.argmax(g_logits + b_group.astype(jnp.float32), axis=-1)
        g_w = jnp.take_along_axis(g_prob, g_sel[:, None], axis=-1)
        e_all = jnp.einsum('sd,de->se', hr, w_expert, preferred_element_type=jnp.float32)
        e_all = e_all.reshape(S, N_GROUPS, EXPERTS_PER_GROUP)
        e_logits = jnp.take_along_axis(e_all, g_sel[:, None, None], axis=1)[:, 0]
        e_bias = b_expert.astype(jnp.float32).reshape(N_GROUPS, EXPERTS_PER_GROUP)[g_sel]
        e_prob = jax.nn.softmax(e_logits, axis=-1)
        _, top_idx = lax.top_k(e_logits + e_bias, TOP_K)
        top_p = jnp.take_along_axis(e_prob, top_idx, axis=-1)
        top_p = top_p / jnp.sum(top_p, axis=-1, keepdims=True)
        local_w = jnp.sum(jax.nn.one_hot(top_idx, EXPERTS_PER_GROUP, dtype=jnp.float32) * top_p[..., None], axis=1)
        gate = (jax.nn.one_hot(g_sel, N_GROUPS, dtype=jnp.float32)[:, :, None]
                * local_w[:, None, :] * g_w[:, :, None]).reshape(S, N_EXPERTS)
        a = jnp.einsum('sd,edf->sef', hr, w_gate)
        u = jnp.einsum('sd,edf->sef', hr, w_up)
        act = jax.nn.silu(a) * u * gate[:, :, None].astype(hr.dtype)
        return jnp.einsum('sef,efd->sd', act, w_down)

    return lax.map(per_row, h)


def setup_inputs(seed: int = 0) -> dict:
    key = jax.random.key(seed)
    ks = jax.random.split(key, 32)

    def w(k, shape, fan_in):
        return jax.random.normal(k, (DEPTH,) + shape, jnp.float32) * (fan_in ** -0.5)

    def gain(k, n):
        return 1.0 + 0.02 * jax.random.normal(k, (DEPTH, n), jnp.float32)

    def bias(k, n):
        return 0.01 * jax.random.normal(k, (DEPTH, n), jnp.float32)

    x = jax.random.normal(ks[0], (BATCH, SEQ, D_MODEL), jnp.float32)
    mem = jax.random.normal(ks[1], (BATCH, MEM_TOKENS, D_MODEL), jnp.float32)
    offsets = jax.random.randint(ks[2], (BATCH, 1), 0, 1024, dtype=jnp.int32)
    positions = offsets + jnp.arange(SEQ, dtype=jnp.int32)[None, :]
    return {
        'x': x,
        'mem': mem,
        'positions': positions,
        'attn_norm_g': gain(ks[3], D_MODEL),
        'w_in': w(ks[4], (D_MODEL, IN_COLS), D_MODEL),
        'mla_q_norm_g': gain(ks[5], MLA_Q_RANK),
        'mla_w_uq': w(ks[6], (MLA_Q_RANK, MLA_HEADS * MLA_QK_DIM), MLA_Q_RANK),
        'mla_kv_norm_g': gain(ks[7], MLA_KV_RANK),
        'mla_w_ukv': w(ks[8], (MLA_KV_RANK, MLA_HEADS * (MLA_NOPE_DIM + MLA_V_DIM)), MLA_KV_RANK),
        'mla_q_qk_g': gain(ks[9], MLA_QK_DIM),
        'mla_k_qk_g': gain(ks[10], MLA_QK_DIM),
        'ret_gn_g': gain(ks[11], RET_WIDTH),
        'w_out': w(ks[12], (MIX_WIDTH, D_MODEL), MIX_WIDTH),
        'cross_norm_g': gain(ks[13], D_MODEL),
        'mem_norm_g': gain(ks[14], D_MODEL),
        'cross_w_q': w(ks[15], (D_MODEL, D_MODEL), D_MODEL),
        'cross_w_kv': w(ks[16], (D_MODEL, 2 * D_MODEL), D_MODEL),
        'cross_q_qk_g': gain(ks[17], CROSS_HEAD_DIM),
        'cross_k_qk_g': gain(ks[18], CROSS_HEAD_DIM),
        'cross_w_o': w(ks[19], (D_MODEL, D_MODEL), D_MODEL),
        'moe_norm_g': gain(ks[20], D_MODEL),
        'router_w_group': w(ks[21], (D_MODEL, N_GROUPS), D_MODEL),
        'router_b_group': bias(ks[22], N_GROUPS),
        'router_w_expert': w(ks[23], (D_MODEL, N_EXPERTS), D_MODEL),
        'router_b_expert': bias(ks[24], N_EXPERTS),
        'expert_w_gate': w(ks[25], (N_EXPERTS, D_MODEL, EXPERT_FF), D_MODEL),
        'expert_w_up': w(ks[26], (N_EXPERTS, D_MODEL, EXPERT_FF), D_MODEL),
        'expert_w_down': w(ks[27], (N_EXPERTS, EXPERT_FF, D_MODEL), EXPERT_FF),
    }


def reference(x, mem, positions, attn_norm_g, w_in, mla_q_norm_g, mla_w_uq, mla_kv_norm_g, mla_w_ukv,
              mla_q_qk_g, mla_k_qk_g, ret_gn_g, w_out, cross_norm_g, mem_norm_g, cross_w_q, cross_w_kv,
              cross_q_qk_g, cross_k_qk_g, cross_w_o, moe_norm_g, router_w_group, router_b_group,
              router_w_expert, router_b_expert, expert_w_gate, expert_w_up, expert_w_down):
    split_points = [int(s) for s in np.cumsum(IN_SPLITS)[:-1]]
    for l in range(DEPTH):
        h = rms_norm(x, attn_norm_g[l])
        proj = h @ w_in[l]
        c_q, c_kv, k_pe, rq, rk, rv, rg = jnp.split(proj, split_points, axis=-1)
        a_out = mla_group(c_q, c_kv, k_pe, positions, mla_q_norm_g[l], mla_w_uq[l],
                          mla_kv_norm_g[l], mla_w_ukv[l], mla_q_qk_g[l], mla_k_qk_g[l])
        r_out = retention_group(rq, rk, rv, rg, positions, ret_gn_g[l])
        x = x + jnp.concatenate([a_out, r_out], axis=-1) @ w_out[l]
        hc = rms_norm(x, cross_norm_g[l])
        m = rms_norm(mem, mem_norm_g[l])
        x = x + memory_cross_attention(hc, m, cross_w_q[l], cross_w_kv[l], cross_q_qk_g[l],
                                       cross_k_qk_g[l], cross_w_o[l])
        hm = rms_norm(x, moe_norm_g[l])
        x = x + hierarchical_moe(hm, router_w_group[l], router_b_group[l], router_w_expert[l],
                                 router_b_expert[l], expert_w_gate[l], expert_w_up[l], expert_w_down[l])
    return x
```

```python
import functools
import math

import numpy as np
import jax
import jax.numpy as jnp
from jax import lax
from jax.experimental import pallas as pl
from jax.experimental.pallas import tpu as pltpu

F32 = jnp.float32
BF16 = jnp.bfloat16

LANES = 128
VMEM_LIMIT_BYTES = 56 * 1024 * 1024

MLA_HEADS = 4
MLA_NOPE_DIM = 128
MLA_ROPE_DIM = 64
MLA_QK_DIM = MLA_NOPE_DIM + MLA_ROPE_DIM
MLA_V_DIM = 128
MLA_Q_RANK = 256
MLA_KV_RANK = 128
MLA_HEAD_PAD = 256
RET_HEADS = 4
RET_HEAD_DIM = 128
RET_WIDTH = RET_HEADS * RET_HEAD_DIM
RET_CHUNK = 128
CROSS_HEADS = 4
N_GROUPS = 4
EXPERTS_PER_GROUP = 8
N_EXPERTS = 32
ROPE_BASE = 10000.0
EPS = 1e-6
NEG_INF = -1e30

_C_Q = 0
_C_KV = _C_Q + MLA_Q_RANK
_RQ = _C_KV + MLA_KV_RANK
_RK = _RQ + RET_WIDTH
_RV = _RK + RET_WIDTH
_RG = _RV + RET_WIDTH
_KPE = _RG + RET_WIDTH
_KPE_SW = _KPE + LANES
IN_COLS_PAD = _KPE_SW + LANES
Q_HEAD_COLS = 3 * LANES


def _rms(x, g):
    return x * lax.rsqrt(jnp.mean(x * x, axis=-1, keepdims=True) + EPS) * g


def _const_spec(shape):
    nd = len(shape)
    return pl.BlockSpec(shape, lambda *_: (0,) * nd, pipeline_mode=pl.Buffered(1))


def _prep_kernel(x_ref, pos_ref, g_ref, win_ref, qng_ref, wuq_ref, kvng_ref, wukv_ref,
                 qgn_ref, qgr_ref, qgs_ref, kgn_ref, kgr_ref, kgs_ref,
                 fm_ref, sm_ref, fr_ref, sr_ref,
                 q_out, k_out, v_out, rq_out, rk_out, rv_out, rg_out):
    x = x_ref[...]
    h = _rms(x, g_ref[...]).astype(BF16)
    proj = jnp.dot(h, win_ref[...], preferred_element_type=F32)

    posf = pos_ref[...].astype(F32)
    ang_m = posf * fm_ref[...]
    cos_m = jnp.cos(ang_m)
    sin_m = jnp.sin(ang_m) * sm_ref[...]
    ang_r = posf * fr_ref[...]
    cos_r = jnp.cos(ang_r)
    sin_r = jnp.sin(ang_r) * sr_ref[...]

    c_q = proj[:, _C_Q:_C_Q + MLA_Q_RANK]
    qall = jnp.dot(_rms(c_q, qng_ref[...]).astype(BF16), wuq_ref[...], preferred_element_type=F32)
    q_scale = MLA_QK_DIM ** -0.5
    for hd in range(MLA_HEADS):
        base = hd * Q_HEAD_COLS
        nope = qall[:, base:base + LANES]
        rope = qall[:, base + LANES:base + 2 * LANES]
        rope_sw = qall[:, base + 2 * LANES:base + 3 * LANES]
        ss = jnp.sum(nope * nope, axis=-1, keepdims=True) + jnp.sum(rope * rope, axis=-1, keepdims=True)
        r = lax.rsqrt(ss * (1.0 / MLA_QK_DIM) + EPS) * q_scale
        roped = rope * qgr_ref[...] * cos_m + rope_sw * qgs_ref[...] * sin_m
        q_out[0, hd, :, 0:LANES] = (nope * qgn_ref[...] * r).astype(BF16)
        q_out[0, hd, :, LANES:2 * LANES] = (roped * r).astype(BF16)

    c_kv = proj[:, _C_KV:_C_KV + MLA_KV_RANK]
    kv = jnp.dot(_rms(c_kv, kvng_ref[...]).astype(BF16), wukv_ref[...], preferred_element_type=F32)
    k_pe = proj[:, _KPE:_KPE + LANES]
    k_pe_sw = proj[:, _KPE_SW:_KPE_SW + LANES]
    kr = k_pe * kgr_ref[...] * cos_m + k_pe_sw * kgs_ref[...] * sin_m
    ss_pe = jnp.sum(k_pe * k_pe, axis=-1, keepdims=True)
    for hd in range(MLA_HEADS):
        k_nope = kv[:, hd * LANES:(hd + 1) * LANES]
        ss = jnp.sum(k_nope * k_nope, axis=-1, keepdims=True) + ss_pe
        r = lax.rsqrt(ss * (1.0 / MLA_QK_DIM) + EPS)
        k_out[0, hd, :, 0:LANES] = (k_nope * kgn_ref[...] * r).astype(BF16)
        k_out[0, hd, :, LANES:2 * LANES] = (kr * r).astype(BF16)
        v_out[0, hd] = kv[:, (MLA_HEADS + hd) * LANES:(MLA_HEADS + hd + 1) * LANES].astype(BF16)

    k_scale = RET_HEAD_DIM ** -0.5
    for hd in range(RET_HEADS):
        sl = slice(hd * LANES, (hd + 1) * LANES)
        rq = proj[:, _RQ + hd * LANES:_RQ + (hd + 1) * LANES]
        rk = proj[:, _RK + hd * LANES:_RK + (hd + 1) * LANES]
        rq_out[:, sl] = (rq * cos_r + pltpu.roll(rq, LANES // 2, axis=1) * sin_r).astype(BF16)
        rk_out[:, sl] = ((rk * cos_r + pltpu.roll(rk, LANES // 2, axis=1) * sin_r) * k_scale).astype(BF16)
    rv_out[...] = proj[:, _RV:_RV + RET_WIDTH].astype(BF16)
    rg_out[...] = proj[:, _RG:_RG + RET_WIDTH].astype(BF16)


def _prep(x2d, pos2d, p, batch, seq, tile):
    n_tok, d = x2d.shape
    tiles_per_b = seq // tile
    tok = lambda i: (i, 0)
    head = lambda i: (i // tiles_per_b, 0, i % tiles_per_b, 0)
    consts = [p['attn_norm_g'], p['w_in'], p['q_norm_g'], p['w_uq'], p['kv_norm_g'], p['w_ukv'],
              p['q_g_nope'], p['q_g_rope'], p['q_g_rope_sw'], p['k_g_nope'], p['k_g_rope'], p['k_g_rope_sw'],
              p['freq_m'], p['sign_m'], p['freq_r'], p['sign_r']]
    out_shape = (
        jax.ShapeDtypeStruct((batch, MLA_HEADS, seq, MLA_HEAD_PAD), BF16),
        jax.ShapeDtypeStruct((batch, MLA_HEADS, seq, MLA_HEAD_PAD), BF16),
        jax.ShapeDtypeStruct((batch, MLA_HEADS, seq, MLA_V_DIM), BF16),
        jax.ShapeDtypeStruct((n_tok, RET_WIDTH), BF16),
        jax.ShapeDtypeStruct((n_tok, RET_WIDTH), BF16),
        jax.ShapeDtypeStruct((n_tok, RET_WIDTH), BF16),
        jax.ShapeDtypeStruct((n_tok, RET_WIDTH), BF16),
    )
    return pl.pallas_call(
        _prep_kernel,
        grid=(n_tok // tile,),
        in_specs=[pl.BlockSpec((tile, d), tok), pl.BlockSpec((tile, 1), tok)]
                 + [_const_spec(c.shape) for c in consts],
        out_specs=(
            pl.BlockSpec((1, MLA_HEADS, tile, MLA_HEAD_PAD), head),
            pl.BlockSpec((1, MLA_HEADS, tile, MLA_HEAD_PAD), head),
            pl.BlockSpec((1, MLA_HEADS, tile, MLA_V_DIM), head),
            pl.BlockSpec((tile, RET_WIDTH), tok),
            pl.BlockSpec((tile, RET_WIDTH), tok),
            pl.BlockSpec((tile, RET_WIDTH), tok),
            pl.BlockSpec((tile, RET_WIDTH), tok),
        ),
        out_shape=out_shape,
        compiler_params=pltpu.CompilerParams(dimension_semantics=("parallel",),
                                             vmem_limit_bytes=VMEM_LIMIT_BYTES),
        name="prep",
    )(x2d, pos2d, *consts)


def _flash_kernel(q_ref, k_ref, v_ref, o_ref, m_sc, l_sc, acc_sc, *, tq, tk):
    qi = pl.program_id(2)
    q = q_ref[0, 0]
    m_sc[...] = jnp.full(m_sc.shape, NEG_INF, F32)
    l_sc[...] = jnp.zeros(l_sc.shape, F32)
    acc_sc[...] = jnp.zeros(acc_sc.shape, F32)

    def block(j, masked):
        k = k_ref[0, 0, pl.ds(pl.multiple_of(j * tk, tk), tk), :]
        v = v_ref[0, 0, pl.ds(pl.multiple_of(j * tk, tk), tk), :]
        s = lax.dot_general(q, k, (((1,), (1,)), ((), ())), preferred_element_type=F32)
        if masked:
            row = lax.broadcasted_iota(jnp.int32, (tq, tk), 0) + qi * tq
            col = lax.broadcasted_iota(jnp.int32, (tq, tk), 1) + j * tk
            s = jnp.where(col <= row, s, NEG_INF)
        m_prev = m_sc[...]
        m_new = jnp.maximum(m_prev, jnp.max(s, axis=-1, keepdims=True))
        alpha = jnp.exp(m_prev - m_new)
        p = jnp.exp(s - m_new)
        l_sc[...] = alpha * l_sc[...] + jnp.sum(p, axis=-1, keepdims=True)
        acc_sc[...] = alpha * acc_sc[...] + jnp.dot(p.astype(BF16), v, preferred_element_type=F32)
        m_sc[...] = m_new

    n_full = qi * (tq // tk)

    def body(j, c):
        block(j, False)
        return c

    lax.fori_loop(0, n_full, body, 0)
    for d in range(tq // tk):
        block(n_full + d, True)
    o_ref[0] = (acc_sc[...] / l_sc[...]).astype(o_ref.dtype)


def _flash(q, k, v, tq, tk):
    batch, heads, seq, dqk = q.shape
    dv = v.shape[-1]
    return pl.pallas_call(
        functools.partial(_flash_kernel, tq=tq, tk=tk),
        grid=(batch, heads, seq // tq),
        in_specs=[
            pl.BlockSpec((1, 1, tq, dqk), lambda b, h, i: (b, h, i, 0)),
            pl.BlockSpec((1, 1, seq, dqk), lambda b, h, i: (b, h, 0, 0)),
            pl.BlockSpec((1, 1, seq, dv), lambda b, h, i: (b, h, 0, 0)),
        ],
        out_specs=pl.BlockSpec((1, tq, dv), lambda b, h, i: (b, i, h)),
        out_shape=jax.ShapeDtypeStruct((batch, seq, heads * dv), BF16),
        scratch_shapes=[pltpu.VMEM((tq, 1), F32), pltpu.VMEM((tq, 1), F32), pltpu.VMEM((tq, dv), F32)],
        compiler_params=pltpu.CompilerParams(dimension_semantics=("parallel", "parallel", "arbitrary"),
                                             vmem_limit_bytes=VMEM_LIMIT_BYTES),
        name="flash",
    )(q, k, v)


def _ret_kernel(q_ref, k_ref, v_ref, g_ref, gn_ref, o_ref, state_sc):
    c = pl.program_id(1)

    @pl.when(c == 0)
    def _():
        state_sc[...] = jnp.zeros(state_sc.shape, F32)

    n = RET_CHUNK
    ri = lax.broadcasted_iota(jnp.int32, (n, n), 0)
    ci = lax.broadcasted_iota(jnp.int32, (n, n), 1)
    diff = (ri - ci).astype(F32)
    idx = lax.broadcasted_iota(jnp.int32, (n, 1), 0).astype(F32)
    for hd in range(RET_HEADS):
        lg = math.log(1.0 - 2.0 ** (-5.0 - hd))
        intra = jnp.where(diff >= 0, jnp.exp(jnp.maximum(diff, 0.0) * lg), 0.0)
        q_decay = jnp.exp((idx + 1.0) * lg)
        k_decay = jnp.exp((n - 1.0 - idx) * lg)
        chunk_decay = math.exp(n * lg)
        sl = slice(hd * LANES, (hd + 1) * LANES)
        qc = q_ref[0, :, sl]
        kc = k_ref[0, :, sl]
        vc = v_ref[0, :, sl]
        state = state_sc[hd]
        scores = lax.dot_general(qc, kc, (((1,), (1,)), ((), ())), preferred_element_type=F32) * intra
        o = (jnp.dot(scores.astype(BF16), vc, preferred_element_type=F32)
             + jnp.dot((qc.astype(F32) * q_decay).astype(BF16), state.astype(BF16), preferred_element_type=F32))
        kd_t = (kc.astype(F32) * k_decay).T.astype(BF16)
        state_sc[hd] = state * chunk_decay + jnp.dot(kd_t, vc, preferred_element_type=F32)
        mu = jnp.mean(o, axis=-1, keepdims=True)
        cen = o - mu
        var = jnp.mean(cen * cen, axis=-1, keepdims=True)
        gate = g_ref[0, :, sl].astype(F32)
        y = cen * lax.rsqrt(var + EPS) * gn_ref[:, sl] * (gate * jax.nn.sigmoid(gate))
        o_ref[0, :, sl] = y.astype(o_ref.dtype)


def _retention(rq, rk, rv, rg, gn_g):
    batch, seq, width = rq.shape
    blk = pl.BlockSpec((1, RET_CHUNK, width), lambda b, c: (b, c, 0))
    return pl.pallas_call(
        _ret_kernel,
        grid=(batch, seq // RET_CHUNK),
        in_specs=[blk, blk, blk, blk, _const_spec(gn_g.shape)],
        out_specs=blk,
        out_shape=jax.ShapeDtypeStruct((batch, seq, width), BF16),
        scratch_shapes=[pltpu.VMEM((RET_HEADS, RET_HEAD_DIM, RET_HEAD_DIM), F32)],
        compiler_params=pltpu.CompilerParams(dimension_semantics=("parallel", "arbitrary")),
        name="retention",
    )(rq, rk, rv, rg, gn_g)


def _memkv_kernel(mem_ref, g_ref, wkv_ref, kg_ref, k_out, v_out):
    d = mem_ref.shape[-1]
    hdim = d // CROSS_HEADS
    m = _rms(mem_ref[0], g_ref[...]).astype(BF16)
    kv = jnp.dot(m, wkv_ref[...], preferred_element_type=F32)
    for hd in range(CROSS_HEADS):
        sl = slice(hd * hdim, (hd + 1) * hdim)
        k_out[0, :, sl] = _rms(kv[:, sl], kg_ref[...]).astype(BF16)
    v_out[0] = kv[:, d:].astype(BF16)


def _memkv(mem, mem_norm_g, w_kv, k_g):
    batch, m_tok, d = mem.shape
    blk = pl.BlockSpec((1, m_tok, d), lambda b: (b, 0, 0))
    return pl.pallas_call(
        _memkv_kernel,
        grid=(batch,),
        in_specs=[blk, _const_spec(mem_norm_g.shape), _const_spec(w_kv.shape), _const_spec(k_g.shape)],
        out_specs=(blk, blk),
        out_shape=(jax.ShapeDtypeStruct((batch, m_tok, d), BF16), jax.ShapeDtypeStruct((batch, m_tok, d), BF16)),
        compiler_params=pltpu.CompilerParams(dimension_semantics=("parallel",),
                                             vmem_limit_bytes=VMEM_LIMIT_BYTES),
        name="memkv",
    )(mem, mem_norm_g, w_kv, k_g)


def _mix_kernel(x_ref, a_ref, r_ref, kc_ref, vc_ref, woa_ref, wor_ref, cng_ref, wq_ref, qg_ref, wo_ref,
                mng_ref, wr_ref, br_ref, x2_out, hm_out, gate_out):
    d = x_ref.shape[-1]
    hdim = d // CROSS_HEADS
    x1 = (x_ref[...]
          + jnp.dot(a_ref[...], woa_ref[...], preferred_element_type=F32)
          + jnp.dot(r_ref[...], wor_ref[...], preferred_element_type=F32))
    hc = _rms(x1, cng_ref[...]).astype(BF16)
    q = jnp.dot(hc, wq_ref[...], preferred_element_type=F32)
    scale = hdim ** -0.5
    heads = []
    for hd in range(CROSS_HEADS):
        sl = slice(hd * hdim, (hd + 1) * hdim)
        qh = (_rms(q[:, sl], qg_ref[...]) * scale).astype(BF16)
        s = lax.dot_general(qh, kc_ref[0, :, sl], (((1,), (1,)), ((), ())), preferred_element_type=F32)
        s = s - jnp.max(s, axis=-1, keepdims=True)
        e = jnp.exp(s)
        pr = e / jnp.sum(e, axis=-1, keepdims=True)
        heads.append(jnp.dot(pr.astype(BF16), vc_ref[0, :, sl], preferred_element_type=F32).astype(BF16))
    o = jnp.concatenate(heads, axis=-1)
    x2 = x1 + jnp.dot(o, wo_ref[...], preferred_element_type=F32)
    x2_out[...] = x2
    hm = _rms(x2, mng_ref[...])
    hm_out[...] = hm.astype(BF16)

    logits = jnp.dot(hm, wr_ref[...], preferred_element_type=F32, precision=lax.Precision.HIGHEST)
    biased = logits + br_ref[...]
    lane = lax.broadcasted_iota(jnp.int32, logits.shape, 1)
    lane_f = lane.astype(F32)
    big = float(4 * LANES)
    is_g = (lane >= N_EXPERTS) & (lane < N_EXPERTS + N_GROUPS)
    gl = jnp.where(is_g, logits, NEG_INF)
    gexp = jnp.where(is_g, jnp.exp(gl - jnp.max(gl, axis=-1, keepdims=True)), 0.0)
    gb = jnp.where(is_g, biased, NEG_INF)
    g_lane = jnp.min(jnp.where(gb == jnp.max(gb, axis=-1, keepdims=True), lane_f, big), axis=-1, keepdims=True)
    g_w = (jnp.sum(jnp.where(lane_f == g_lane, gexp, 0.0), axis=-1, keepdims=True)
           / jnp.sum(gexp, axis=-1, keepdims=True))
    g_idx = g_lane.astype(jnp.int32) - N_EXPERTS
    in_grp = (lane < N_EXPERTS) & ((lane // EXPERTS_PER_GROUP) == g_idx)
    el = jnp.where(in_grp, logits, NEG_INF)
    eexp = jnp.where(in_grp, jnp.exp(el - jnp.max(el, axis=-1, keepdims=True)), 0.0)
    eb = jnp.where(in_grp, biased, NEG_INF)
    i1 = jnp.min(jnp.where(eb == jnp.max(eb, axis=-1, keepdims=True), lane_f, big), axis=-1, keepdims=True)
    eb2 = jnp.where(lane_f == i1, NEG_INF, eb)
    i2 = jnp.min(jnp.where(eb2 == jnp.max(eb2, axis=-1, keepdims=True), lane_f, big), axis=-1, keepdims=True)
    psel = jnp.where((lane_f == i1) | (lane_f == i2), eexp, 0.0)
    gate_out[...] = psel / jnp.sum(psel, axis=-1, keepdims=True) * g_w


def _mix(x2d, a2d, r2d, kc, vc, p, batch, seq, tile):
    n_tok, d = x2d.shape
    half = a2d.shape[-1]
    tiles_per_b = seq // tile
    tok = lambda i: (i, 0)
    mem_blk = pl.BlockSpec((1,) + kc.shape[1:], lambda i: (i // tiles_per_b, 0, 0))
    consts = [p['w_out_a'], p['w_out_r'], p['cross_norm_g'], p['cross_w_q'], p['cross_q_g'], p['cross_w_o'],
              p['moe_norm_g'], p['w_router'], p['b_router']]
    return pl.pallas_call(
        _mix_kernel,
        grid=(n_tok // tile,),
        in_specs=[pl.BlockSpec((tile, d), tok), pl.BlockSpec((tile, half), tok), pl.BlockSpec((tile, half), tok),
                  mem_blk, mem_blk] + [_const_spec(c.shape) for c in consts],
        out_specs=(pl.BlockSpec((tile, d), tok), pl.BlockSpec((tile, d), tok), pl.BlockSpec((tile, LANES), tok)),
        out_shape=(jax.ShapeDtypeStruct((n_tok, d), F32), jax.ShapeDtypeStruct((n_tok, d), BF16),
                   jax.ShapeDtypeStruct((n_tok, LANES), F32)),
        compiler_params=pltpu.CompilerParams(dimension_semantics=("parallel",),
                                             vmem_limit_bytes=VMEM_LIMIT_BYTES),
        name="mix",
    )(x2d, a2d, r2d, kc, vc, *consts)


def _moe_kernel(hm_ref, gate_ref, x2_ref, wgu_ref, wd_ref, o_ref):
    e = pl.program_id(1)
    ff = wd_ref.shape[1]

    @pl.when(e == 0)
    def _():
        o_ref[...] = x2_ref[...]

    au = jnp.dot(hm_ref[...], wgu_ref[0], preferred_element_type=F32)
    a = au[:, :ff]
    u = au[:, ff:]
    lane = lax.broadcasted_iota(jnp.int32, gate_ref.shape, 1)
    g_e = jnp.sum(jnp.where(lane == e, gate_ref[...], 0.0), axis=-1, keepdims=True)
    act = (a * jax.nn.sigmoid(a) * u * g_e).astype(BF16)
    o_ref[...] += jnp.dot(act, wd_ref[0], preferred_element_type=F32)


def _moe(hm, gate, x2, w_gu, w_d, tile):
    n_tok, d = hm.shape
    n_exp, _, ff2 = w_gu.shape
    tok = lambda i, e: (i, 0)
    return pl.pallas_call(
        _moe_kernel,
        grid=(n_tok // tile, n_exp),
        in_specs=[pl.BlockSpec((tile, d), tok), pl.BlockSpec((tile, LANES), tok), pl.BlockSpec((tile, d), tok),
                  pl.BlockSpec((1, d, ff2), lambda i, e: (e, 0, 0)),
                  pl.BlockSpec((1, ff2 // 2, d), lambda i, e: (e, 0, 0))],
        out_specs=pl.BlockSpec((tile, d), tok),
        out_shape=jax.ShapeDtypeStruct((n_tok, d), F32),
        compiler_params=pltpu.CompilerParams(dimension_semantics=("parallel", "arbitrary"),
                                             vmem_limit_bytes=VMEM_LIMIT_BYTES),
        name="moe",
    )(hm, gate, x2, w_gu, w_d)


def _row(v):
    return v.reshape(1, -1).astype(F32)


def _pad_lanes(v, width=LANES):
    return jnp.pad(v, [(0, 0)] * (v.ndim - 1) + [(0, width - v.shape[-1])])


def _swap_halves(v):
    half = v.shape[-1] // 2
    return jnp.concatenate([v[..., half:], v[..., :half]], axis=-1)


def _layout_params(l, attn_norm_g, w_in, mla_q_norm_g, mla_w_uq, mla_kv_norm_g, mla_w_ukv, mla_q_qk_g,
                   mla_k_qk_g, ret_gn_g, w_out, cross_norm_g, mem_norm_g, cross_w_q, cross_w_kv, cross_q_qk_g,
                   cross_k_qk_g, cross_w_o, moe_norm_g, router_w_group, router_b_group, router_w_expert,
                   router_b_expert):
    p = {}
    d = w_in.shape[1]
    wi = w_in[l]
    c_q, c_kv, k_pe, rq, rk, rv, rg = jnp.split(
        wi, np.cumsum([MLA_Q_RANK, MLA_KV_RANK, MLA_ROPE_DIM, RET_WIDTH, RET_WIDTH, RET_WIDTH]).tolist(), axis=1)
    p['w_in'] = jnp.concatenate([c_q, c_kv, rq, rk, rv, rg, _pad_lanes(k_pe), _pad_lanes(_swap_halves(k_pe))],
                                axis=1).astype(BF16)
    wq = mla_w_uq[l].reshape(MLA_Q_RANK, MLA_HEADS, MLA_QK_DIM)
    wq_rope = wq[:, :, MLA_NOPE_DIM:]
    p['w_uq'] = jnp.concatenate([wq[:, :, :MLA_NOPE_DIM], _pad_lanes(wq_rope), _pad_lanes(_swap_halves(wq_rope))],
                                axis=2).reshape(MLA_Q_RANK, MLA_HEADS * Q_HEAD_COLS).astype(BF16)
    wkv = mla_w_ukv[l].reshape(MLA_KV_RANK, MLA_HEADS, MLA_NOPE_DIM + MLA_V_DIM)
    p['w_ukv'] = jnp.concatenate([wkv[:, :, :MLA_NOPE_DIM].reshape(MLA_KV_RANK, -1),
                                  wkv[:, :, MLA_NOPE_DIM:].reshape(MLA_KV_RANK, -1)], axis=1).astype(BF16)
    p['attn_norm_g'] = _row(attn_norm_g[l])
    p['q_norm_g'] = _row(mla_q_norm_g[l])
    p['kv_norm_g'] = _row(mla_kv_norm_g[l])
    for name, g in (('q', mla_q_qk_g[l]), ('k', mla_k_qk_g[l])):
        p[name + '_g_nope'] = _row(g[:MLA_NOPE_DIM])
        p[name + '_g_rope'] = _row(_pad_lanes(g[MLA_NOPE_DIM:]))
        p[name + '_g_rope_sw'] = _row(_pad_lanes(_swap_halves(g[MLA_NOPE_DIM:])))
    hm_ = MLA_ROPE_DIM // 2
    f_m = ROPE_BASE ** (-jnp.arange(hm_, dtype=F32) / hm_)
    p['freq_m'] = _row(_pad_lanes(jnp.concatenate([f_m, f_m])))
    p['sign_m'] = _row(_pad_lanes(jnp.concatenate([-jnp.ones(hm_, F32), jnp.ones(hm_, F32)])))
    hr_ = RET_HEAD_DIM // 2
    f_r = ROPE_BASE ** (-jnp.arange(hr_, dtype=F32) / hr_)
    p['freq_r'] = _row(jnp.concatenate([f_r, f_r]))
    p['sign_r'] = _row(jnp.concatenate([-jnp.ones(hr_, F32), jnp.ones(hr_, F32)]))
    p['ret_gn_g'] = _row(ret_gn_g[l])
    half = MLA_HEADS * MLA_V_DIM
    p['w_out_a'] = w_out[l][:half].astype(BF16)
    p['w_out_r'] = w_out[l][half:].astype(BF16)
    p['cross_norm_g'] = _row(cross_norm_g[l])
    p['mem_norm_g'] = _row(mem_norm_g[l])
    p['cross_w_q'] = cross_w_q[l].astype(BF16)
    p['cross_w_kv'] = cross_w_kv[l].astype(BF16)
    p['cross_q_g'] = _row(cross_q_qk_g[l])
    p['cross_k_g'] = _row(cross_k_qk_g[l])
    p['cross_w_o'] = cross_w_o[l].astype(BF16)
    p['moe_norm_g'] = _row(moe_norm_g[l])
    p['w_router'] = _pad_lanes(jnp.concatenate([router_w_expert[l], router_w_group[l]], axis=1)).astype(F32)
    p['b_router'] = _row(_pad_lanes(jnp.concatenate([router_b_expert[l], router_b_group[l]])))
    return p


PREP_TILE = 512
FLASH_TQ = 512
FLASH_TK = 512
MIX_TILE = 512
MOE_TILE = 1024


def kernel(x, mem, positions, attn_norm_g, w_in, mla_q_norm_g, mla_w_uq, mla_kv_norm_g, mla_w_ukv, mla_q_qk_g, mla_k_qk_g, ret_gn_g, w_out, cross_norm_g, mem_norm_g, cross_w_q, cross_w_kv, cross_q_qk_g, cross_k_qk_g, cross_w_o, moe_norm_g, router_w_group, router_b_group, router_w_expert, router_b_expert, expert_w_gate, expert_w_up, expert_w_down):
    batch, seq, d = x.shape
    depth = w_in.shape[0]
    pos2d = positions.reshape(batch * seq, 1).astype(jnp.int32)
    x2d = x.reshape(batch * seq, d)
    for l in range(depth):
        p = _layout_params(l, attn_norm_g, w_in, mla_q_norm_g, mla_w_uq, mla_kv_norm_g, mla_w_ukv, mla_q_qk_g,
                           mla_k_qk_g, ret_gn_g, w_out, cross_norm_g, mem_norm_g, cross_w_q, cross_w_kv,
                           cross_q_qk_g, cross_k_qk_g, cross_w_o, moe_norm_g, router_w_group, router_b_group,
                           router_w_expert, router_b_expert)
        w_gu = jnp.concatenate([expert_w_gate[l], expert_w_up[l]], axis=-1).astype(BF16)
        w_d = expert_w_down[l].astype(BF16)

        q, k, v, rq, rk, rv, rg = _prep(x2d, pos2d, p, batch, seq, PREP_TILE)
        a_out = _flash(q, k, v, FLASH_TQ, FLASH_TK)
        shp = (batch, seq, RET_WIDTH)
        r_out = _retention(rq.reshape(shp), rk.reshape(shp), rv.reshape(shp), rg.reshape(shp), p['ret_gn_g'])
        kc, vc = _memkv(mem, p['mem_norm_g'], p['cross_w_kv'], p['cross_k_g'])
        x2, hm, gate = _mix(x2d, a_out.reshape(batch * seq, -1), r_out.reshape(batch * seq, -1), kc, vc, p,
                            batch, seq, MIX_TILE)
        x2d = _moe(hm, gate, x2, w_gu, w_d, MOE_TILE)
    return x2d.reshape(batch, seq, d)
```

```python
import functools
import math

import numpy as np
import jax
import jax.numpy as jnp
from jax import lax
from jax.experimental import pallas as pl
from jax.experimental.pallas import tpu as pltpu

F32 = jnp.float32
BF16 = jnp.bfloat16

LANES = 128
VMEM_LIMIT_BYTES = 56 * 1024 * 1024

MLA_HEADS = 4
MLA_NOPE_DIM = 128
MLA_ROPE_DIM = 64
MLA_QK_DIM = MLA_NOPE_DIM + MLA_ROPE_DIM
MLA_V_DIM = 128
MLA_Q_RANK = 256
MLA_KV_RANK = 128
MLA_HEAD_PAD = 256
RET_HEADS = 4
RET_HEAD_DIM = 128
RET_WIDTH = RET_HEADS * RET_HEAD_DIM
RET_CHUNK = 128
CROSS_HEADS = 4
N_GROUPS = 4
EXPERTS_PER_GROUP = 8
N_EXPERTS = 32
ROPE_BASE = 10000.0
EPS = 1e-6
NEG_INF = -1e30

_C_Q = 0
_C_KV = _C_Q + MLA_Q_RANK
_RQ = _C_KV + MLA_KV_RANK
_RK = _RQ + RET_WIDTH
_RV = _RK + RET_WIDTH
_RG = _RV + RET_WIDTH
_KPE = _RG + RET_WIDTH
_KPE_SW = _KPE + LANES
IN_COLS_PAD = _KPE_SW + LANES
Q_HEAD_COLS = 3 * LANES


def _rms(x, g):
    return x * lax.rsqrt(jnp.mean(x * x, axis=-1, keepdims=True) + EPS) * g


def _const_spec(shape):
    nd = len(shape)
    return pl.BlockSpec(shape, lambda *_: (0,) * nd, pipeline_mode=pl.Buffered(1))


def _prep_kernel(x_ref, pos_ref, g_ref, win_ref, qng_ref, wuq_ref, kvng_ref, wukv_ref,
                 qgn_ref, qgr_ref, qgs_ref, kgn_ref, kgr_ref, kgs_ref,
                 fm_ref, sm_ref, fr_ref, sr_ref,
                 q_out, k_out, v_out, rq_out, rk_out, rv_out, rg_out):
    x = x_ref[...]
    h = _rms(x, g_ref[...]).astype(BF16)
    proj = jnp.dot(h, win_ref[...], preferred_element_type=F32)

    posf = pos_ref[...].astype(F32)
    ang_m = posf * fm_ref[...]
    cos_m = jnp.cos(ang_m)
    sin_m = jnp.sin(ang_m) * sm_ref[...]
    ang_r = posf * fr_ref[...]
    cos_r = jnp.cos(ang_r)
    sin_r = jnp.sin(ang_r) * sr_ref[...]

    c_q = proj[:, _C_Q:_C_Q + MLA_Q_RANK]
    qall = jnp.dot(_rms(c_q, qng_ref[...]).astype(BF16), wuq_ref[...], preferred_element_type=F32)
    q_scale = MLA_QK_DIM ** -0.5 * math.log2(math.e)
    for hd in range(MLA_HEADS):
        base = hd * Q_HEAD_COLS
        nope = qall[:, base:base + LANES]
        rope = qall[:, base + LANES:base + 2 * LANES]
        rope_sw = qall[:, base + 2 * LANES:base + 3 * LANES]
        ss = jnp.sum(nope * nope, axis=-1, keepdims=True) + jnp.sum(rope * rope, axis=-1, keepdims=True)
        r = lax.rsqrt(ss * (1.0 / MLA_QK_DIM) + EPS) * q_scale
        roped = rope * qgr_ref[...] * cos_m + rope_sw * qgs_ref[...] * sin_m
        q_out[0, hd, :, 0:LANES] = (nope * qgn_ref[...] * r).astype(BF16)
        q_out[0, hd, :, LANES:2 * LANES] = (roped * r).astype(BF16)

    c_kv = proj[:, _C_KV:_C_KV + MLA_KV_RANK]
    kv = jnp.dot(_rms(c_kv, kvng_ref[...]).astype(BF16), wukv_ref[...], preferred_element_type=F32)
    k_pe = proj[:, _KPE:_KPE + LANES]
    k_pe_sw = proj[:, _KPE_SW:_KPE_SW + LANES]
    kr = k_pe * kgr_ref[...] * cos_m + k_pe_sw * kgs_ref[...] * sin_m
    ss_pe = jnp.sum(k_pe * k_pe, axis=-1, keepdims=True)
    for hd in range(MLA_HEADS):
        k_nope = kv[:, hd * LANES:(hd + 1) * LANES]
        ss = jnp.sum(k_nope * k_nope, axis=-1, keepdims=True) + ss_pe
        r = lax.rsqrt(ss * (1.0 / MLA_QK_DIM) + EPS)
        k_out[0, hd, :, 0:LANES] = (k_nope * kgn_ref[...] * r).astype(BF16)
        k_out[0, hd, :, LANES:2 * LANES] = (kr * r).astype(BF16)
        v_out[0, hd] = kv[:, (MLA_HEADS + hd) * LANES:(MLA_HEADS + hd + 1) * LANES].astype(BF16)

    k_scale = RET_HEAD_DIM ** -0.5
    for hd in range(RET_HEADS):
        sl = slice(hd * LANES, (hd + 1) * LANES)
        rq = proj[:, _RQ + hd * LANES:_RQ + (hd + 1) * LANES]
        rk = proj[:, _RK + hd * LANES:_RK + (hd + 1) * LANES]
        rq_out[:, sl] = (rq * cos_r + pltpu.roll(rq, LANES // 2, axis=1) * sin_r).astype(BF16)
        rk_out[:, sl] = ((rk * cos_r + pltpu.roll(rk, LANES // 2, axis=1) * sin_r) * k_scale).astype(BF16)
    rv_out[...] = proj[:, _RV:_RV + RET_WIDTH].astype(BF16)
    rg_out[...] = proj[:, _RG:_RG + RET_WIDTH].astype(BF16)


def _prep(x2d, pos2d, p, batch, seq, tile):
    n_tok, d = x2d.shape
    tiles_per_b = seq // tile
    tok = lambda i: (i, 0)
    head = lambda i: (i // tiles_per_b, 0, i % tiles_per_b, 0)
    consts = [p['attn_norm_g'], p['w_in'], p['q_norm_g'], p['w_uq'], p['kv_norm_g'], p['w_ukv'],
              p['q_g_nope'], p['q_g_rope'], p['q_g_rope_sw'], p['k_g_nope'], p['k_g_rope'], p['k_g_rope_sw'],
              p['freq_m'], p['sign_m'], p['freq_r'], p['sign_r']]
    out_shape = (
        jax.ShapeDtypeStruct((batch, MLA_HEADS, seq, MLA_HEAD_PAD), BF16),
        jax.ShapeDtypeStruct((batch, MLA_HEADS, seq, MLA_HEAD_PAD), BF16),
        jax.ShapeDtypeStruct((batch, MLA_HEADS, seq, MLA_V_DIM), BF16),
        jax.ShapeDtypeStruct((n_tok, RET_WIDTH), BF16),
        jax.ShapeDtypeStruct((n_tok, RET_WIDTH), BF16),
        jax.ShapeDtypeStruct((n_tok, RET_WIDTH), BF16),
        jax.ShapeDtypeStruct((n_tok, RET_WIDTH), BF16),
    )
    return pl.pallas_call(
        _prep_kernel,
        grid=(n_tok // tile,),
        in_specs=[pl.BlockSpec((tile, d), tok), pl.BlockSpec((tile, 1), tok)]
                 + [_const_spec(c.shape) for c in consts],
        out_specs=(
            pl.BlockSpec((1, MLA_HEADS, tile, MLA_HEAD_PAD), head),
            pl.BlockSpec((1, MLA_HEADS, tile, MLA_HEAD_PAD), head),
            pl.BlockSpec((1, MLA_HEADS, tile, MLA_V_DIM), head),
            pl.BlockSpec((tile, RET_WIDTH), tok),
            pl.BlockSpec((tile, RET_WIDTH), tok),
            pl.BlockSpec((tile, RET_WIDTH), tok),
            pl.BlockSpec((tile, RET_WIDTH), tok),
        ),
        out_shape=out_shape,
        compiler_params=pltpu.CompilerParams(dimension_semantics=("parallel",),
                                             vmem_limit_bytes=VMEM_LIMIT_BYTES),
        name="prep",
    )(x2d, pos2d, *consts)


def _flash_kernel(q_ref, k_ref, v_ref, o_ref, m_sc, l_sc, acc_sc, *, tq, tk, hb):
    qi = pl.program_id(2)
    m_sc[...] = jnp.full(m_sc.shape, NEG_INF, F32)
    l_sc[...] = jnp.zeros(l_sc.shape, F32)
    acc_sc[...] = jnp.zeros(acc_sc.shape, F32)
    reps = tk // LANES

    def step(j, mask):
        off = pl.multiple_of(j * tk, tk)
        for hd in range(hb):
            k = k_ref[0, hd, pl.ds(off, tk), :]
            v = v_ref[0, hd, pl.ds(off, tk), :]
            s = lax.dot_general(q_ref[0, hd], k, (((1,), (1,)), ((), ())), preferred_element_type=F32)
            if mask is not None:
                s = jnp.where(mask, s, NEG_INF)
            m_prev = m_sc[hd]
            m_new = jnp.maximum(m_prev, jnp.max(s, axis=-1, keepdims=True))
            alpha = jnp.exp2(m_prev - m_new)
            p = jnp.exp2(s - pltpu.repeat(m_new, reps, axis=1))
            p_cols = p[:, 0:LANES]
            for c in range(1, reps):
                p_cols = p_cols + p[:, c * LANES:(c + 1) * LANES]
            l_sc[hd] = alpha * l_sc[hd] + p_cols
            acc_sc[hd] = alpha * acc_sc[hd] + jnp.dot(p.astype(BF16), v, preferred_element_type=F32)
            m_sc[hd] = m_new

    n_full = qi * (tq // tk)

    def body(j, c):
        step(j, None)
        return c

    lax.fori_loop(0, n_full, body, 0)
    row = lax.broadcasted_iota(jnp.int32, (tq, tk), 0)
    col = lax.broadcasted_iota(jnp.int32, (tq, tk), 1)
    for d in range(tq // tk):
        step(n_full + d, col + d * tk <= row)
    for hd in range(hb):
        l = jnp.sum(l_sc[hd], axis=-1, keepdims=True)
        o_ref[0, :, hd * LANES:(hd + 1) * LANES] = (acc_sc[hd] / l).astype(o_ref.dtype)


def _flash(q, k, v, tq, tk, hb):
    batch, heads, seq, dqk = q.shape
    dv = v.shape[-1]
    return pl.pallas_call(
        functools.partial(_flash_kernel, tq=tq, tk=tk, hb=hb),
        grid=(batch, heads // hb, seq // tq),
        in_specs=[
            pl.BlockSpec((1, hb, tq, dqk), lambda b, h, i: (b, h, i, 0)),
            pl.BlockSpec((1, hb, seq, dqk), lambda b, h, i: (b, h, 0, 0)),
            pl.BlockSpec((1, hb, seq, dv), lambda b, h, i: (b, h, 0, 0)),
        ],
        out_specs=pl.BlockSpec((1, tq, hb * dv), lambda b, h, i: (b, i, h)),
        out_shape=jax.ShapeDtypeStruct((batch, seq, heads * dv), BF16),
        scratch_shapes=[pltpu.VMEM((hb, tq, LANES), F32), pltpu.VMEM((hb, tq, LANES), F32),
                        pltpu.VMEM((hb, tq, dv), F32)],
        compiler_params=pltpu.CompilerParams(dimension_semantics=("parallel", "parallel", "arbitrary"),
                                             vmem_limit_bytes=VMEM_LIMIT_BYTES),
        name="flash",
    )(q, k, v)


def _ret_kernel(q_ref, k_ref, v_ref, g_ref, gn_ref, o_ref, state_sc):
    c = pl.program_id(1)

    @pl.when(c == 0)
    def _():
        state_sc[...] = jnp.zeros(state_sc.shape, F32)

    n = RET_CHUNK
    ri = lax.broadcasted_iota(jnp.int32, (n, n), 0)
    ci = lax.broadcasted_iota(jnp.int32, (n, n), 1)
    diff = (ri - ci).astype(F32)
    idx = lax.broadcasted_iota(jnp.int32, (n, 1), 0).astype(F32)
    for hd in range(RET_HEADS):
        lg = math.log(1.0 - 2.0 ** (-5.0 - hd))
        intra = jnp.where(diff >= 0, jnp.exp(jnp.maximum(diff, 0.0) * lg), 0.0)
        q_decay = jnp.exp((idx + 1.0) * lg)
        k_decay = jnp.exp((n - 1.0 - idx) * lg)
        chunk_decay = math.exp(n * lg)
        sl = slice(hd * LANES, (hd + 1) * LANES)
        qc = q_ref[0, :, sl]
        kc = k_ref[0, :, sl]
        vc = v_ref[0, :, sl]
        state = state_sc[hd]
        scores = lax.dot_general(qc, kc, (((1,), (1,)), ((), ())), preferred_element_type=F32) * intra
        o = (jnp.dot(scores.astype(BF16), vc, preferred_element_type=F32)
             + jnp.dot((qc.astype(F32) * q_decay).astype(BF16), state.astype(BF16), preferred_element_type=F32))
        kd_t = (kc.astype(F32) * k_decay).T.astype(BF16)
        state_sc[hd] = state * chunk_decay + jnp.dot(kd_t, vc, preferred_element_type=F32)
        mu = jnp.mean(o, axis=-1, keepdims=True)
        cen = o - mu
        var = jnp.mean(cen * cen, axis=-1, keepdims=True)
        gate = g_ref[0, :, sl].astype(F32)
        y = cen * lax.rsqrt(var + EPS) * gn_ref[:, sl] * (gate * jax.nn.sigmoid(gate))
        o_ref[0, :, sl] = y.astype(o_ref.dtype)


def _retention(rq, rk, rv, rg, gn_g):
    batch, seq, width = rq.shape
    blk = pl.BlockSpec((1, RET_CHUNK, width), lambda b, c: (b, c, 0))
    return pl.pallas_call(
        _ret_kernel,
        grid=(batch, seq // RET_CHUNK),
        in_specs=[blk, blk, blk, blk, _const_spec(gn_g.shape)],
        out_specs=blk,
        out_shape=jax.ShapeDtypeStruct((batch, seq, width), BF16),
        scratch_shapes=[pltpu.VMEM((RET_HEADS, RET_HEAD_DIM, RET_HEAD_DIM), F32)],
        compiler_params=pltpu.CompilerParams(dimension_semantics=("parallel", "arbitrary")),
        name="retention",
    )(rq, rk, rv, rg, gn_g)


def _memkv_kernel(mem_ref, g_ref, wkv_ref, kg_ref, k_out, v_out):
    d = mem_ref.shape[-1]
    hdim = d // CROSS_HEADS
    m = _rms(mem_ref[0], g_ref[...]).astype(BF16)
    kv = jnp.dot(m, wkv_ref[...], preferred_element_type=F32)
    for hd in range(CROSS_HEADS):
        sl = slice(hd * hdim, (hd + 1) * hdim)
        k_out[0, :, sl] = _rms(kv[:, sl], kg_ref[...]).astype(BF16)
    v_out[0] = kv[:, d:].astype(BF16)


def _memkv(mem, mem_norm_g, w_kv, k_g):
    batch, m_tok, d = mem.shape
    blk = pl.BlockSpec((1, m_tok, d), lambda b: (b, 0, 0))
    return pl.pallas_call(
        _memkv_kernel,
        grid=(batch,),
        in_specs=[blk, _const_spec(mem_norm_g.shape), _const_spec(w_kv.shape), _const_spec(k_g.shape)],
        out_specs=(blk, blk),
        out_shape=(jax.ShapeDtypeStruct((batch, m_tok, d), BF16), jax.ShapeDtypeStruct((batch, m_tok, d), BF16)),
        compiler_params=pltpu.CompilerParams(dimension_semantics=("parallel",),
                                             vmem_limit_bytes=VMEM_LIMIT_BYTES),
        name="memkv",
    )(mem, mem_norm_g, w_kv, k_g)


def _mix_rows(x, a, r, kc_ref, vc_ref, woa_ref, wor_ref, cng_ref, wq_ref, qg_ref, wo_ref, mng_ref, wr_ref,
              br_ref):
    d = x.shape[-1]
    hdim = d // CROSS_HEADS
    x1 = (x
          + jnp.dot(a, woa_ref[...], preferred_element_type=F32)
          + jnp.dot(r, wor_ref[...], preferred_element_type=F32))
    hc = _rms(x1, cng_ref[...]).astype(BF16)
    q = jnp.dot(hc, wq_ref[...], preferred_element_type=F32)
    scale = hdim ** -0.5
    heads = []
    for hd in range(CROSS_HEADS):
        sl = slice(hd * hdim, (hd + 1) * hdim)
        qh = (_rms(q[:, sl], qg_ref[...]) * scale).astype(BF16)
        s = lax.dot_general(qh, kc_ref[0, :, sl], (((1,), (1,)), ((), ())), preferred_element_type=F32)
        s = s - jnp.max(s, axis=-1, keepdims=True)
        e = jnp.exp(s)
        pr = e / jnp.sum(e, axis=-1, keepdims=True)
        heads.append(jnp.dot(pr.astype(BF16), vc_ref[0, :, sl], preferred_element_type=F32).astype(BF16))
    o = jnp.concatenate(heads, axis=-1)
    x2 = x1 + jnp.dot(o, wo_ref[...], preferred_element_type=F32)
    hm = _rms(x2, mng_ref[...])

    hm_hi = hm.astype(BF16)
    hm_lo = (hm - hm_hi.astype(F32)).astype(BF16)
    logits = jnp.dot(jnp.concatenate([hm_hi, hm_lo, hm_hi], axis=-1), wr_ref[...], preferred_element_type=F32)
    biased = logits + br_ref[...]
    lane = lax.broadcasted_iota(jnp.int32, logits.shape, 1)
    lane_f = lane.astype(F32)
    big = float(4 * LANES)
    is_g = (lane >= N_EXPERTS) & (lane < N_EXPERTS + N_GROUPS)
    gl = jnp.where(is_g, logits, NEG_INF)
    gexp = jnp.where(is_g, jnp.exp(gl - jnp.max(gl, axis=-1, keepdims=True)), 0.0)
    gb = jnp.where(is_g, biased, NEG_INF)
    g_lane = jnp.min(jnp.where(gb == jnp.max(gb, axis=-1, keepdims=True), lane_f, big), axis=-1, keepdims=True)
    g_w = (jnp.sum(jnp.where(lane_f == g_lane, gexp, 0.0), axis=-1, keepdims=True)
           / jnp.sum(gexp, axis=-1, keepdims=True))
    g_idx = g_lane.astype(jnp.int32) - N_EXPERTS
    in_grp = (lane < N_EXPERTS) & ((lane // EXPERTS_PER_GROUP) == g_idx)
    el = jnp.where(in_grp, logits, NEG_INF)
    eexp = jnp.where(in_grp, jnp.exp(el - jnp.max(el, axis=-1, keepdims=True)), 0.0)
    eb = jnp.where(in_grp, biased, NEG_INF)
    i1 = jnp.min(jnp.where(eb == jnp.max(eb, axis=-1, keepdims=True), lane_f, big), axis=-1, keepdims=True)
    eb2 = jnp.where(lane_f == i1, NEG_INF, eb)
    i2 = jnp.min(jnp.where(eb2 == jnp.max(eb2, axis=-1, keepdims=True), lane_f, big), axis=-1, keepdims=True)
    psel = jnp.where((lane_f == i1) | (lane_f == i2), eexp, 0.0)
    gate = psel / jnp.sum(psel, axis=-1, keepdims=True) * g_w
    return x2, hm.astype(BF16), gate


def _mix_kernel(x_ref, a_ref, r_ref, *rest, chains):
    *w_refs, x2_out, hm_out, gate_out = rest
    rows = x_ref.shape[0] // chains
    for c in range(chains):
        sl = slice(c * rows, (c + 1) * rows)
        x2, hm, gate = _mix_rows(x_ref[sl, :], a_ref[sl, :], r_ref[sl, :], *w_refs)
        x2_out[sl, :] = x2
        hm_out[sl, :] = hm
        gate_out[sl, :] = gate


def _mix(x2d, a2d, r2d, kc, vc, p, batch, seq, tile):
    n_tok, d = x2d.shape
    half = a2d.shape[-1]
    tiles_per_b = seq // tile
    tok = lambda i: (i, 0)
    mem_blk = pl.BlockSpec((1,) + kc.shape[1:], lambda i: (i // tiles_per_b, 0, 0))
    consts = [p['w_out_a'], p['w_out_r'], p['cross_norm_g'], p['cross_w_q'], p['cross_q_g'], p['cross_w_o'],
              p['moe_norm_g'], p['w_router'], p['b_router']]
    return pl.pallas_call(
        functools.partial(_mix_kernel, chains=MIX_CHAINS),
        grid=(n_tok // tile,),
        in_specs=[pl.BlockSpec((tile, d), tok), pl.BlockSpec((tile, half), tok), pl.BlockSpec((tile, half), tok),
                  mem_blk, mem_blk] + [_const_spec(c.shape) for c in consts],
        out_specs=(pl.BlockSpec((tile, d), tok), pl.BlockSpec((tile, d), tok), pl.BlockSpec((tile, LANES), tok)),
        out_shape=(jax.ShapeDtypeStruct((n_tok, d), F32), jax.ShapeDtypeStruct((n_tok, d), BF16),
                   jax.ShapeDtypeStruct((n_tok, LANES), F32)),
        compiler_params=pltpu.CompilerParams(dimension_semantics=("parallel",),
                                             vmem_limit_bytes=VMEM_LIMIT_BYTES),
        name="mix",
    )(x2d, a2d, r2d, kc, vc, *consts)


def _moe_kernel(hm_ref, gate_ref, x2_ref, wgu_ref, wd_ref, o_ref, *, group):
    step = pl.program_id(1)
    ff = wgu_ref.shape[2] // 2

    @pl.when(step == 0)
    def _():
        o_ref[...] = x2_ref[...]

    hm = hm_ref[...]
    gate = gate_ref[...]
    lane = lax.broadcasted_iota(jnp.int32, gate.shape, 1)
    acts = []
    for g in range(group):
        au = jnp.dot(hm, wgu_ref[g], preferred_element_type=F32)
        a = au[:, :ff]
        u = au[:, ff:]
        g_e = jnp.sum(jnp.where(lane == step * group + g, gate, 0.0), axis=-1, keepdims=True)
        acts.append((a * jax.nn.sigmoid(a) * u * g_e).astype(BF16))
    act = jnp.concatenate(acts, axis=-1)
    o_ref[...] += jnp.dot(act, wd_ref[...], preferred_element_type=F32)


def _moe(hm, gate, x2, w_gu, w_d, tile, group):
    n_tok, d = hm.shape
    n_exp, _, ff2 = w_gu.shape
    ff = ff2 // 2
    tok = lambda i, e: (i, 0)
    return pl.pallas_call(
        functools.partial(_moe_kernel, group=group),
        grid=(n_tok // tile, n_exp // group),
        in_specs=[pl.BlockSpec((tile, d), tok), pl.BlockSpec((tile, LANES), tok), pl.BlockSpec((tile, d), tok),
                  pl.BlockSpec((group, d, ff2), lambda i, e: (e, 0, 0)),
                  pl.BlockSpec((group * ff, d), lambda i, e: (e, 0))],
        out_specs=pl.BlockSpec((tile, d), tok),
        out_shape=jax.ShapeDtypeStruct((n_tok, d), F32),
        compiler_params=pltpu.CompilerParams(dimension_semantics=("parallel", "arbitrary"),
                                             vmem_limit_bytes=VMEM_LIMIT_BYTES),
        name="moe",
    )(hm, gate, x2, w_gu, w_d.reshape(n_exp * ff, d))


def _row(v):
    return v.reshape(1, -1).astype(F32)


def _pad_lanes(v, width=LANES):
    return jnp.pad(v, [(0, 0)] * (v.ndim - 1) + [(0, width - v.shape[-1])])


def _swap_halves(v):
    half = v.shape[-1] // 2
    return jnp.concatenate([v[..., half:], v[..., :half]], axis=-1)


def _layout_params(l, attn_norm_g, w_in, mla_q_norm_g, mla_w_uq, mla_kv_norm_g, mla_w_ukv, mla_q_qk_g,
                   mla_k_qk_g, ret_gn_g, w_out, cross_norm_g, mem_norm_g, cross_w_q, cross_w_kv, cross_q_qk_g,
                   cross_k_qk_g, cross_w_o, moe_norm_g, router_w_group, router_b_group, router_w_expert,
                   router_b_expert):
    p = {}
    d = w_in.shape[1]
    wi = w_in[l]
    c_q, c_kv, k_pe, rq, rk, rv, rg = jnp.split(
        wi, np.cumsum([MLA_Q_RANK, MLA_KV_RANK, MLA_ROPE_DIM, RET_WIDTH, RET_WIDTH, RET_WIDTH]).tolist(), axis=1)
    p['w_in'] = jnp.concatenate([c_q, c_kv, rq, rk, rv, rg, _pad_lanes(k_pe), _pad_lanes(_swap_halves(k_pe))],
                                axis=1).astype(BF16)
    wq = mla_w_uq[l].reshape(MLA_Q_RANK, MLA_HEADS, MLA_QK_DIM)
    wq_rope = wq[:, :, MLA_NOPE_DIM:]
    p['w_uq'] = jnp.concatenate([wq[:, :, :MLA_NOPE_DIM], _pad_lanes(wq_rope), _pad_lanes(_swap_halves(wq_rope))],
                                axis=2).reshape(MLA_Q_RANK, MLA_HEADS * Q_HEAD_COLS).astype(BF16)
    wkv = mla_w_ukv[l].reshape(MLA_KV_RANK, MLA_HEADS, MLA_NOPE_DIM + MLA_V_DIM)
    p['w_ukv'] = jnp.concatenate([wkv[:, :, :MLA_NOPE_DIM].reshape(MLA_KV_RANK, -1),
                                  wkv[:, :, MLA_NOPE_DIM:].reshape(MLA_KV_RANK, -1)], axis=1).astype(BF16)
    p['attn_norm_g'] = _row(attn_norm_g[l])
    p['q_norm_g'] = _row(mla_q_norm_g[l])
    p['kv_norm_g'] = _row(mla_kv_norm_g[l])
    for name, g in (('q', mla_q_qk_g[l]), ('k', mla_k_qk_g[l])):
        p[name + '_g_nope'] = _row(g[:MLA_NOPE_DIM])
        p[name + '_g_rope'] = _row(_pad_lanes(g[MLA_NOPE_DIM:]))
        p[name + '_g_rope_sw'] = _row(_pad_lanes(_swap_halves(g[MLA_NOPE_DIM:])))
    hm_ = MLA_ROPE_DIM // 2
    f_m = ROPE_BASE ** (-jnp.arange(hm_, dtype=F32) / hm_)
    p['freq_m'] = _row(_pad_lanes(jnp.concatenate([f_m, f_m])))
    p['sign_m'] = _row(_pad_lanes(jnp.concatenate([-jnp.ones(hm_, F32), jnp.ones(hm_, F32)])))
    hr_ = RET_HEAD_DIM // 2
    f_r = ROPE_BASE ** (-jnp.arange(hr_, dtype=F32) / hr_)
    p['freq_r'] = _row(jnp.concatenate([f_r, f_r]))
    p['sign_r'] = _row(jnp.concatenate([-jnp.ones(hr_, F32), jnp.ones(hr_, F32)]))
    p['ret_gn_g'] = _row(ret_gn_g[l])
    half = MLA_HEADS * MLA_V_DIM
    p['w_out_a'] = w_out[l][:half].astype(BF16)
    p['w_out_r'] = w_out[l][half:].astype(BF16)
    p['cross_norm_g'] = _row(cross_norm_g[l])
    p['mem_norm_g'] = _row(mem_norm_g[l])
    p['cross_w_q'] = cross_w_q[l].astype(BF16)
    p['cross_w_kv'] = cross_w_kv[l].astype(BF16)
    p['cross_q_g'] = _row(cross_q_qk_g[l])
    p['cross_k_g'] = _row(cross_k_qk_g[l])
    p['cross_w_o'] = cross_w_o[l].astype(BF16)
    p['moe_norm_g'] = _row(moe_norm_g[l])
    w_r = _pad_lanes(jnp.concatenate([router_w_expert[l], router_w_group[l]], axis=1)).astype(F32)
    w_r_hi = w_r.astype(BF16)
    w_r_lo = (w_r - w_r_hi.astype(F32)).astype(BF16)
    p['w_router'] = jnp.concatenate([w_r_hi, w_r_hi, w_r_lo], axis=0)
    p['b_router'] = _row(_pad_lanes(jnp.concatenate([router_b_expert[l], router_b_group[l]])))
    return p


PREP_TILE = 512
FLASH_TQ = 512
FLASH_TK = 512
FLASH_HEADS_PER_STEP = 4
MIX_TILE = 512
MIX_CHAINS = 2
MOE_TILE = 1024
MOE_EXPERTS_PER_STEP = 4


def kernel(x, mem, positions, attn_norm_g, w_in, mla_q_norm_g, mla_w_uq, mla_kv_norm_g, mla_w_ukv, mla_q_qk_g, mla_k_qk_g, ret_gn_g, w_out, cross_norm_g, mem_norm_g, cross_w_q, cross_w_kv, cross_q_qk_g, cross_k_qk_g, cross_w_o, moe_norm_g, router_w_group, router_b_group, router_w_expert, router_b_expert, expert_w_gate, expert_w_up, expert_w_down):
    batch, seq, d = x.shape
    depth = w_in.shape[0]
    pos2d = positions.reshape(batch * seq, 1).astype(jnp.int32)
    x2d = x.reshape(batch * seq, d)
    for l in range(depth):
        p = _layout_params(l, attn_norm_g, w_in, mla_q_norm_g, mla_w_uq, mla_kv_norm_g, mla_w_ukv, mla_q_qk_g,
                           mla_k_qk_g, ret_gn_g, w_out, cross_norm_g, mem_norm_g, cross_w_q, cross_w_kv,
                           cross_q_qk_g, cross_k_qk_g, cross_w_o, moe_norm_g, router_w_group, router_b_group,
                           router_w_expert, router_b_expert)
        w_gu = jnp.concatenate([expert_w_gate[l], expert_w_up[l]], axis=-1).astype(BF16)
        w_d = expert_w_down[l].astype(BF16)

        q, k, v, rq, rk, rv, rg = _prep(x2d, pos2d, p, batch, seq, PREP_TILE)
        a_out = _flash(q, k, v, FLASH_TQ, FLASH_TK, FLASH_HEADS_PER_STEP)
        shp = (batch, seq, RET_WIDTH)
        r_out = _retention(rq.reshape(shp), rk.reshape(shp), rv.reshape(shp), rg.reshape(shp), p['ret_gn_g'])
        kc, vc = _memkv(mem, p['mem_norm_g'], p['cross_w_kv'], p['cross_k_g'])
        x2, hm, gate = _mix(x2d, a_out.reshape(batch * seq, -1), r_out.reshape(batch * seq, -1), kc, vc, p,
                            batch, seq, MIX_TILE)
        x2d = _moe(hm, gate, x2, w_gu, w_d, MOE_TILE, MOE_EXPERTS_PER_STEP)
    return x2d.reshape(batch, seq, d)
```

```python
import functools
import math

import numpy as np
import jax
import jax.numpy as jnp
from jax import lax
from jax.experimental import pallas as pl
from jax.experimental.pallas import tpu as pltpu

F32 = jnp.float32
BF16 = jnp.bfloat16

LANES = 128
VMEM_LIMIT_BYTES = 56 * 1024 * 1024

MLA_HEADS = 4
MLA_NOPE_DIM = 128
MLA_ROPE_DIM = 64
MLA_QK_DIM = MLA_NOPE_DIM + MLA_ROPE_DIM
MLA_V_DIM = 128
MLA_Q_RANK = 256
MLA_KV_RANK = 128
MLA_HEAD_PAD = 256
RET_HEADS = 4
RET_HEAD_DIM = 128
RET_WIDTH = RET_HEADS * RET_HEAD_DIM
RET_CHUNK = 128
CROSS_HEADS = 4
N_GROUPS = 4
EXPERTS_PER_GROUP = 8
N_EXPERTS = 32
ROPE_BASE = 10000.0
EPS = 1e-6
NEG_INF = -1e30

_C_Q = 0
_C_KV = _C_Q + MLA_Q_RANK
_RQ = _C_KV + MLA_KV_RANK
_RK = _RQ + RET_WIDTH
_RV = _RK + RET_WIDTH
_RG = _RV + RET_WIDTH
_KPE = _RG + RET_WIDTH
_KPE_SW = _KPE + LANES
IN_COLS_PAD = _KPE_SW + LANES
Q_HEAD_COLS = 3 * LANES


def _rms(x, g):
    return x * lax.rsqrt(jnp.mean(x * x, axis=-1, keepdims=True) + EPS) * g


def _const_spec(shape):
    nd = len(shape)
    return pl.BlockSpec(shape, lambda *_: (0,) * nd, pipeline_mode=pl.Buffered(1))


def _prep_kernel(x_ref, pos_ref, g_ref, win_ref, qng_ref, wuq_ref, kvng_ref, wukv_ref,
                 qgn_ref, qgr_ref, qgs_ref, kgn_ref, kgr_ref, kgs_ref,
                 fm_ref, sm_ref, fr_ref, sr_ref,
                 q_out, k_out, v_out, rq_out, rk_out, rv_out, rg_out):
    x = x_ref[...]
    h = _rms(x, g_ref[...]).astype(BF16)
    proj = jnp.dot(h, win_ref[...], preferred_element_type=F32)

    posf = pos_ref[...].astype(F32)
    ang_m = posf * fm_ref[...]
    cos_m = jnp.cos(ang_m)
    sin_m = jnp.sin(ang_m) * sm_ref[...]
    ang_r = posf * fr_ref[...]
    cos_r = jnp.cos(ang_r)
    sin_r = jnp.sin(ang_r) * sr_ref[...]

    c_q = proj[:, _C_Q:_C_Q + MLA_Q_RANK]
    qall = jnp.dot(_rms(c_q, qng_ref[...]).astype(BF16), wuq_ref[...], preferred_element_type=F32)
    q_scale = MLA_QK_DIM ** -0.5 * math.log2(math.e)
    for hd in range(MLA_HEADS):
        base = hd * Q_HEAD_COLS
        nope = qall[:, base:base + LANES]
        rope = qall[:, base + LANES:base + 2 * LANES]
        rope_sw = qall[:, base + 2 * LANES:base + 3 * LANES]
        ss = jnp.sum(nope * nope, axis=-1, keepdims=True) + jnp.sum(rope * rope, axis=-1, keepdims=True)
        r = lax.rsqrt(ss * (1.0 / MLA_QK_DIM) + EPS) * q_scale
        roped = rope * qgr_ref[...] * cos_m + rope_sw * qgs_ref[...] * sin_m
        q_out[0, hd, :, 0:LANES] = (nope * qgn_ref[...] * r).astype(BF16)
        q_out[0, hd, :, LANES:2 * LANES] = (roped * r).astype(BF16)

    c_kv = proj[:, _C_KV:_C_KV + MLA_KV_RANK]
    kv = jnp.dot(_rms(c_kv, kvng_ref[...]).astype(BF16), wukv_ref[...], preferred_element_type=F32)
    k_pe = proj[:, _KPE:_KPE + LANES]
    k_pe_sw = proj[:, _KPE_SW:_KPE_SW + LANES]
    kr = k_pe * kgr_ref[...] * cos_m + k_pe_sw * kgs_ref[...] * sin_m
    ss_pe = jnp.sum(k_pe * k_pe, axis=-1, keepdims=True)
    for hd in range(MLA_HEADS):
        k_nope = kv[:, hd * LANES:(hd + 1) * LANES]
        ss = jnp.sum(k_nope * k_nope, axis=-1, keepdims=True) + ss_pe
        r = lax.rsqrt(ss * (1.0 / MLA_QK_DIM) + EPS)
        k_out[0, hd, :, 0:LANES] = (k_nope * kgn_ref[...] * r).astype(BF16)
        k_out[0, hd, :, LANES:2 * LANES] = (kr * r).astype(BF16)
        v_out[0, hd] = kv[:, (MLA_HEADS + hd) * LANES:(MLA_HEADS + hd + 1) * LANES].astype(BF16)

    k_scale = RET_HEAD_DIM ** -0.5
    for hd in range(RET_HEADS):
        sl = slice(hd * LANES, (hd + 1) * LANES)
        rq = proj[:, _RQ + hd * LANES:_RQ + (hd + 1) * LANES]
        rk = proj[:, _RK + hd * LANES:_RK + (hd + 1) * LANES]
        rq_out[:, sl] = (rq * cos_r + pltpu.roll(rq, LANES // 2, axis=1) * sin_r).astype(BF16)
        rk_out[:, sl] = ((rk * cos_r + pltpu.roll(rk, LANES // 2, axis=1) * sin_r) * k_scale).astype(BF16)
    rv_out[...] = proj[:, _RV:_RV + RET_WIDTH].astype(BF16)
    rg_out[...] = proj[:, _RG:_RG + RET_WIDTH].astype(BF16)


def _prep(x2d, pos2d, p, batch, seq, tile):
    n_tok, d = x2d.shape
    tiles_per_b = seq // tile
    tok = lambda i: (i, 0)
    head = lambda i: (i // tiles_per_b, 0, i % tiles_per_b, 0)
    consts = [p['attn_norm_g'], p['w_in'], p['q_norm_g'], p['w_uq'], p['kv_norm_g'], p['w_ukv'],
              p['q_g_nope'], p['q_g_rope'], p['q_g_rope_sw'], p['k_g_nope'], p['k_g_rope'], p['k_g_rope_sw'],
              p['freq_m'], p['sign_m'], p['freq_r'], p['sign_r']]
    out_shape = (
        jax.ShapeDtypeStruct((batch, MLA_HEADS, seq, MLA_HEAD_PAD), BF16),
        jax.ShapeDtypeStruct((batch, MLA_HEADS, seq, MLA_HEAD_PAD), BF16),
        jax.ShapeDtypeStruct((batch, MLA_HEADS, seq, MLA_V_DIM), BF16),
        jax.ShapeDtypeStruct((n_tok, RET_WIDTH), BF16),
        jax.ShapeDtypeStruct((n_tok, RET_WIDTH), BF16),
        jax.ShapeDtypeStruct((n_tok, RET_WIDTH), BF16),
        jax.ShapeDtypeStruct((n_tok, RET_WIDTH), BF16),
    )
    return pl.pallas_call(
        _prep_kernel,
        grid=(n_tok // tile,),
        in_specs=[pl.BlockSpec((tile, d), tok), pl.BlockSpec((tile, 1), tok)]
                 + [_const_spec(c.shape) for c in consts],
        out_specs=(
            pl.BlockSpec((1, MLA_HEADS, tile, MLA_HEAD_PAD), head),
            pl.BlockSpec((1, MLA_HEADS, tile, MLA_HEAD_PAD), head),
            pl.BlockSpec((1, MLA_HEADS, tile, MLA_V_DIM), head),
            pl.BlockSpec((tile, RET_WIDTH), tok),
            pl.BlockSpec((tile, RET_WIDTH), tok),
            pl.BlockSpec((tile, RET_WIDTH), tok),
            pl.BlockSpec((tile, RET_WIDTH), tok),
        ),
        out_shape=out_shape,
        compiler_params=pltpu.CompilerParams(dimension_semantics=("parallel",),
                                             vmem_limit_bytes=VMEM_LIMIT_BYTES),
        name="prep",
    )(x2d, pos2d, *consts)


def _flash_kernel(q_ref, k_ref, v_ref, o_ref, m_sc, l_sc, acc_sc, *, tq, tk, hb):
    qi = pl.program_id(2)
    m_sc[...] = jnp.full(m_sc.shape, NEG_INF, F32)
    l_sc[...] = jnp.zeros(l_sc.shape, F32)
    acc_sc[...] = jnp.zeros(acc_sc.shape, F32)
    reps = tk // LANES

    def step(j, mask):
        off = pl.multiple_of(j * tk, tk)
        for hd in range(hb):
            k = k_ref[0, hd, pl.ds(off, tk), :]
            v = v_ref[0, hd, pl.ds(off, tk), :]
            s = lax.dot_general(q_ref[0, hd], k, (((1,), (1,)), ((), ())), preferred_element_type=F32)
            if mask is not None:
                s = jnp.where(mask, s, NEG_INF)
            m_prev = m_sc[hd]
            m_new = jnp.maximum(m_prev, jnp.max(s, axis=-1, keepdims=True))
            alpha = jnp.exp2(m_prev - m_new)
            p = jnp.exp2(s - jnp.concatenate([m_new] * reps, axis=1))
            p_cols = p[:, 0:LANES]
            for c in range(1, reps):
                p_cols = p_cols + p[:, c * LANES:(c + 1) * LANES]
            l_sc[hd] = alpha * l_sc[hd] + p_cols
            acc_sc[hd] = alpha * acc_sc[hd] + jnp.dot(p.astype(BF16), v, preferred_element_type=F32)
            m_sc[hd] = m_new

    n_full = qi * (tq // tk)

    def body(j, c):
        step(j, None)
        return c

    lax.fori_loop(0, n_full, body, 0)
    row = lax.broadcasted_iota(jnp.int32, (tq, tk), 0)
    col = lax.broadcasted_iota(jnp.int32, (tq, tk), 1)
    for d in range(tq // tk):
        step(n_full + d, col + d * tk <= row)
    for hd in range(hb):
        l = jnp.sum(l_sc[hd], axis=-1, keepdims=True)
        o_ref[0, :, hd * LANES:(hd + 1) * LANES] = (acc_sc[hd] / l).astype(o_ref.dtype)


def _flash(q, k, v, tq, tk, hb):
    batch, heads, seq, dqk = q.shape
    dv = v.shape[-1]
    return pl.pallas_call(
        functools.partial(_flash_kernel, tq=tq, tk=tk, hb=hb),
        grid=(batch, heads // hb, seq // tq),
        in_specs=[
            pl.BlockSpec((1, hb, tq, dqk), lambda b, h, i: (b, h, i, 0)),
            pl.BlockSpec((1, hb, seq, dqk), lambda b, h, i: (b, h, 0, 0)),
            pl.BlockSpec((1, hb, seq, dv), lambda b, h, i: (b, h, 0, 0)),
        ],
        out_specs=pl.BlockSpec((1, tq, hb * dv), lambda b, h, i: (b, i, h)),
        out_shape=jax.ShapeDtypeStruct((batch, seq, heads * dv), BF16),
        scratch_shapes=[pltpu.VMEM((hb, tq, LANES), F32), pltpu.VMEM((hb, tq, LANES), F32),
                        pltpu.VMEM((hb, tq, dv), F32)],
        compiler_params=pltpu.CompilerParams(dimension_semantics=("parallel", "parallel", "arbitrary"),
                                             vmem_limit_bytes=VMEM_LIMIT_BYTES),
        name="flash",
    )(q, k, v)


def _ret_kernel(q_ref, k_ref, v_ref, g_ref, gn_ref, o_ref, state_sc):
    c = pl.program_id(1)

    @pl.when(c == 0)
    def _():
        state_sc[...] = jnp.zeros(state_sc.shape, F32)

    n = RET_CHUNK
    ri = lax.broadcasted_iota(jnp.int32, (n, n), 0)
    ci = lax.broadcasted_iota(jnp.int32, (n, n), 1)
    diff = (ri - ci).astype(F32)
    idx = lax.broadcasted_iota(jnp.int32, (n, 1), 0).astype(F32)
    for hd in range(RET_HEADS):
        lg = math.log(1.0 - 2.0 ** (-5.0 - hd))
        intra = jnp.where(diff >= 0, jnp.exp(jnp.maximum(diff, 0.0) * lg), 0.0)
        q_decay = jnp.exp((idx + 1.0) * lg)
        k_decay = jnp.exp((n - 1.0 - idx) * lg)
        chunk_decay = math.exp(n * lg)
        sl = slice(hd * LANES, (hd + 1) * LANES)
        qc = q_ref[0, :, sl]
        kc = k_ref[0, :, sl]
        vc = v_ref[0, :, sl]
        state = state_sc[hd]
        scores = lax.dot_general(qc, kc, (((1,), (1,)), ((), ())), preferred_element_type=F32) * intra
        o = (jnp.dot(scores.astype(BF16), vc, preferred_element_type=F32)
             + jnp.dot((qc.astype(F32) * q_decay).astype(BF16), state.astype(BF16), preferred_element_type=F32))
        kd_t = (kc.astype(F32) * k_decay).T.astype(BF16)
        state_sc[hd] = state * chunk_decay + jnp.dot(kd_t, vc, preferred_element_type=F32)
        mu = jnp.mean(o, axis=-1, keepdims=True)
        cen = o - mu
        var = jnp.mean(cen * cen, axis=-1, keepdims=True)
        gate = g_ref[0, :, sl].astype(F32)
        y = cen * lax.rsqrt(var + EPS) * gn_ref[:, sl] * (gate * jax.nn.sigmoid(gate))
        o_ref[0, :, sl] = y.astype(o_ref.dtype)


def _retention(rq, rk, rv, rg, gn_g):
    batch, seq, width = rq.shape
    blk = pl.BlockSpec((1, RET_CHUNK, width), lambda b, c: (b, c, 0))
    return pl.pallas_call(
        _ret_kernel,
        grid=(batch, seq // RET_CHUNK),
        in_specs=[blk, blk, blk, blk, _const_spec(gn_g.shape)],
        out_specs=blk,
        out_shape=jax.ShapeDtypeStruct((batch, seq, width), BF16),
        scratch_shapes=[pltpu.VMEM((RET_HEADS, RET_HEAD_DIM, RET_HEAD_DIM), F32)],
        compiler_params=pltpu.CompilerParams(dimension_semantics=("parallel", "arbitrary")),
        name="retention",
    )(rq, rk, rv, rg, gn_g)


def _memkv_kernel(mem_ref, g_ref, wkv_ref, kg_ref, k_out, v_out):
    d = mem_ref.shape[-1]
    hdim = d // CROSS_HEADS
    m = _rms(mem_ref[0], g_ref[...]).astype(BF16)
    kv = jnp.dot(m, wkv_ref[...], preferred_element_type=F32)
    for hd in range(CROSS_HEADS):
        sl = slice(hd * hdim, (hd + 1) * hdim)
        k_out[0, :, sl] = _rms(kv[:, sl], kg_ref[...]).astype(BF16)
    v_out[0] = kv[:, d:].astype(BF16)


def _memkv(mem, mem_norm_g, w_kv, k_g):
    batch, m_tok, d = mem.shape
    blk = pl.BlockSpec((1, m_tok, d), lambda b: (b, 0, 0))
    return pl.pallas_call(
        _memkv_kernel,
        grid=(batch,),
        in_specs=[blk, _const_spec(mem_norm_g.shape), _const_spec(w_kv.shape), _const_spec(k_g.shape)],
        out_specs=(blk, blk),
        out_shape=(jax.ShapeDtypeStruct((batch, m_tok, d), BF16), jax.ShapeDtypeStruct((batch, m_tok, d), BF16)),
        compiler_params=pltpu.CompilerParams(dimension_semantics=("parallel",),
                                             vmem_limit_bytes=VMEM_LIMIT_BYTES),
        name="memkv",
    )(mem, mem_norm_g, w_kv, k_g)


def _mix_rows(x, a, r, kc_ref, vc_ref, woa_ref, wor_ref, cng_ref, wq_ref, qg_ref, wo_ref, mng_ref, wr_ref,
              br_ref):
    d = x.shape[-1]
    hdim = d // CROSS_HEADS
    x1 = (x
          + jnp.dot(a, woa_ref[...], preferred_element_type=F32)
          + jnp.dot(r, wor_ref[...], preferred_element_type=F32))
    hc = _rms(x1, cng_ref[...]).astype(BF16)
    q = jnp.dot(hc, wq_ref[...], preferred_element_type=F32)
    scale = hdim ** -0.5
    heads = []
    for hd in range(CROSS_HEADS):
        sl = slice(hd * hdim, (hd + 1) * hdim)
        qh = (_rms(q[:, sl], qg_ref[...]) * scale).astype(BF16)
        s = lax.dot_general(qh, kc_ref[0, :, sl], (((1,), (1,)), ((), ())), preferred_element_type=F32)
        s = s - jnp.max(s, axis=-1, keepdims=True)
        e = jnp.exp(s)
        pr = e / jnp.sum(e, axis=-1, keepdims=True)
        heads.append(jnp.dot(pr.astype(BF16), vc_ref[0, :, sl], preferred_element_type=F32).astype(BF16))
    o = jnp.concatenate(heads, axis=-1)
    x2 = x1 + jnp.dot(o, wo_ref[...], preferred_element_type=F32)
    hm = _rms(x2, mng_ref[...])

    hm_hi = hm.astype(BF16)
    hm_lo = (hm - hm_hi.astype(F32)).astype(BF16)
    logits = jnp.dot(jnp.concatenate([hm_hi, hm_lo, hm_hi], axis=-1), wr_ref[...], preferred_element_type=F32)
    biased = logits + br_ref[...]
    lane = lax.broadcasted_iota(jnp.int32, logits.shape, 1)
    lane_f = lane.astype(F32)
    big = float(4 * LANES)
    is_g = (lane >= N_EXPERTS) & (lane < N_EXPERTS + N_GROUPS)
    gl = jnp.where(is_g, logits, NEG_INF)
    gexp = jnp.where(is_g, jnp.exp(gl - jnp.max(gl, axis=-1, keepdims=True)), 0.0)
    gb = jnp.where(is_g, biased, NEG_INF)
    g_lane = jnp.min(jnp.where(gb == jnp.max(gb, axis=-1, keepdims=True), lane_f, big), axis=-1, keepdims=True)
    g_w = (jnp.sum(jnp.where(lane_f == g_lane, gexp, 0.0), axis=-1, keepdims=True)
           / jnp.sum(gexp, axis=-1, keepdims=True))
    g_idx = g_lane.astype(jnp.int32) - N_EXPERTS
    in_grp = (lane < N_EXPERTS) & ((lane // EXPERTS_PER_GROUP) == g_idx)
    el = jnp.where(in_grp, logits, NEG_INF)
    eexp = jnp.where(in_grp, jnp.exp(el - jnp.max(el, axis=-1, keepdims=True)), 0.0)
    eb = jnp.where(in_grp, biased, NEG_INF)
    i1 = jnp.min(jnp.where(eb == jnp.max(eb, axis=-1, keepdims=True), lane_f, big), axis=-1, keepdims=True)
    eb2 = jnp.where(lane_f == i1, NEG_INF, eb)
    i2 = jnp.min(jnp.where(eb2 == jnp.max(eb2, axis=-1, keepdims=True), lane_f, big), axis=-1, keepdims=True)
    psel = jnp.where((lane_f == i1) | (lane_f == i2), eexp, 0.0)
    gate = psel / jnp.sum(psel, axis=-1, keepdims=True) * g_w
    grp_onehot = (lane == g_idx).astype(F32)
    return x2, hm.astype(BF16), gate, grp_onehot


def _group_sort(hm, gate, grp, hms_out, pos_out, meta_out):
    t = hm.shape[0]
    ts = hms_out.shape[0]
    earlier = (lax.broadcasted_iota(jnp.int32, (t, t), 1) < lax.broadcasted_iota(jnp.int32, (t, t), 0))
    cnt = jnp.dot(earlier.astype(BF16), grp.astype(BF16), preferred_element_type=F32)
    n = jnp.sum(grp, axis=0, keepdims=True)
    n_al = jnp.floor((n + (SORT_ALIGN - 1.0)) * (1.0 / SORT_ALIGN)) * SORT_ALIGN
    lane1 = lax.broadcasted_iota(jnp.int32, (1, LANES), 1)
    start = jnp.zeros((1, LANES), F32)
    run = jnp.zeros((1, 1), F32)
    for g in range(1, N_GROUPS):
        run = run + jnp.sum(jnp.where(lane1 == g - 1, n_al, 0.0), axis=-1, keepdims=True)
        start = jnp.where(lane1 == g, run, start)
    pos = jnp.sum(grp * (cnt + start), axis=-1, keepdims=True)
    pos_out[...] = pos
    row8 = lax.broadcasted_iota(jnp.int32, (8, LANES), 0)
    meta_out[0] = jnp.where(row8 == 0, start, jnp.where(row8 == 1, n, 0.0)).astype(jnp.int32)
    pos_row = jnp.broadcast_to(pos, (t, LANES)).T[0:1, :]
    perm = (lax.broadcasted_iota(jnp.int32, (ts, t), 0).astype(F32) == pos_row).astype(BF16)
    g_hi = gate.astype(BF16).astype(F32)
    rem = gate - g_hi
    g_mid = rem.astype(BF16).astype(F32)
    g_lo = rem - g_mid
    gate_pieces = (g_hi + pltpu.roll(g_mid, N_EXPERTS, axis=1) + pltpu.roll(g_lo, 2 * N_EXPERTS, axis=1)).astype(BF16)
    hms_out[...] = jnp.dot(perm, jnp.concatenate([hm, gate_pieces], axis=-1),
                           preferred_element_type=F32).astype(BF16)


def _mix_kernel(x_ref, a_ref, r_ref, *rest, chains):
    *w_refs, x2_out, hms_out, pos_out, meta_out, hm_sc, gate_sc, grp_sc = rest
    rows = x_ref.shape[0] // chains
    for c in range(chains):
        sl = slice(c * rows, (c + 1) * rows)
        x2, hm, gate, grp = _mix_rows(x_ref[sl, :], a_ref[sl, :], r_ref[sl, :], *w_refs)
        x2_out[sl, :] = x2
        hm_sc[sl, :] = hm
        gate_sc[sl, :] = gate
        grp_sc[sl, :] = grp
    _group_sort(hm_sc[...], gate_sc[...], grp_sc[...], hms_out, pos_out, meta_out)


def _mix(x2d, a2d, r2d, kc, vc, p, batch, seq, tile):
    n_tok, d = x2d.shape
    half = a2d.shape[-1]
    tiles_per_b = seq // tile
    n_tiles = n_tok // tile
    tok = lambda i: (i, 0)
    mem_blk = pl.BlockSpec((1,) + kc.shape[1:], lambda i: (i // tiles_per_b, 0, 0))
    consts = [p['w_out_a'], p['w_out_r'], p['cross_norm_g'], p['cross_w_q'], p['cross_q_g'], p['cross_w_o'],
              p['moe_norm_g'], p['w_router'], p['b_router']]
    return pl.pallas_call(
        functools.partial(_mix_kernel, chains=MIX_CHAINS),
        grid=(n_tiles,),
        in_specs=[pl.BlockSpec((tile, d), tok), pl.BlockSpec((tile, half), tok), pl.BlockSpec((tile, half), tok),
                  mem_blk, mem_blk] + [_const_spec(c.shape) for c in consts],
        out_specs=(pl.BlockSpec((tile, d), tok), pl.BlockSpec((SORT_SLOTS, d + LANES), tok),
                   pl.BlockSpec((tile, 1), tok), pl.BlockSpec((1, 8, LANES), lambda i: (i, 0, 0))),
        out_shape=(jax.ShapeDtypeStruct((n_tok, d), F32),
                   jax.ShapeDtypeStruct((n_tiles * SORT_SLOTS, d + LANES), BF16),
                   jax.ShapeDtypeStruct((n_tok, 1), F32),
                   jax.ShapeDtypeStruct((n_tiles, 8, LANES), jnp.int32)),
        scratch_shapes=[pltpu.VMEM((tile, d), BF16), pltpu.VMEM((tile, LANES), F32), pltpu.VMEM((tile, LANES), F32)],
        compiler_params=pltpu.CompilerParams(dimension_semantics=("parallel",),
                                             vmem_limit_bytes=VMEM_LIMIT_BYTES),
        name="mix",
    )(x2d, a2d, r2d, kc, vc, *consts)


def _moe_kernel(start_ref, count_ref, hms_ref, wgu_ref, wd_ref, o_ref, *, group, tiles, chunk):
    blk = pl.program_id(0)
    step = pl.program_id(1)
    d = wd_ref.shape[1]
    ff = wgu_ref.shape[2] // 2
    grp = step // (EXPERTS_PER_GROUP // group)

    @pl.when(step == 0)
    def _():
        o_ref[...] = jnp.zeros(o_ref.shape, o_ref.dtype)

    lane = lax.broadcasted_iota(jnp.int32, (chunk, LANES), 1)
    piece_lane = lane < 3 * N_EXPERTS

    def tile_body(tl, carry):
        idx = (blk * tiles + tl) * N_GROUPS + grp
        base = tl * SORT_SLOTS + start_ref[idx]
        n_chunks = (count_ref[idx] + chunk - 1) // chunk

        def chunk_body(c, carry2):
            r0 = pl.multiple_of(base + c * chunk, SORT_ALIGN)
            xs = hms_ref[pl.ds(r0, chunk), 0:d]
            gates = hms_ref[pl.ds(r0, chunk), d:d + LANES].astype(F32)
            acts = []
            for j in range(group):
                e = step * group + j
                au = jnp.dot(xs, wgu_ref[j], preferred_element_type=F32)
                a = au[:, :ff]
                u = au[:, ff:]
                g_e = jnp.sum(jnp.where(piece_lane & ((lane & (N_EXPERTS - 1)) == e), gates, 0.0),
                              axis=-1, keepdims=True)
                acts.append((a * jax.nn.sigmoid(a) * u * g_e).astype(BF16))
            y = jnp.dot(jnp.concatenate(acts, axis=-1), wd_ref[...], preferred_element_type=F32)
            o_ref[pl.ds(r0, chunk), :] = (o_ref[pl.ds(r0, chunk), :].astype(F32) + y).astype(o_ref.dtype)
            return carry2

        lax.fori_loop(0, n_chunks, chunk_body, 0)
        return carry

    lax.fori_loop(0, tiles, tile_body, 0)


def _moe(hms, starts, counts, w_gu, w_d, tiles, group, chunk):
    n_rows, dx = hms.shape
    n_exp, d, ff2 = w_gu.shape
    ff = ff2 // 2
    rows = tiles * SORT_SLOTS
    grid_spec = pltpu.PrefetchScalarGridSpec(
        num_scalar_prefetch=2,
        grid=(n_rows // rows, n_exp // group),
        in_specs=[pl.BlockSpec((rows, dx), lambda i, e, s, c: (i, 0)),
                  pl.BlockSpec((group, d, ff2), lambda i, e, s, c: (e, 0, 0)),
                  pl.BlockSpec((group * ff, d), lambda i, e, s, c: (e, 0))],
        out_specs=pl.BlockSpec((rows, d), lambda i, e, s, c: (i, 0)),
    )
    return pl.pallas_call(
        functools.partial(_moe_kernel, group=group, tiles=tiles, chunk=chunk),
        grid_spec=grid_spec,
        out_shape=jax.ShapeDtypeStruct((n_rows, d), BF16),
        compiler_params=pltpu.CompilerParams(dimension_semantics=("parallel", "arbitrary"),
                                             vmem_limit_bytes=VMEM_LIMIT_BYTES),
        name="moe",
    )(starts, counts, hms, w_gu, w_d.reshape(n_exp * ff, d))


def _unsort_kernel(x2_ref, pos_ref, ys_ref, o_ref):
    t = x2_ref.shape[0]
    ts = ys_ref.shape[0]
    unperm = (lax.broadcasted_iota(jnp.int32, (t, ts), 1).astype(F32) == pos_ref[...]).astype(BF16)
    o_ref[...] = x2_ref[...] + jnp.dot(unperm, ys_ref[...], preferred_element_type=F32)


def _unsort(x2, pos, ys, tile):
    n_tok, d = x2.shape
    tok = lambda i: (i, 0)
    return pl.pallas_call(
        _unsort_kernel,
        grid=(n_tok // tile,),
        in_specs=[pl.BlockSpec((tile, d), tok), pl.BlockSpec((tile, 1), tok), pl.BlockSpec((SORT_SLOTS, d), tok)],
        out_specs=pl.BlockSpec((tile, d), tok),
        out_shape=jax.ShapeDtypeStruct((n_tok, d), F32),
        compiler_params=pltpu.CompilerParams(dimension_semantics=("parallel",)),
        name="unsort",
    )(x2, pos, ys)


def _row(v):
    return v.reshape(1, -1).astype(F32)


def _pad_lanes(v, width=LANES):
    return jnp.pad(v, [(0, 0)] * (v.ndim - 1) + [(0, width - v.shape[-1])])


def _swap_halves(v):
    half = v.shape[-1] // 2
    return jnp.concatenate([v[..., half:], v[..., :half]], axis=-1)


def _layout_params(l, attn_norm_g, w_in, mla_q_norm_g, mla_w_uq, mla_kv_norm_g, mla_w_ukv, mla_q_qk_g,
                   mla_k_qk_g, ret_gn_g, w_out, cross_norm_g, mem_norm_g, cross_w_q, cross_w_kv, cross_q_qk_g,
                   cross_k_qk_g, cross_w_o, moe_norm_g, router_w_group, router_b_group, router_w_expert,
                   router_b_expert):
    p = {}
    d = w_in.shape[1]
    wi = w_in[l]
    c_q, c_kv, k_pe, rq, rk, rv, rg = jnp.split(
        wi, np.cumsum([MLA_Q_RANK, MLA_KV_RANK, MLA_ROPE_DIM, RET_WIDTH, RET_WIDTH, RET_WIDTH]).tolist(), axis=1)
    p['w_in'] = jnp.concatenate([c_q, c_kv, rq, rk, rv, rg, _pad_lanes(k_pe), _pad_lanes(_swap_halves(k_pe))],
                                axis=1).astype(BF16)
    wq = mla_w_uq[l].reshape(MLA_Q_RANK, MLA_HEADS, MLA_QK_DIM)
    wq_rope = wq[:, :, MLA_NOPE_DIM:]
    p['w_uq'] = jnp.concatenate([wq[:, :, :MLA_NOPE_DIM], _pad_lanes(wq_rope), _pad_lanes(_swap_halves(wq_rope))],
                                axis=2).reshape(MLA_Q_RANK, MLA_HEADS * Q_HEAD_COLS).astype(BF16)
    wkv = mla_w_ukv[l].reshape(MLA_KV_RANK, MLA_HEADS, MLA_NOPE_DIM + MLA_V_DIM)
    p['w_ukv'] = jnp.concatenate([wkv[:, :, :MLA_NOPE_DIM].reshape(MLA_KV_RANK, -1),
                                  wkv[:, :, MLA_NOPE_DIM:].reshape(MLA_KV_RANK, -1)], axis=1).astype(BF16)
    p['attn_norm_g'] = _row(attn_norm_g[l])
    p['q_norm_g'] = _row(mla_q_norm_g[l])
    p['kv_norm_g'] = _row(mla_kv_norm_g[l])
    for name, g in (('q', mla_q_qk_g[l]), ('k', mla_k_qk_g[l])):
        p[name + '_g_nope'] = _row(g[:MLA_NOPE_DIM])
        p[name + '_g_rope'] = _row(_pad_lanes(g[MLA_NOPE_DIM:]))
        p[name + '_g_rope_sw'] = _row(_pad_lanes(_swap_halves(g[MLA_NOPE_DIM:])))
    hm_ = MLA_ROPE_DIM // 2
    f_m = ROPE_BASE ** (-jnp.arange(hm_, dtype=F32) / hm_)
    p['freq_m'] = _row(_pad_lanes(jnp.concatenate([f_m, f_m])))
    p['sign_m'] = _row(_pad_lanes(jnp.concatenate([-jnp.ones(hm_, F32), jnp.ones(hm_, F32)])))
    hr_ = RET_HEAD_DIM // 2
    f_r = ROPE_BASE ** (-jnp.arange(hr_, dtype=F32) / hr_)
    p['freq_r'] = _row(jnp.concatenate([f_r, f_r]))
    p['sign_r'] = _row(jnp.concatenate([-jnp.ones(hr_, F32), jnp.ones(hr_, F32)]))
    p['ret_gn_g'] = _row(ret_gn_g[l])
    half = MLA_HEADS * MLA_V_DIM
    p['w_out_a'] = w_out[l][:half].astype(BF16)
    p['w_out_r'] = w_out[l][half:].astype(BF16)
    p['cross_norm_g'] = _row(cross_norm_g[l])
    p['mem_norm_g'] = _row(mem_norm_g[l])
    p['cross_w_q'] = cross_w_q[l].astype(BF16)
    p['cross_w_kv'] = cross_w_kv[l].astype(BF16)
    p['cross_q_g'] = _row(cross_q_qk_g[l])
    p['cross_k_g'] = _row(cross_k_qk_g[l])
    p['cross_w_o'] = cross_w_o[l].astype(BF16)
    p['moe_norm_g'] = _row(moe_norm_g[l])
    w_r = _pad_lanes(jnp.concatenate([router_w_expert[l], router_w_group[l]], axis=1)).astype(F32)
    w_r_hi = w_r.astype(BF16)
    w_r_lo = (w_r - w_r_hi.astype(F32)).astype(BF16)
    p['w_router'] = jnp.concatenate([w_r_hi, w_r_hi, w_r_lo], axis=0)
    p['b_router'] = _row(_pad_lanes(jnp.concatenate([router_b_expert[l], router_b_group[l]])))
    return p


PREP_TILE = 512
FLASH_TQ = 512
FLASH_TK = 512
FLASH_HEADS_PER_STEP = 4
MIX_TILE = 512
MIX_CHAINS = 2
SORT_ALIGN = 16
MOE_CHUNK = 128
SORT_SLOTS = -(-(MIX_TILE + N_GROUPS * (SORT_ALIGN - 1) + MOE_CHUNK) // LANES) * LANES
MOE_TILES_PER_BLOCK = 4
MOE_EXPERTS_PER_STEP = 4


def kernel(x, mem, positions, attn_norm_g, w_in, mla_q_norm_g, mla_w_uq, mla_kv_norm_g, mla_w_ukv, mla_q_qk_g, mla_k_qk_g, ret_gn_g, w_out, cross_norm_g, mem_norm_g, cross_w_q, cross_w_kv, cross_q_qk_g, cross_k_qk_g, cross_w_o, moe_norm_g, router_w_group, router_b_group, router_w_expert, router_b_expert, expert_w_gate, expert_w_up, expert_w_down):
    batch, seq, d = x.shape
    depth = w_in.shape[0]
    pos2d = positions.reshape(batch * seq, 1).astype(jnp.int32)
    x2d = x.reshape(batch * seq, d)
    for l in range(depth):
        p = _layout_params(l, attn_norm_g, w_in, mla_q_norm_g, mla_w_uq, mla_kv_norm_g, mla_w_ukv, mla_q_qk_g,
                           mla_k_qk_g, ret_gn_g, w_out, cross_norm_g, mem_norm_g, cross_w_q, cross_w_kv,
                           cross_q_qk_g, cross_k_qk_g, cross_w_o, moe_norm_g, router_w_group, router_b_group,
                           router_w_expert, router_b_expert)
        w_gu = jnp.concatenate([expert_w_gate[l], expert_w_up[l]], axis=-1).astype(BF16)
        w_d = expert_w_down[l].astype(BF16)

        q, k, v, rq, rk, rv, rg = _prep(x2d, pos2d, p, batch, seq, PREP_TILE)
        a_out = _flash(q, k, v, FLASH_TQ, FLASH_TK, FLASH_HEADS_PER_STEP)
        shp = (batch, seq, RET_WIDTH)
        r_out = _retention(rq.reshape(shp), rk.reshape(shp), rv.reshape(shp), rg.reshape(shp), p['ret_gn_g'])
        kc, vc = _memkv(mem, p['mem_norm_g'], p['cross_w_kv'], p['cross_k_g'])
        x2, hms, pos, meta = _mix(x2d, a_out.reshape(batch * seq, -1), r_out.reshape(batch * seq, -1), kc, vc, p,
                                  batch, seq, MIX_TILE)
        starts = meta[:, 0, :N_GROUPS].reshape(-1)
        counts = meta[:, 1, :N_GROUPS].reshape(-1)
        ys = _moe(hms, starts, counts, w_gu, w_d, MOE_TILES_PER_BLOCK, MOE_EXPERTS_PER_STEP, MOE_CHUNK)
        x2d = _unsort(x2, pos, ys, MIX_TILE)
    return x2d.reshape(batch, seq, d)
```

```python
import functools
import math

import numpy as np
import jax
import jax.numpy as jnp
from jax import lax
from jax.experimental import pallas as pl
from jax.experimental.pallas import tpu as pltpu

F32 = jnp.float32
BF16 = jnp.bfloat16

LANES = 128
VMEM_LIMIT_BYTES = 56 * 1024 * 1024

MLA_HEADS = 4
MLA_NOPE_DIM = 128
MLA_ROPE_DIM = 64
MLA_QK_DIM = MLA_NOPE_DIM + MLA_ROPE_DIM
MLA_V_DIM = 128
MLA_Q_RANK = 256
MLA_KV_RANK = 128
MLA_HEAD_PAD = 256
RET_HEADS = 4
RET_HEAD_DIM = 128
RET_WIDTH = RET_HEADS * RET_HEAD_DIM
RET_CHUNK = 256
CROSS_HEADS = 4
N_GROUPS = 4
EXPERTS_PER_GROUP = 8
N_EXPERTS = 32
ROPE_BASE = 10000.0
EPS = 1e-6
NEG_INF = -1e30

_C_Q = 0
_C_KV = _C_Q + MLA_Q_RANK
_RQ = _C_KV + MLA_KV_RANK
_RK = _RQ + RET_WIDTH
_RV = _RK + RET_WIDTH
_RG = _RV + RET_WIDTH
_KPE = _RG + RET_WIDTH
_KPE_SW = _KPE + LANES
IN_COLS_PAD = _KPE_SW + LANES
Q_HEAD_COLS = 3 * LANES


def _rms(x, g):
    return x * lax.rsqrt(jnp.mean(x * x, axis=-1, keepdims=True) + EPS) * g


def _const_spec(shape):
    nd = len(shape)
    return pl.BlockSpec(shape, lambda *_: (0,) * nd, pipeline_mode=pl.Buffered(1))


def _prep_rows(rs, x_ref, pos_ref, g_ref, win_ref, qng_ref, wuq_ref, kvng_ref, wukv_ref,
               qgn_ref, qgr_ref, qgs_ref, kgn_ref, kgr_ref, kgs_ref,
               fm_ref, sm_ref, fr_ref, sr_ref,
               q_out, k_out, v_out, rq_out, rk_out, rv_out, rg_out):
    x = x_ref[rs, :]
    h = _rms(x, g_ref[...]).astype(BF16)
    proj = jnp.dot(h, win_ref[...], preferred_element_type=F32)

    posf = pos_ref[rs, :].astype(F32)
    ang_m = posf * fm_ref[...]
    cos_m = jnp.cos(ang_m)
    sin_m = jnp.sin(ang_m) * sm_ref[...]
    ang_r = posf * fr_ref[...]
    cos_r = jnp.cos(ang_r)
    sin_r = jnp.sin(ang_r) * sr_ref[...]

    c_q = proj[:, _C_Q:_C_Q + MLA_Q_RANK]
    qall = jnp.dot(_rms(c_q, qng_ref[...]).astype(BF16), wuq_ref[...], preferred_element_type=F32)
    q_scale = MLA_QK_DIM ** -0.5 * math.log2(math.e)
    for hd in range(MLA_HEADS):
        base = hd * Q_HEAD_COLS
        nope = qall[:, base:base + LANES]
        rope = qall[:, base + LANES:base + 2 * LANES]
        rope_sw = qall[:, base + 2 * LANES:base + 3 * LANES]
        ss = jnp.sum(nope * nope, axis=-1, keepdims=True) + jnp.sum(rope * rope, axis=-1, keepdims=True)
        r = lax.rsqrt(ss * (1.0 / MLA_QK_DIM) + EPS) * q_scale
        roped = rope * qgr_ref[...] * cos_m + rope_sw * qgs_ref[...] * sin_m
        q_out[0, hd, rs,0:LANES] = (nope * qgn_ref[...] * r).astype(BF16)
        q_out[0, hd, rs,LANES:2 * LANES] = (roped * r).astype(BF16)

    c_kv = proj[:, _C_KV:_C_KV + MLA_KV_RANK]
    kv = jnp.dot(_rms(c_kv, kvng_ref[...]).astype(BF16), wukv_ref[...], preferred_element_type=F32)
    k_pe = proj[:, _KPE:_KPE + LANES]
    k_pe_sw = proj[:, _KPE_SW:_KPE_SW + LANES]
    kr = k_pe * kgr_ref[...] * cos_m + k_pe_sw * kgs_ref[...] * sin_m
    ss_pe = jnp.sum(k_pe * k_pe, axis=-1, keepdims=True)
    for hd in range(MLA_HEADS):
        k_nope = kv[:, hd * LANES:(hd + 1) * LANES]
        ss = jnp.sum(k_nope * k_nope, axis=-1, keepdims=True) + ss_pe
        r = lax.rsqrt(ss * (1.0 / MLA_QK_DIM) + EPS)
        k_out[0, hd, rs,0:LANES] = (k_nope * kgn_ref[...] * r).astype(BF16)
        k_out[0, hd, rs,LANES:2 * LANES] = (kr * r).astype(BF16)
        v_out[0, hd, rs, :] = kv[:, (MLA_HEADS + hd) * LANES:(MLA_HEADS + hd + 1) * LANES].astype(BF16)

    k_scale = RET_HEAD_DIM ** -0.5
    for hd in range(RET_HEADS):
        sl = slice(hd * LANES, (hd + 1) * LANES)
        rq = proj[:, _RQ + hd * LANES:_RQ + (hd + 1) * LANES]
        rk = proj[:, _RK + hd * LANES:_RK + (hd + 1) * LANES]
        rq_out[rs, sl] =(rq * cos_r + pltpu.roll(rq, LANES // 2, axis=1) * sin_r).astype(BF16)
        rk_out[rs, sl] =((rk * cos_r + pltpu.roll(rk, LANES // 2, axis=1) * sin_r) * k_scale).astype(BF16)
    rv_out[rs, :] = proj[:, _RV:_RV + RET_WIDTH].astype(BF16)
    rg_out[rs, :] = proj[:, _RG:_RG + RET_WIDTH].astype(BF16)


def _prep_kernel(x_ref, *refs, chains):
    rows = x_ref.shape[0] // chains
    for c in range(chains):
        _prep_rows(slice(c * rows, (c + 1) * rows), x_ref, *refs)


def _prep(x2d, pos2d, p, batch, seq, tile):
    n_tok, d = x2d.shape
    tiles_per_b = seq // tile
    tok = lambda i: (i, 0)
    head = lambda i: (i // tiles_per_b, 0, i % tiles_per_b, 0)
    consts = [p['attn_norm_g'], p['w_in'], p['q_norm_g'], p['w_uq'], p['kv_norm_g'], p['w_ukv'],
              p['q_g_nope'], p['q_g_rope'], p['q_g_rope_sw'], p['k_g_nope'], p['k_g_rope'], p['k_g_rope_sw'],
              p['freq_m'], p['sign_m'], p['freq_r'], p['sign_r']]
    out_shape = (
        jax.ShapeDtypeStruct((batch, MLA_HEADS, seq, MLA_HEAD_PAD), BF16),
        jax.ShapeDtypeStruct((batch, MLA_HEADS, seq, MLA_HEAD_PAD), BF16),
        jax.ShapeDtypeStruct((batch, MLA_HEADS, seq, MLA_V_DIM), BF16),
        jax.ShapeDtypeStruct((n_tok, RET_WIDTH), BF16),
        jax.ShapeDtypeStruct((n_tok, RET_WIDTH), BF16),
        jax.ShapeDtypeStruct((n_tok, RET_WIDTH), BF16),
        jax.ShapeDtypeStruct((n_tok, RET_WIDTH), BF16),
    )
    return pl.pallas_call(
        functools.partial(_prep_kernel, chains=PREP_CHAINS),
        grid=(n_tok // tile,),
        in_specs=[pl.BlockSpec((tile, d), tok), pl.BlockSpec((tile, 1), tok)]
                 + [_const_spec(c.shape) for c in consts],
        out_specs=(
            pl.BlockSpec((1, MLA_HEADS, tile, MLA_HEAD_PAD), head),
            pl.BlockSpec((1, MLA_HEADS, tile, MLA_HEAD_PAD), head),
            pl.BlockSpec((1, MLA_HEADS, tile, MLA_V_DIM), head),
            pl.BlockSpec((tile, RET_WIDTH), tok),
            pl.BlockSpec((tile, RET_WIDTH), tok),
            pl.BlockSpec((tile, RET_WIDTH), tok),
            pl.BlockSpec((tile, RET_WIDTH), tok),
        ),
        out_shape=out_shape,
        compiler_params=pltpu.CompilerParams(dimension_semantics=("parallel",),
                                             vmem_limit_bytes=VMEM_LIMIT_BYTES),
        name="prep",
    )(x2d, pos2d, *consts)


def _flash_kernel(q_ref, k_ref, v_ref, o_ref, m_sc, l_sc, acc_sc, *, tq, tk, hb):
    qi = pl.program_id(2)
    m_sc[...] = jnp.full(m_sc.shape, NEG_INF, F32)
    l_sc[...] = jnp.zeros(l_sc.shape, F32)
    acc_sc[...] = jnp.zeros(acc_sc.shape, F32)
    reps = tk // LANES

    def step(j, mask):
        off = pl.multiple_of(j * tk, tk)
        for hd in range(hb):
            k = k_ref[0, hd, pl.ds(off, tk), :]
            v = v_ref[0, hd, pl.ds(off, tk), :]
            s = lax.dot_general(q_ref[0, hd], k, (((1,), (1,)), ((), ())), preferred_element_type=F32)
            if mask is not None:
                s = jnp.where(mask, s, NEG_INF)
            m_prev = m_sc[hd]
            m_new = jnp.maximum(m_prev, jnp.max(s, axis=-1, keepdims=True))
            alpha = jnp.exp2(m_prev - m_new)
            p = jnp.exp2(s - jnp.concatenate([m_new] * reps, axis=1))
            p_cols = p[:, 0:LANES]
            for c in range(1, reps):
                p_cols = p_cols + p[:, c * LANES:(c + 1) * LANES]
            l_sc[hd] = alpha * l_sc[hd] + p_cols
            acc_sc[hd] = alpha * acc_sc[hd] + jnp.dot(p.astype(BF16), v, preferred_element_type=F32)
            m_sc[hd] = m_new

    n_full = qi * (tq // tk)

    def body(j, c):
        step(j, None)
        return c

    lax.fori_loop(0, n_full, body, 0)
    row = lax.broadcasted_iota(jnp.int32, (tq, tk), 0)
    col = lax.broadcasted_iota(jnp.int32, (tq, tk), 1)
    for d in range(tq // tk):
        step(n_full + d, col + d * tk <= row)
    for hd in range(hb):
        l = jnp.sum(l_sc[hd], axis=-1, keepdims=True)
        o_ref[0, :, hd * LANES:(hd + 1) * LANES] = (acc_sc[hd] / l).astype(o_ref.dtype)


def _flash(q, k, v, tq, tk, hb):
    batch, heads, seq, dqk = q.shape
    dv = v.shape[-1]
    return pl.pallas_call(
        functools.partial(_flash_kernel, tq=tq, tk=tk, hb=hb),
        grid=(batch, heads // hb, seq // tq),
        in_specs=[
            pl.BlockSpec((1, hb, tq, dqk), lambda b, h, i: (b, h, i, 0)),
            pl.BlockSpec((1, hb, seq, dqk), lambda b, h, i: (b, h, 0, 0)),
            pl.BlockSpec((1, hb, seq, dv), lambda b, h, i: (b, h, 0, 0)),
        ],
        out_specs=pl.BlockSpec((1, tq, hb * dv), lambda b, h, i: (b, i, h)),
        out_shape=jax.ShapeDtypeStruct((batch, seq, heads * dv), BF16),
        scratch_shapes=[pltpu.VMEM((hb, tq, LANES), F32), pltpu.VMEM((hb, tq, LANES), F32),
                        pltpu.VMEM((hb, tq, dv), F32)],
        compiler_params=pltpu.CompilerParams(dimension_semantics=("parallel", "parallel", "arbitrary"),
                                             vmem_limit_bytes=VMEM_LIMIT_BYTES),
        name="flash",
    )(q, k, v)


def _ret_kernel(q_ref, k_ref, v_ref, g_ref, gn_ref, o_ref, state_sc):
    c = pl.program_id(1)

    @pl.when(c == 0)
    def _():
        state_sc[...] = jnp.zeros(state_sc.shape, F32)

    n = RET_CHUNK
    ri = lax.broadcasted_iota(jnp.int32, (n, n), 0)
    ci = lax.broadcasted_iota(jnp.int32, (n, n), 1)
    diff = (ri - ci).astype(F32)
    idx = lax.broadcasted_iota(jnp.int32, (n, 1), 0).astype(F32)
    for hd in range(RET_HEADS):
        lg = math.log(1.0 - 2.0 ** (-5.0 - hd))
        intra = jnp.where(diff >= 0, jnp.exp(jnp.maximum(diff, 0.0) * lg), 0.0)
        q_decay = jnp.exp((idx + 1.0) * lg)
        k_decay = jnp.exp((n - 1.0 - idx) * lg)
        chunk_decay = math.exp(n * lg)
        sl = slice(hd * LANES, (hd + 1) * LANES)
        qc = q_ref[0, :, sl]
        kc = k_ref[0, :, sl]
        vc = v_ref[0, :, sl]
        state = state_sc[hd]
        scores = lax.dot_general(qc, kc, (((1,), (1,)), ((), ())), preferred_element_type=F32) * intra
        o = (jnp.dot(scores.astype(BF16), vc, preferred_element_type=F32)
             + jnp.dot((qc.astype(F32) * q_decay).astype(BF16), state.astype(BF16), preferred_element_type=F32))
        kd_t = (kc.astype(F32) * k_decay).T.astype(BF16)
        state_sc[hd] = state * chunk_decay + jnp.dot(kd_t, vc, preferred_element_type=F32)
        mu = jnp.mean(o, axis=-1, keepdims=True)
        cen = o - mu
        var = jnp.mean(cen * cen, axis=-1, keepdims=True)
        gate = g_ref[0, :, sl].astype(F32)
        y = cen * lax.rsqrt(var + EPS) * gn_ref[:, sl] * (gate * jax.nn.sigmoid(gate))
        o_ref[0, :, sl] = y.astype(o_ref.dtype)


def _retention(rq, rk, rv, rg, gn_g):
    batch, seq, width = rq.shape
    blk = pl.BlockSpec((1, RET_CHUNK, width), lambda b, c: (b, c, 0))
    return pl.pallas_call(
        _ret_kernel,
        grid=(batch, seq // RET_CHUNK),
        in_specs=[blk, blk, blk, blk, _const_spec(gn_g.shape)],
        out_specs=blk,
        out_shape=jax.ShapeDtypeStruct((batch, seq, width), BF16),
        scratch_shapes=[pltpu.VMEM((RET_HEADS, RET_HEAD_DIM, RET_HEAD_DIM), F32)],
        compiler_params=pltpu.CompilerParams(dimension_semantics=("parallel", "arbitrary")),
        name="retention",
    )(rq, rk, rv, rg, gn_g)


def _memkv_kernel(mem_ref, g_ref, wkv_ref, kg_ref, k_out, v_out):
    d = mem_ref.shape[-1]
    hdim = d // CROSS_HEADS
    m = _rms(mem_ref[0], g_ref[...]).astype(BF16)
    kv = jnp.dot(m, wkv_ref[...], preferred_element_type=F32)
    for hd in range(CROSS_HEADS):
        sl = slice(hd * hdim, (hd + 1) * hdim)
        k_out[0, :, sl] = _rms(kv[:, sl], kg_ref[...]).astype(BF16)
    v_out[0] = kv[:, d:].astype(BF16)


def _memkv(mem, mem_norm_g, w_kv, k_g):
    batch, m_tok, d = mem.shape
    blk = pl.BlockSpec((1, m_tok, d), lambda b: (b, 0, 0))
    return pl.pallas_call(
        _memkv_kernel,
        grid=(batch,),
        in_specs=[blk, _const_spec(mem_norm_g.shape), _const_spec(w_kv.shape), _const_spec(k_g.shape)],
        out_specs=(blk, blk),
        out_shape=(jax.ShapeDtypeStruct((batch, m_tok, d), BF16), jax.ShapeDtypeStruct((batch, m_tok, d), BF16)),
        compiler_params=pltpu.CompilerParams(dimension_semantics=("parallel",),
                                             vmem_limit_bytes=VMEM_LIMIT_BYTES),
        name="memkv",
    )(mem, mem_norm_g, w_kv, k_g)


def _mix_rows(x, a, r, kc_ref, vc_ref, woa_ref, wor_ref, cng_ref, wq_ref, qg_ref, wo_ref, mng_ref, wr_ref,
              br_ref):
    d = x.shape[-1]
    hdim = d // CROSS_HEADS
    x1 = (x
          + jnp.dot(a, woa_ref[...], preferred_element_type=F32)
          + jnp.dot(r, wor_ref[...], preferred_element_type=F32))
    hc = _rms(x1, cng_ref[...]).astype(BF16)
    q = jnp.dot(hc, wq_ref[...], preferred_element_type=F32)
    scale = hdim ** -0.5
    heads = []
    for hd in range(CROSS_HEADS):
        sl = slice(hd * hdim, (hd + 1) * hdim)
        qh = (_rms(q[:, sl], qg_ref[...]) * scale).astype(BF16)
        s = lax.dot_general(qh, kc_ref[0, :, sl], (((1,), (1,)), ((), ())), preferred_element_type=F32)
        s = s - jnp.max(s, axis=-1, keepdims=True)
        e = jnp.exp(s)
        pr = e / jnp.sum(e, axis=-1, keepdims=True)
        heads.append(jnp.dot(pr.astype(BF16), vc_ref[0, :, sl], preferred_element_type=F32).astype(BF16))
    o = jnp.concatenate(heads, axis=-1)
    x2 = x1 + jnp.dot(o, wo_ref[...], preferred_element_type=F32)
    hm = _rms(x2, mng_ref[...])

    hm_hi = hm.astype(BF16)
    hm_lo = (hm - hm_hi.astype(F32)).astype(BF16)
    logits = jnp.dot(jnp.concatenate([hm_hi, hm_lo, hm_hi], axis=-1), wr_ref[...], preferred_element_type=F32)
    biased = logits + br_ref[...]
    lane = lax.broadcasted_iota(jnp.int32, logits.shape, 1)
    lane_f = lane.astype(F32)
    big = float(4 * LANES)
    is_g = (lane >= N_EXPERTS) & (lane < N_EXPERTS + N_GROUPS)
    gl = jnp.where(is_g, logits, NEG_INF)
    gexp = jnp.where(is_g, jnp.exp(gl - jnp.max(gl, axis=-1, keepdims=True)), 0.0)
    gb = jnp.where(is_g, biased, NEG_INF)
    g_lane = jnp.min(jnp.where(gb == jnp.max(gb, axis=-1, keepdims=True), lane_f, big), axis=-1, keepdims=True)
    g_w = (jnp.sum(jnp.where(lane_f == g_lane, gexp, 0.0), axis=-1, keepdims=True)
           / jnp.sum(gexp, axis=-1, keepdims=True))
    g_idx = g_lane.astype(jnp.int32) - N_EXPERTS
    in_grp = (lane < N_EXPERTS) & ((lane // EXPERTS_PER_GROUP) == g_idx)
    el = jnp.where(in_grp, logits, NEG_INF)
    eexp = jnp.where(in_grp, jnp.exp(el - jnp.max(el, axis=-1, keepdims=True)), 0.0)
    eb = jnp.where(in_grp, biased, NEG_INF)
    i1 = jnp.min(jnp.where(eb == jnp.max(eb, axis=-1, keepdims=True), lane_f, big), axis=-1, keepdims=True)
    eb2 = jnp.where(lane_f == i1, NEG_INF, eb)
    i2 = jnp.min(jnp.where(eb2 == jnp.max(eb2, axis=-1, keepdims=True), lane_f, big), axis=-1, keepdims=True)
    psel = jnp.where((lane_f == i1) | (lane_f == i2), eexp, 0.0)
    gate = psel / jnp.sum(psel, axis=-1, keepdims=True) * g_w
    grp_onehot = (lane == g_idx).astype(F32)
    return x2, hm.astype(BF16), gate, grp_onehot


def _group_sort(hm, gate, grp, hms_out, pos_out, meta_out):
    t = hm.shape[0]
    ts = hms_out.shape[0]
    earlier = (lax.broadcasted_iota(jnp.int32, (t, t), 1) < lax.broadcasted_iota(jnp.int32, (t, t), 0))
    cnt = jnp.dot(earlier.astype(BF16), grp.astype(BF16), preferred_element_type=F32)
    n = jnp.sum(grp, axis=0, keepdims=True)
    n_al = jnp.floor((n + (SORT_ALIGN - 1.0)) * (1.0 / SORT_ALIGN)) * SORT_ALIGN
    lane1 = lax.broadcasted_iota(jnp.int32, (1, LANES), 1)
    start = jnp.zeros((1, LANES), F32)
    run = jnp.zeros((1, 1), F32)
    for g in range(1, N_GROUPS):
        run = run + jnp.sum(jnp.where(lane1 == g - 1, n_al, 0.0), axis=-1, keepdims=True)
        start = jnp.where(lane1 == g, run, start)
    pos = jnp.sum(grp * (cnt + start), axis=-1, keepdims=True)
    pos_out[...] = pos
    row8 = lax.broadcasted_iota(jnp.int32, (8, LANES), 0)
    meta_out[0] = jnp.where(row8 == 0, start, jnp.where(row8 == 1, n, 0.0)).astype(jnp.int32)
    pos_row = jnp.broadcast_to(pos, (t, LANES)).T[0:1, :]
    perm = (lax.broadcasted_iota(jnp.int32, (ts, t), 0).astype(F32) == pos_row).astype(BF16)
    g_hi = gate.astype(BF16).astype(F32)
    rem = gate - g_hi
    g_mid = rem.astype(BF16).astype(F32)
    g_lo = rem - g_mid
    gate_pieces = (g_hi + pltpu.roll(g_mid, N_EXPERTS, axis=1) + pltpu.roll(g_lo, 2 * N_EXPERTS, axis=1)).astype(BF16)
    hms_out[...] = jnp.dot(perm, jnp.concatenate([hm, gate_pieces], axis=-1),
                           preferred_element_type=F32).astype(BF16)


def _mix_kernel(x_ref, a_ref, r_ref, *rest, chains):
    *w_refs, x2_out, hms_out, pos_out, meta_out, hm_sc, gate_sc, grp_sc = rest
    rows = x_ref.shape[0] // chains
    for c in range(chains):
        sl = slice(c * rows, (c + 1) * rows)
        x2, hm, gate, grp = _mix_rows(x_ref[sl, :], a_ref[sl, :], r_ref[sl, :], *w_refs)
        x2_out[sl, :] = x2
        hm_sc[sl, :] = hm
        gate_sc[sl, :] = gate
        grp_sc[sl, :] = grp
    _group_sort(hm_sc[...], gate_sc[...], grp_sc[...], hms_out, pos_out, meta_out)


def _mix(x2d, a2d, r2d, kc, vc, p, batch, seq, tile):
    n_tok, d = x2d.shape
    half = a2d.shape[-1]
    tiles_per_b = seq // tile
    n_tiles = n_tok // tile
    tok = lambda i: (i, 0)
    mem_blk = pl.BlockSpec((1,) + kc.shape[1:], lambda i: (i // tiles_per_b, 0, 0))
    consts = [p['w_out_a'], p['w_out_r'], p['cross_norm_g'], p['cross_w_q'], p['cross_q_g'], p['cross_w_o'],
              p['moe_norm_g'], p['w_router'], p['b_router']]
    return pl.pallas_call(
        functools.partial(_mix_kernel, chains=MIX_CHAINS),
        grid=(n_tiles,),
        in_specs=[pl.BlockSpec((tile, d), tok), pl.BlockSpec((tile, half), tok), pl.BlockSpec((tile, half), tok),
                  mem_blk, mem_blk] + [_const_spec(c.shape) for c in consts],
        out_specs=(pl.BlockSpec((tile, d), tok), pl.BlockSpec((SORT_SLOTS, d + LANES), tok),
                   pl.BlockSpec((tile, 1), tok), pl.BlockSpec((1, 8, LANES), lambda i: (i, 0, 0))),
        out_shape=(jax.ShapeDtypeStruct((n_tok, d), F32),
                   jax.ShapeDtypeStruct((n_tiles * SORT_SLOTS, d + LANES), BF16),
                   jax.ShapeDtypeStruct((n_tok, 1), F32),
                   jax.ShapeDtypeStruct((n_tiles, 8, LANES), jnp.int32)),
        scratch_shapes=[pltpu.VMEM((tile, d), BF16), pltpu.VMEM((tile, LANES), F32), pltpu.VMEM((tile, LANES), F32)],
        compiler_params=pltpu.CompilerParams(dimension_semantics=("parallel",),
                                             vmem_limit_bytes=VMEM_LIMIT_BYTES),
        name="mix",
    )(x2d, a2d, r2d, kc, vc, *consts)


def _moe_kernel(start_ref, count_ref, hms_ref, wgu_ref, wd_ref, o_ref, *, group, tiles, chunk):
    blk = pl.program_id(0)
    step = pl.program_id(1)
    d = wd_ref.shape[1]
    ff = wgu_ref.shape[2] // 2
    grp = step // (EXPERTS_PER_GROUP // group)

    @pl.when(step == 0)
    def _():
        o_ref[...] = jnp.zeros(o_ref.shape, o_ref.dtype)

    lane = lax.broadcasted_iota(jnp.int32, (chunk, LANES), 1)
    piece_lane = lane < 3 * N_EXPERTS

    def tile_body(tl, carry):
        idx = (blk * tiles + tl) * N_GROUPS + grp
        base = tl * SORT_SLOTS + start_ref[idx]
        n_chunks = (count_ref[idx] + chunk - 1) // chunk

        def chunk_body(c, carry2):
            r0 = pl.multiple_of(base + c * chunk, SORT_ALIGN)
            xs = hms_ref[pl.ds(r0, chunk), 0:d]
            gates = hms_ref[pl.ds(r0, chunk), d:d + LANES].astype(F32)
            acts = []
            for j in range(group):
                e = step * group + j
                au = jnp.dot(xs, wgu_ref[j], preferred_element_type=F32)
                a = au[:, :ff]
                u = au[:, ff:]
                g_e = jnp.sum(jnp.where(piece_lane & ((lane & (N_EXPERTS - 1)) == e), gates, 0.0),
                              axis=-1, keepdims=True)
                acts.append((a * jax.nn.sigmoid(a) * u * g_e).astype(BF16))
            y = jnp.dot(jnp.concatenate(acts, axis=-1), wd_ref[...], preferred_element_type=F32)
            o_ref[pl.ds(r0, chunk), :] = (o_ref[pl.ds(r0, chunk), :].astype(F32) + y).astype(o_ref.dtype)
            return carry2

        lax.fori_loop(0, n_chunks, chunk_body, 0)
        return carry

    lax.fori_loop(0, tiles, tile_body, 0)


def _moe(hms, starts, counts, w_gu, w_d, tiles, group, chunk):
    n_rows, dx = hms.shape
    n_exp, d, ff2 = w_gu.shape
    ff = ff2 // 2
    rows = tiles * SORT_SLOTS
    grid_spec = pltpu.PrefetchScalarGridSpec(
        num_scalar_prefetch=2,
        grid=(n_rows // rows, n_exp // group),
        in_specs=[pl.BlockSpec((rows, dx), lambda i, e, s, c: (i, 0)),
                  pl.BlockSpec((group, d, ff2), lambda i, e, s, c: (e, 0, 0)),
                  pl.BlockSpec((group * ff, d), lambda i, e, s, c: (e, 0))],
        out_specs=pl.BlockSpec((rows, d), lambda i, e, s, c: (i, 0)),
    )
    return pl.pallas_call(
        functools.partial(_moe_kernel, group=group, tiles=tiles, chunk=chunk),
        grid_spec=grid_spec,
        out_shape=jax.ShapeDtypeStruct((n_rows, d), BF16),
        compiler_params=pltpu.CompilerParams(dimension_semantics=("parallel", "arbitrary"),
                                             vmem_limit_bytes=VMEM_LIMIT_BYTES),
        name="moe",
    )(starts, counts, hms, w_gu, w_d.reshape(n_exp * ff, d))


def _unsort_kernel(x2_ref, pos_ref, ys_ref, o_ref):
    t = x2_ref.shape[0]
    ts = ys_ref.shape[0]
    unperm = (lax.broadcasted_iota(jnp.int32, (t, ts), 1).astype(F32) == pos_ref[...]).astype(BF16)
    o_ref[...] = x2_ref[...] + jnp.dot(unperm, ys_ref[...], preferred_element_type=F32)


def _unsort(x2, pos, ys, tile):
    n_tok, d = x2.shape
    tok = lambda i: (i, 0)
    return pl.pallas_call(
        _unsort_kernel,
        grid=(n_tok // tile,),
        in_specs=[pl.BlockSpec((tile, d), tok), pl.BlockSpec((tile, 1), tok), pl.BlockSpec((SORT_SLOTS, d), tok)],
        out_specs=pl.BlockSpec((tile, d), tok),
        out_shape=jax.ShapeDtypeStruct((n_tok, d), F32),
        compiler_params=pltpu.CompilerParams(dimension_semantics=("parallel",)),
        name="unsort",
    )(x2, pos, ys)


def _row(v):
    return v.reshape(1, -1).astype(F32)


def _pad_lanes(v, width=LANES):
    return jnp.pad(v, [(0, 0)] * (v.ndim - 1) + [(0, width - v.shape[-1])])


def _swap_halves(v):
    half = v.shape[-1] // 2
    return jnp.concatenate([v[..., half:], v[..., :half]], axis=-1)


def _layout_params(l, attn_norm_g, w_in, mla_q_norm_g, mla_w_uq, mla_kv_norm_g, mla_w_ukv, mla_q_qk_g,
                   mla_k_qk_g, ret_gn_g, w_out, cross_norm_g, mem_norm_g, cross_w_q, cross_w_kv, cross_q_qk_g,
                   cross_k_qk_g, cross_w_o, moe_norm_g, router_w_group, router_b_group, router_w_expert,
                   router_b_expert):
    p = {}
    d = w_in.shape[1]
    wi = w_in[l]
    c_q, c_kv, k_pe, rq, rk, rv, rg = jnp.split(
        wi, np.cumsum([MLA_Q_RANK, MLA_KV_RANK, MLA_ROPE_DIM, RET_WIDTH, RET_WIDTH, RET_WIDTH]).tolist(), axis=1)
    p['w_in'] = jnp.concatenate([c_q, c_kv, rq, rk, rv, rg, _pad_lanes(k_pe), _pad_lanes(_swap_halves(k_pe))],
                                axis=1).astype(BF16)
    wq = mla_w_uq[l].reshape(MLA_Q_RANK, MLA_HEADS, MLA_QK_DIM)
    wq_rope = wq[:, :, MLA_NOPE_DIM:]
    p['w_uq'] = jnp.concatenate([wq[:, :, :MLA_NOPE_DIM], _pad_lanes(wq_rope), _pad_lanes(_swap_halves(wq_rope))],
                                axis=2).reshape(MLA_Q_RANK, MLA_HEADS * Q_HEAD_COLS).astype(BF16)
    wkv = mla_w_ukv[l].reshape(MLA_KV_RANK, MLA_HEADS, MLA_NOPE_DIM + MLA_V_DIM)
    p['w_ukv'] = jnp.concatenate([wkv[:, :, :MLA_NOPE_DIM].reshape(MLA_KV_RANK, -1),
                                  wkv[:, :, MLA_NOPE_DIM:].reshape(MLA_KV_RANK, -1)], axis=1).astype(BF16)
    p['attn_norm_g'] = _row(attn_norm_g[l])
    p['q_norm_g'] = _row(mla_q_norm_g[l])
    p['kv_norm_g'] = _row(mla_kv_norm_g[l])
    for name, g in (('q', mla_q_qk_g[l]), ('k', mla_k_qk_g[l])):
        p[name + '_g_nope'] = _row(g[:MLA_NOPE_DIM])
        p[name + '_g_rope'] = _row(_pad_lanes(g[MLA_NOPE_DIM:]))
        p[name + '_g_rope_sw'] = _row(_pad_lanes(_swap_halves(g[MLA_NOPE_DIM:])))
    hm_ = MLA_ROPE_DIM // 2
    f_m = ROPE_BASE ** (-jnp.arange(hm_, dtype=F32) / hm_)
    p['freq_m'] = _row(_pad_lanes(jnp.concatenate([f_m, f_m])))
    p['sign_m'] = _row(_pad_lanes(jnp.concatenate([-jnp.ones(hm_, F32), jnp.ones(hm_, F32)])))
    hr_ = RET_HEAD_DIM // 2
    f_r = ROPE_BASE ** (-jnp.arange(hr_, dtype=F32) / hr_)
    p['freq_r'] = _row(jnp.concatenate([f_r, f_r]))
    p['sign_r'] = _row(jnp.concatenate([-jnp.ones(hr_, F32), jnp.ones(hr_, F32)]))
    p['ret_gn_g'] = _row(ret_gn_g[l])
    half = MLA_HEADS * MLA_V_DIM
    p['w_out_a'] = w_out[l][:half].astype(BF16)
    p['w_out_r'] = w_out[l][half:].astype(BF16)
    p['cross_norm_g'] = _row(cross_norm_g[l])
    p['mem_norm_g'] = _row(mem_norm_g[l])
    p['cross_w_q'] = cross_w_q[l].astype(BF16)
    p['cross_w_kv'] = cross_w_kv[l].astype(BF16)
    p['cross_q_g'] = _row(cross_q_qk_g[l])
    p['cross_k_g'] = _row(cross_k_qk_g[l])
    p['cross_w_o'] = cross_w_o[l].astype(BF16)
    p['moe_norm_g'] = _row(moe_norm_g[l])
    w_r = _pad_lanes(jnp.concatenate([router_w_expert[l], router_w_group[l]], axis=1)).astype(F32)
    w_r_hi = w_r.astype(BF16)
    w_r_lo = (w_r - w_r_hi.astype(F32)).astype(BF16)
    p['w_router'] = jnp.concatenate([w_r_hi, w_r_hi, w_r_lo], axis=0)
    p['b_router'] = _row(_pad_lanes(jnp.concatenate([router_b_expert[l], router_b_group[l]])))
    return p


PREP_TILE = 512
PREP_CHAINS = 1
FLASH_TQ = 512
FLASH_TK = 512
FLASH_HEADS_PER_STEP = 4
MIX_TILE = 512
MIX_CHAINS = 2
SORT_ALIGN = 16
MOE_CHUNK = 144
SORT_SLOTS = -(-(MIX_TILE + N_GROUPS * (SORT_ALIGN - 1) + MOE_CHUNK) // LANES) * LANES
MOE_TILES_PER_BLOCK = 4
MOE_EXPERTS_PER_STEP = 4


def kernel(x, mem, positions, attn_norm_g, w_in, mla_q_norm_g, mla_w_uq, mla_kv_norm_g, mla_w_ukv, mla_q_qk_g, mla_k_qk_g, ret_gn_g, w_out, cross_norm_g, mem_norm_g, cross_w_q, cross_w_kv, cross_q_qk_g, cross_k_qk_g, cross_w_o, moe_norm_g, router_w_group, router_b_group, router_w_expert, router_b_expert, expert_w_gate, expert_w_up, expert_w_down):
    batch, seq, d = x.shape
    depth = w_in.shape[0]
    pos2d = positions.reshape(batch * seq, 1).astype(jnp.int32)
    x2d = x.reshape(batch * seq, d)
    for l in range(depth):
        p = _layout_params(l, attn_norm_g, w_in, mla_q_norm_g, mla_w_uq, mla_kv_norm_g, mla_w_ukv, mla_q_qk_g,
                           mla_k_qk_g, ret_gn_g, w_out, cross_norm_g, mem_norm_g, cross_w_q, cross_w_kv,
                           cross_q_qk_g, cross_k_qk_g, cross_w_o, moe_norm_g, router_w_group, router_b_group,
                           router_w_expert, router_b_expert)
        w_gu = jnp.concatenate([expert_w_gate[l], expert_w_up[l]], axis=-1).astype(BF16)
        w_d = expert_w_down[l].astype(BF16)

        q, k, v, rq, rk, rv, rg = _prep(x2d, pos2d, p, batch, seq, PREP_TILE)
        a_out = _flash(q, k, v, FLASH_TQ, FLASH_TK, FLASH_HEADS_PER_STEP)
        shp = (batch, seq, RET_WIDTH)
        r_out = _retention(rq.reshape(shp), rk.reshape(shp), rv.reshape(shp), rg.reshape(shp), p['ret_gn_g'])
        kc, vc = _memkv(mem, p['mem_norm_g'], p['cross_w_kv'], p['cross_k_g'])
        x2, hms, pos, meta = _mix(x2d, a_out.reshape(batch * seq, -1), r_out.reshape(batch * seq, -1), kc, vc, p,
                                  batch, seq, MIX_TILE)
        starts = meta[:, 0, :N_GROUPS].reshape(-1)
        counts = meta[:, 1, :N_GROUPS].reshape(-1)
        ys = _moe(hms, starts, counts, w_gu, w_d, MOE_TILES_PER_BLOCK, MOE_EXPERTS_PER_STEP, MOE_CHUNK)
        x2d = _unsort(x2, pos, ys, MIX_TILE)
    return x2d.reshape(batch, seq, d)
```

```python
import functools
import math

import numpy as np
import jax
import jax.numpy as jnp
from jax import lax
from jax.experimental import pallas as pl
from jax.experimental.pallas import tpu as pltpu

F32 = jnp.float32
BF16 = jnp.bfloat16

LANES = 128
VMEM_LIMIT_BYTES = 56 * 1024 * 1024

MLA_HEADS = 4
MLA_NOPE_DIM = 128
MLA_ROPE_DIM = 64
MLA_QK_DIM = MLA_NOPE_DIM + MLA_ROPE_DIM
MLA_V_DIM = 128
MLA_Q_RANK = 256
MLA_KV_RANK = 128
MLA_HEAD_PAD = 256
RET_HEADS = 4
RET_HEAD_DIM = 128
RET_WIDTH = RET_HEADS * RET_HEAD_DIM
RET_CHUNK = 256
CROSS_HEADS = 4
N_GROUPS = 4
EXPERTS_PER_GROUP = 8
N_EXPERTS = 32
ROPE_BASE = 10000.0
EPS = 1e-6
NEG_INF = -1e30

_C_Q = 0
_C_KV = _C_Q + MLA_Q_RANK
_RQ = _C_KV + MLA_KV_RANK
_RK = _RQ + RET_WIDTH
_RV = _RK + RET_WIDTH
_RG = _RV + RET_WIDTH
_KPE = _RG + RET_WIDTH
_KPE_SW = _KPE + LANES
IN_COLS_PAD = _KPE_SW + LANES
Q_HEAD_COLS = 3 * LANES


def _rms(x, g):
    return x * lax.rsqrt(jnp.mean(x * x, axis=-1, keepdims=True) + EPS) * g


def _const_spec(shape):
    nd = len(shape)
    return pl.BlockSpec(shape, lambda *_: (0,) * nd, pipeline_mode=pl.Buffered(1))


def _prep_rows(rs, x_ref, pos_ref, g_ref, win_ref, qng_ref, wuq_ref, kvng_ref, wukv_ref,
               qgn_ref, qgr_ref, qgs_ref, kgn_ref, kgr_ref, kgs_ref,
               fm_ref, sm_ref, fr_ref, sr_ref,
               q_out, k_out, v_out, rq_out, rk_out, rv_out, rg_out):
    x = x_ref[rs, :]
    h = _rms(x, g_ref[...]).astype(BF16)
    proj = jnp.dot(h, win_ref[...], preferred_element_type=F32)

    posf = pos_ref[rs, :].astype(F32)
    ang_m = posf * fm_ref[...]
    cos_m = jnp.cos(ang_m)
    sin_m = jnp.sin(ang_m) * sm_ref[...]
    ang_r = posf * fr_ref[...]
    cos_r = jnp.cos(ang_r)
    sin_r = jnp.sin(ang_r) * sr_ref[...]

    c_q = proj[:, _C_Q:_C_Q + MLA_Q_RANK]
    qall = jnp.dot(_rms(c_q, qng_ref[...]).astype(BF16), wuq_ref[...], preferred_element_type=F32)
    q_scale = MLA_QK_DIM ** -0.5 * math.log2(math.e)
    for hd in range(MLA_HEADS):
        base = hd * Q_HEAD_COLS
        nope = qall[:, base:base + LANES]
        rope = qall[:, base + LANES:base + 2 * LANES]
        rope_sw = qall[:, base + 2 * LANES:base + 3 * LANES]
        ss = jnp.sum(nope * nope, axis=-1, keepdims=True) + jnp.sum(rope * rope, axis=-1, keepdims=True)
        r = lax.rsqrt(ss * (1.0 / MLA_QK_DIM) + EPS) * q_scale
        roped = rope * qgr_ref[...] * cos_m + rope_sw * qgs_ref[...] * sin_m
        q_out[0, hd, rs,0:LANES] = (nope * qgn_ref[...] * r).astype(BF16)
        q_out[0, hd, rs,LANES:2 * LANES] = (roped * r).astype(BF16)

    c_kv = proj[:, _C_KV:_C_KV + MLA_KV_RANK]
    kv = jnp.dot(_rms(c_kv, kvng_ref[...]).astype(BF16), wukv_ref[...], preferred_element_type=F32)
    k_pe = proj[:, _KPE:_KPE + LANES]
    k_pe_sw = proj[:, _KPE_SW:_KPE_SW + LANES]
    kr = k_pe * kgr_ref[...] * cos_m + k_pe_sw * kgs_ref[...] * sin_m
    ss_pe = jnp.sum(k_pe * k_pe, axis=-1, keepdims=True)
    for hd in range(MLA_HEADS):
        k_nope = kv[:, hd * LANES:(hd + 1) * LANES]
        ss = jnp.sum(k_nope * k_nope, axis=-1, keepdims=True) + ss_pe
        r = lax.rsqrt(ss * (1.0 / MLA_QK_DIM) + EPS)
        k_out[0, hd, rs,0:LANES] = (k_nope * kgn_ref[...] * r).astype(BF16)
        k_out[0, hd, rs,LANES:2 * LANES] = (kr * r).astype(BF16)
        v_out[0, hd, rs, :] = kv[:, (MLA_HEADS + hd) * LANES:(MLA_HEADS + hd + 1) * LANES].astype(BF16)

    k_scale = RET_HEAD_DIM ** -0.5
    for hd in range(RET_HEADS):
        sl = slice(hd * LANES, (hd + 1) * LANES)
        rq = proj[:, _RQ + hd * LANES:_RQ + (hd + 1) * LANES]
        rk = proj[:, _RK + hd * LANES:_RK + (hd + 1) * LANES]
        rq_out[rs, sl] =(rq * cos_r + pltpu.roll(rq, LANES // 2, axis=1) * sin_r).astype(BF16)
        rk_out[rs, sl] =((rk * cos_r + pltpu.roll(rk, LANES // 2, axis=1) * sin_r) * k_scale).astype(BF16)
    rv_out[rs, :] = proj[:, _RV:_RV + RET_WIDTH].astype(BF16)
    rg_out[rs, :] = proj[:, _RG:_RG + RET_WIDTH].astype(BF16)


def _prep_kernel(x_ref, *refs, chains):
    rows = x_ref.shape[0] // chains
    for c in range(chains):
        _prep_rows(slice(c * rows, (c + 1) * rows), x_ref, *refs)


def _prep(x2d, pos2d, p, batch, seq, tile):
    n_tok, d = x2d.shape
    tiles_per_b = seq // tile
    tok = lambda i: (i, 0)
    head = lambda i: (i // tiles_per_b, 0, i % tiles_per_b, 0)
    consts = [p['attn_norm_g'], p['w_in'], p['q_norm_g'], p['w_uq'], p['kv_norm_g'], p['w_ukv'],
              p['q_g_nope'], p['q_g_rope'], p['q_g_rope_sw'], p['k_g_nope'], p['k_g_rope'], p['k_g_rope_sw'],
              p['freq_m'], p['sign_m'], p['freq_r'], p['sign_r']]
    out_shape = (
        jax.ShapeDtypeStruct((batch, MLA_HEADS, seq, MLA_HEAD_PAD), BF16),
        jax.ShapeDtypeStruct((batch, MLA_HEADS, seq, MLA_HEAD_PAD), BF16),
        jax.ShapeDtypeStruct((batch, MLA_HEADS, seq, MLA_V_DIM), BF16),
        jax.ShapeDtypeStruct((n_tok, RET_WIDTH), BF16),
        jax.ShapeDtypeStruct((n_tok, RET_WIDTH), BF16),
        jax.ShapeDtypeStruct((n_tok, RET_WIDTH), BF16),
        jax.ShapeDtypeStruct((n_tok, RET_WIDTH), BF16),
    )
    return pl.pallas_call(
        functools.partial(_prep_kernel, chains=PREP_CHAINS),
        grid=(n_tok // tile,),
        in_specs=[pl.BlockSpec((tile, d), tok), pl.BlockSpec((tile, 1), tok)]
                 + [_const_spec(c.shape) for c in consts],
        out_specs=(
            pl.BlockSpec((1, MLA_HEADS, tile, MLA_HEAD_PAD), head),
            pl.BlockSpec((1, MLA_HEADS, tile, MLA_HEAD_PAD), head),
            pl.BlockSpec((1, MLA_HEADS, tile, MLA_V_DIM), head),
            pl.BlockSpec((tile, RET_WIDTH), tok),
            pl.BlockSpec((tile, RET_WIDTH), tok),
            pl.BlockSpec((tile, RET_WIDTH), tok),
            pl.BlockSpec((tile, RET_WIDTH), tok),
        ),
        out_shape=out_shape,
        compiler_params=pltpu.CompilerParams(dimension_semantics=("parallel",),
                                             vmem_limit_bytes=VMEM_LIMIT_BYTES),
        name="prep",
    )(x2d, pos2d, *consts)


def _flash_kernel(q_ref, k_ref, v_ref, o_ref, m_sc, l_sc, acc_sc, *, tq, tk, hb):
    qi = pl.program_id(2)
    m_sc[...] = jnp.full(m_sc.shape, NEG_INF, F32)
    l_sc[...] = jnp.zeros(l_sc.shape, F32)
    acc_sc[...] = jnp.zeros(acc_sc.shape, F32)
    reps = tk // LANES

    def step(j, mask):
        off = pl.multiple_of(j * tk, tk)
        for hd in range(hb):
            k = k_ref[0, hd, pl.ds(off, tk), :]
            v = v_ref[0, hd, pl.ds(off, tk), :]
            s = lax.dot_general(q_ref[0, hd], k, (((1,), (1,)), ((), ())), preferred_element_type=F32)
            if mask is not None:
                s = jnp.where(mask, s, NEG_INF)
            m_prev = m_sc[hd]
            m_new = jnp.maximum(m_prev, jnp.max(s, axis=-1, keepdims=True))
            alpha = jnp.exp2(m_prev - m_new)
            p = jnp.exp2(s - jnp.concatenate([m_new] * reps, axis=1))
            p_cols = p[:, 0:LANES]
            for c in range(1, reps):
                p_cols = p_cols + p[:, c * LANES:(c + 1) * LANES]
            l_sc[hd] = alpha * l_sc[hd] + p_cols
            acc_sc[hd] = alpha * acc_sc[hd] + jnp.dot(p.astype(BF16), v, preferred_element_type=F32)
            m_sc[hd] = m_new

    n_full = qi * (tq // tk)

    def body(j, c):
        step(j, None)
        return c

    lax.fori_loop(0, n_full, body, 0)
    row = lax.broadcasted_iota(jnp.int32, (tq, tk), 0)
    col = lax.broadcasted_iota(jnp.int32, (tq, tk), 1)
    for d in range(tq // tk):
        step(n_full + d, col + d * tk <= row)
    for hd in range(hb):
        l = jnp.sum(l_sc[hd], axis=-1, keepdims=True)
        o_ref[0, :, hd * LANES:(hd + 1) * LANES] = (acc_sc[hd] / l).astype(o_ref.dtype)


def _flash(q, k, v, tq, tk, hb):
    batch, heads, seq, dqk = q.shape
    dv = v.shape[-1]
    return pl.pallas_call(
        functools.partial(_flash_kernel, tq=tq, tk=tk, hb=hb),
        grid=(batch, heads // hb, seq // tq),
        in_specs=[
            pl.BlockSpec((1, hb, tq, dqk), lambda b, h, i: (b, h, i, 0)),
            pl.BlockSpec((1, hb, seq, dqk), lambda b, h, i: (b, h, 0, 0)),
            pl.BlockSpec((1, hb, seq, dv), lambda b, h, i: (b, h, 0, 0)),
        ],
        out_specs=pl.BlockSpec((1, tq, hb * dv), lambda b, h, i: (b, i, h)),
        out_shape=jax.ShapeDtypeStruct((batch, seq, heads * dv), BF16),
        scratch_shapes=[pltpu.VMEM((hb, tq, LANES), F32), pltpu.VMEM((hb, tq, LANES), F32),
                        pltpu.VMEM((hb, tq, dv), F32)],
        compiler_params=pltpu.CompilerParams(dimension_semantics=("parallel", "parallel", "arbitrary"),
                                             vmem_limit_bytes=VMEM_LIMIT_BYTES),
        name="flash",
    )(q, k, v)


def _ret_kernel(q_ref, k_ref, v_ref, g_ref, gn_ref, o_ref, state_sc):
    c = pl.program_id(1)

    @pl.when(c == 0)
    def _():
        state_sc[...] = jnp.zeros(state_sc.shape, F32)

    n = RET_CHUNK
    ri = lax.broadcasted_iota(jnp.int32, (n, n), 0)
    ci = lax.broadcasted_iota(jnp.int32, (n, n), 1)
    diff = (ri - ci).astype(F32)
    idx = lax.broadcasted_iota(jnp.int32, (n, 1), 0).astype(F32)
    for hd in range(RET_HEADS):
        lg = math.log(1.0 - 2.0 ** (-5.0 - hd))
        intra = jnp.where(diff >= 0, jnp.exp(jnp.maximum(diff, 0.0) * lg), 0.0)
        q_decay = jnp.exp((idx + 1.0) * lg)
        k_decay = jnp.exp((n - 1.0 - idx) * lg)
        chunk_decay = math.exp(n * lg)
        sl = slice(hd * LANES, (hd + 1) * LANES)
        qc = q_ref[0, :, sl]
        kc = k_ref[0, :, sl]
        vc = v_ref[0, :, sl]
        state = state_sc[hd]
        scores = lax.dot_general(qc, kc, (((1,), (1,)), ((), ())), preferred_element_type=F32) * intra
        o = (jnp.dot(scores.astype(BF16), vc, preferred_element_type=F32)
             + jnp.dot((qc.astype(F32) * q_decay).astype(BF16), state.astype(BF16), preferred_element_type=F32))
        kd_t = (kc.astype(F32) * k_decay).T.astype(BF16)
        state_sc[hd] = state * chunk_decay + jnp.dot(kd_t, vc, preferred_element_type=F32)
        mu = jnp.mean(o, axis=-1, keepdims=True)
        cen = o - mu
        var = jnp.mean(cen * cen, axis=-1, keepdims=True)
        gate = g_ref[0, :, sl].astype(F32)
        y = cen * lax.rsqrt(var + EPS) * gn_ref[:, sl] * (gate * jax.nn.sigmoid(gate))
        o_ref[0, :, sl] = y.astype(o_ref.dtype)


def _retention(rq, rk, rv, rg, gn_g):
    batch, seq, width = rq.shape
    blk = pl.BlockSpec((1, RET_CHUNK, width), lambda b, c: (b, c, 0))
    return pl.pallas_call(
        _ret_kernel,
        grid=(batch, seq // RET_CHUNK),
        in_specs=[blk, blk, blk, blk, _const_spec(gn_g.shape)],
        out_specs=blk,
        out_shape=jax.ShapeDtypeStruct((batch, seq, width), BF16),
        scratch_shapes=[pltpu.VMEM((RET_HEADS, RET_HEAD_DIM, RET_HEAD_DIM), F32)],
        compiler_params=pltpu.CompilerParams(dimension_semantics=("parallel", "arbitrary")),
        name="retention",
    )(rq, rk, rv, rg, gn_g)


def _memkv_kernel(mem_ref, g_ref, wkv_ref, kg_ref, k_out, v_out):
    d = mem_ref.shape[-1]
    hdim = d // CROSS_HEADS
    m = _rms(mem_ref[0], g_ref[...]).astype(BF16)
    kv = jnp.dot(m, wkv_ref[...], preferred_element_type=F32)
    for hd in range(CROSS_HEADS):
        sl = slice(hd * hdim, (hd + 1) * hdim)
        k_out[0, :, sl] = _rms(kv[:, sl], kg_ref[...]).astype(BF16)
    v_out[0] = kv[:, d:].astype(BF16)


def _memkv(mem, mem_norm_g, w_kv, k_g):
    batch, m_tok, d = mem.shape
    blk = pl.BlockSpec((1, m_tok, d), lambda b: (b, 0, 0))
    return pl.pallas_call(
        _memkv_kernel,
        grid=(batch,),
        in_specs=[blk, _const_spec(mem_norm_g.shape), _const_spec(w_kv.shape), _const_spec(k_g.shape)],
        out_specs=(blk, blk),
        out_shape=(jax.ShapeDtypeStruct((batch, m_tok, d), BF16), jax.ShapeDtypeStruct((batch, m_tok, d), BF16)),
        compiler_params=pltpu.CompilerParams(dimension_semantics=("parallel",),
                                             vmem_limit_bytes=VMEM_LIMIT_BYTES),
        name="memkv",
    )(mem, mem_norm_g, w_kv, k_g)


def _mix_rows(x, a, r, kc_ref, vc_ref, woa_ref, wor_ref, cng_ref, wq_ref, qg_ref, wo_ref, mng_ref, wr_ref,
              br_ref):
    d = x.shape[-1]
    hdim = d // CROSS_HEADS
    x1 = (x
          + jnp.dot(a, woa_ref[...], preferred_element_type=F32)
          + jnp.dot(r, wor_ref[...], preferred_element_type=F32))
    hc = _rms(x1, cng_ref[...]).astype(BF16)
    q = jnp.dot(hc, wq_ref[...], preferred_element_type=F32)
    scale = hdim ** -0.5
    heads = []
    for hd in range(CROSS_HEADS):
        sl = slice(hd * hdim, (hd + 1) * hdim)
        qh = (_rms(q[:, sl], qg_ref[...]) * scale).astype(BF16)
        s = lax.dot_general(qh, kc_ref[0, :, sl], (((1,), (1,)), ((), ())), preferred_element_type=F32)
        s = s - jnp.max(s, axis=-1, keepdims=True)
        e = jnp.exp(s)
        pr = e / jnp.sum(e, axis=-1, keepdims=True)
        heads.append(jnp.dot(pr.astype(BF16), vc_ref[0, :, sl], preferred_element_type=F32).astype(BF16))
    o = jnp.concatenate(heads, axis=-1)
    x2 = x1 + jnp.dot(o, wo_ref[...], preferred_element_type=F32)
    hm = _rms(x2, mng_ref[...])

    hm_hi = hm.astype(BF16)
    hm_lo = (hm - hm_hi.astype(F32)).astype(BF16)
    logits = jnp.dot(jnp.concatenate([hm_hi, hm_lo, hm_hi], axis=-1), wr_ref[...], preferred_element_type=F32)
    biased = logits + br_ref[...]
    lane = lax.broadcasted_iota(jnp.int32, logits.shape, 1)
    lane_f = lane.astype(F32)
    big = float(4 * LANES)
    is_g = (lane >= N_EXPERTS) & (lane < N_EXPERTS + N_GROUPS)
    gl = jnp.where(is_g, logits, NEG_INF)
    gexp = jnp.where(is_g, jnp.exp(gl - jnp.max(gl, axis=-1, keepdims=True)), 0.0)
    gb = jnp.where(is_g, biased, NEG_INF)
    g_lane = jnp.min(jnp.where(gb == jnp.max(gb, axis=-1, keepdims=True), lane_f, big), axis=-1, keepdims=True)
    g_w = (jnp.sum(jnp.where(lane_f == g_lane, gexp, 0.0), axis=-1, keepdims=True)
           / jnp.sum(gexp, axis=-1, keepdims=True))
    g_idx = g_lane.astype(jnp.int32) - N_EXPERTS
    in_grp = (lane < N_EXPERTS) & ((lane // EXPERTS_PER_GROUP) == g_idx)
    el = jnp.where(in_grp, logits, NEG_INF)
    eexp = jnp.where(in_grp, jnp.exp(el - jnp.max(el, axis=-1, keepdims=True)), 0.0)
    eb = jnp.where(in_grp, biased, NEG_INF)
    i1 = jnp.min(jnp.where(eb == jnp.max(eb, axis=-1, keepdims=True), lane_f, big), axis=-1, keepdims=True)
    eb2 = jnp.where(lane_f == i1, NEG_INF, eb)
    i2 = jnp.min(jnp.where(eb2 == jnp.max(eb2, axis=-1, keepdims=True), lane_f, big), axis=-1, keepdims=True)
    psel = jnp.where((lane_f == i1) | (lane_f == i2), eexp, 0.0)
    gate = psel / jnp.sum(psel, axis=-1, keepdims=True) * g_w
    grp_onehot = (lane == g_idx).astype(F32)
    return x2, hm.astype(BF16), gate, grp_onehot


def _group_sort(hm, gate, grp, hms_out, pos_out, meta_out):
    t = hm.shape[0]
    ts = hms_out.shape[0]
    earlier = (lax.broadcasted_iota(jnp.int32, (t, t), 1) < lax.broadcasted_iota(jnp.int32, (t, t), 0))
    cnt = jnp.dot(earlier.astype(BF16), grp.astype(BF16), preferred_element_type=F32)
    n = jnp.sum(grp, axis=0, keepdims=True)
    n_al = jnp.floor((n + (SORT_ALIGN - 1.0)) * (1.0 / SORT_ALIGN)) * SORT_ALIGN
    lane1 = lax.broadcasted_iota(jnp.int32, (1, LANES), 1)
    start = jnp.zeros((1, LANES), F32)
    run = jnp.zeros((1, 1), F32)
    for g in range(1, N_GROUPS):
        run = run + jnp.sum(jnp.where(lane1 == g - 1, n_al, 0.0), axis=-1, keepdims=True)
        start = jnp.where(lane1 == g, run, start)
    pos = jnp.sum(grp * (cnt + start), axis=-1, keepdims=True)
    pos_out[...] = pos
    row8 = lax.broadcasted_iota(jnp.int32, (8, LANES), 0)
    meta_out[0] = jnp.where(row8 == 0, start, jnp.where(row8 == 1, n, 0.0)).astype(jnp.int32)
    pos_row = jnp.broadcast_to(pos, (t, LANES)).T[0:1, :]
    perm = (lax.broadcasted_iota(jnp.int32, (ts, t), 0).astype(F32) == pos_row).astype(BF16)
    g_hi = gate.astype(BF16).astype(F32)
    rem = gate - g_hi
    g_mid = rem.astype(BF16).astype(F32)
    g_lo = rem - g_mid
    gate_pieces = (g_hi + pltpu.roll(g_mid, N_EXPERTS, axis=1) + pltpu.roll(g_lo, 2 * N_EXPERTS, axis=1)).astype(BF16)
    hms_out[...] = jnp.dot(perm, jnp.concatenate([hm, gate_pieces], axis=-1),
                           preferred_element_type=F32).astype(BF16)


def _mix_kernel(x_ref, a_ref, r_ref, *rest, chains):
    *w_refs, x2_out, hms_out, pos_out, meta_out, hm_sc, gate_sc, grp_sc = rest
    rows = x_ref.shape[0] // chains
    for c in range(chains):
        sl = slice(c * rows, (c + 1) * rows)
        x2, hm, gate, grp = _mix_rows(x_ref[sl, :], a_ref[sl, :], r_ref[sl, :], *w_refs)
        x2_out[sl, :] = x2
        hm_sc[sl, :] = hm
        gate_sc[sl, :] = gate
        grp_sc[sl, :] = grp
    _group_sort(hm_sc[...], gate_sc[...], grp_sc[...], hms_out, pos_out, meta_out)


def _mix(x2d, a2d, r2d, kc, vc, p, batch, seq, tile):
    n_tok, d = x2d.shape
    half = a2d.shape[-1]
    tiles_per_b = seq // tile
    n_tiles = n_tok // tile
    tok = lambda i: (i, 0)
    mem_blk = pl.BlockSpec((1,) + kc.shape[1:], lambda i: (i // tiles_per_b, 0, 0))
    consts = [p['w_out_a'], p['w_out_r'], p['cross_norm_g'], p['cross_w_q'], p['cross_q_g'], p['cross_w_o'],
              p['moe_norm_g'], p['w_router'], p['b_router']]
    return pl.pallas_call(
        functools.partial(_mix_kernel, chains=MIX_CHAINS),
        grid=(n_tiles,),
        in_specs=[pl.BlockSpec((tile, d), tok), pl.BlockSpec((tile, half), tok), pl.BlockSpec((tile, half), tok),
                  mem_blk, mem_blk] + [_const_spec(c.shape) for c in consts],
        out_specs=(pl.BlockSpec((tile, d), tok), pl.BlockSpec((SORT_SLOTS, d + LANES), tok),
                   pl.BlockSpec((tile, 1), tok), pl.BlockSpec((1, 8, LANES), lambda i: (i, 0, 0))),
        out_shape=(jax.ShapeDtypeStruct((n_tok, d), F32),
                   jax.ShapeDtypeStruct((n_tiles * SORT_SLOTS, d + LANES), BF16),
                   jax.ShapeDtypeStruct((n_tok, 1), F32),
                   jax.ShapeDtypeStruct((n_tiles, 8, LANES), jnp.int32)),
        scratch_shapes=[pltpu.VMEM((tile, d), BF16), pltpu.VMEM((tile, LANES), F32), pltpu.VMEM((tile, LANES), F32)],
        compiler_params=pltpu.CompilerParams(dimension_semantics=("parallel",),
                                             vmem_limit_bytes=VMEM_LIMIT_BYTES),
        name="mix",
    )(x2d, a2d, r2d, kc, vc, *consts)


def _moe_kernel(start_ref, count_ref, hms_ref, wgu_ref, wd_ref, o_ref, *, group, tiles, chunk):
    blk = pl.program_id(0)
    step = pl.program_id(1)
    d = wd_ref.shape[1]
    ff = wgu_ref.shape[2] // 2
    grp = step // (EXPERTS_PER_GROUP // group)

    @pl.when(step == 0)
    def _():
        o_ref[...] = jnp.zeros(o_ref.shape, o_ref.dtype)

    lane = lax.broadcasted_iota(jnp.int32, (chunk, LANES), 1)
    piece_lane = lane < 3 * N_EXPERTS

    def tile_body(tl, carry):
        idx = (blk * tiles + tl) * N_GROUPS + grp
        base = tl * SORT_SLOTS + start_ref[idx]
        n_chunks = (count_ref[idx] + chunk - 1) // chunk

        def chunk_body(c, carry2):
            r0 = pl.multiple_of(base + c * chunk, SORT_ALIGN)
            xs = hms_ref[pl.ds(r0, chunk), 0:d]
            gates = hms_ref[pl.ds(r0, chunk), d:d + LANES].astype(F32)
            acts = []
            for j in range(group):
                e = step * group + j
                au = jnp.dot(xs, wgu_ref[j], preferred_element_type=F32)
                a = au[:, :ff]
                u = au[:, ff:]
                g_e = jnp.sum(jnp.where(piece_lane & ((lane & (N_EXPERTS - 1)) == e), gates, 0.0),
                              axis=-1, keepdims=True)
                acts.append((a * jax.nn.sigmoid(a) * u * g_e).astype(BF16))
            y = jnp.dot(jnp.concatenate(acts, axis=-1), wd_ref[...], preferred_element_type=F32)
            o_ref[pl.ds(r0, chunk), :] = (o_ref[pl.ds(r0, chunk), :].astype(F32) + y).astype(o_ref.dtype)
            return carry2

        lax.fori_loop(0, n_chunks, chunk_body, 0)
        return carry

    lax.fori_loop(0, tiles, tile_body, 0)


def _moe(hms, starts, counts, w_gu, w_d, tiles, group, chunk):
    n_rows, dx = hms.shape
    n_exp, d, ff2 = w_gu.shape
    ff = ff2 // 2
    rows = tiles * SORT_SLOTS
    grid_spec = pltpu.PrefetchScalarGridSpec(
        num_scalar_prefetch=2,
        grid=(n_rows // rows, n_exp // group),
        in_specs=[pl.BlockSpec((rows, dx), lambda i, e, s, c: (i, 0)),
                  pl.BlockSpec((group, d, ff2), lambda i, e, s, c: (e, 0, 0)),
                  pl.BlockSpec((group * ff, d), lambda i, e, s, c: (e, 0))],
        out_specs=pl.BlockSpec((rows, d), lambda i, e, s, c: (i, 0)),
    )
    return pl.pallas_call(
        functools.partial(_moe_kernel, group=group, tiles=tiles, chunk=chunk),
        grid_spec=grid_spec,
        out_shape=jax.ShapeDtypeStruct((n_rows, d), BF16),
        compiler_params=pltpu.CompilerParams(dimension_semantics=("parallel", "arbitrary"),
                                             vmem_limit_bytes=VMEM_LIMIT_BYTES),
        name="moe",
    )(starts, counts, hms, w_gu, w_d.reshape(n_exp * ff, d))


def _unsort_kernel(x2_ref, pos_ref, ys_ref, o_ref):
    t = x2_ref.shape[0]
    ts = ys_ref.shape[0]
    unperm = (lax.broadcasted_iota(jnp.int32, (t, ts), 1).astype(F32) == pos_ref[...]).astype(BF16)
    o_ref[...] = x2_ref[...] + jnp.dot(unperm, ys_ref[...], preferred_element_type=F32)


def _unsort(x2, pos, ys, tile):
    n_tok, d = x2.shape
    tok = lambda i: (i, 0)
    return pl.pallas_call(
        _unsort_kernel,
        grid=(n_tok // tile,),
        in_specs=[pl.BlockSpec((tile, d), tok), pl.BlockSpec((tile, 1), tok), pl.BlockSpec((SORT_SLOTS, d), tok)],
        out_specs=pl.BlockSpec((tile, d), tok),
        out_shape=jax.ShapeDtypeStruct((n_tok, d), F32),
        compiler_params=pltpu.CompilerParams(dimension_semantics=("parallel",)),
        name="unsort",
    )(x2, pos, ys)


def _row(v):
    return v.reshape(1, -1).astype(F32)


def _pad_lanes(v, width=LANES):
    return jnp.pad(v, [(0, 0)] * (v.ndim - 1) + [(0, width - v.shape[-1])])


def _swap_halves(v):
    half = v.shape[-1] // 2
    return jnp.concatenate([v[..., half:], v[..., :half]], axis=-1)


def _layout_params(l, attn_norm_g, w_in, mla_q_norm_g, mla_w_uq, mla_kv_norm_g, mla_w_ukv, mla_q_qk_g,
                   mla_k_qk_g, ret_gn_g, w_out, cross_norm_g, mem_norm_g, cross_w_q, cross_w_kv, cross_q_qk_g,
                   cross_k_qk_g, cross_w_o, moe_norm_g, router_w_group, router_b_group, router_w_expert,
                   router_b_expert):
    p = {}
    d = w_in.shape[1]
    wi = w_in[l]
    c_q, c_kv, k_pe, rq, rk, rv, rg = jnp.split(
        wi, np.cumsum([MLA_Q_RANK, MLA_KV_RANK, MLA_ROPE_DIM, RET_WIDTH, RET_WIDTH, RET_WIDTH]).tolist(), axis=1)
    p['w_in'] = jnp.concatenate([c_q, c_kv, rq, rk, rv, rg, _pad_lanes(k_pe), _pad_lanes(_swap_halves(k_pe))],
                                axis=1).astype(BF16)
    wq = mla_w_uq[l].reshape(MLA_Q_RANK, MLA_HEADS, MLA_QK_DIM)
    wq_rope = wq[:, :, MLA_NOPE_DIM:]
    p['w_uq'] = jnp.concatenate([wq[:, :, :MLA_NOPE_DIM], _pad_lanes(wq_rope), _pad_lanes(_swap_halves(wq_rope))],
                                axis=2).reshape(MLA_Q_RANK, MLA_HEADS * Q_HEAD_COLS).astype(BF16)
    wkv = mla_w_ukv[l].reshape(MLA_KV_RANK, MLA_HEADS, MLA_NOPE_DIM + MLA_V_DIM)
    p['w_ukv'] = jnp.concatenate([wkv[:, :, :MLA_NOPE_DIM].reshape(MLA_KV_RANK, -1),
                                  wkv[:, :, MLA_NOPE_DIM:].reshape(MLA_KV_RANK, -1)], axis=1).astype(BF16)
    p['attn_norm_g'] = _row(attn_norm_g[l])
    p['q_norm_g'] = _row(mla_q_norm_g[l])
    p['kv_norm_g'] = _row(mla_kv_norm_g[l])
    for name, g in (('q', mla_q_qk_g[l]), ('k', mla_k_qk_g[l])):
        p[name + '_g_nope'] = _row(g[:MLA_NOPE_DIM])
        p[name + '_g_rope'] = _row(_pad_lanes(g[MLA_NOPE_DIM:]))
        p[name + '_g_rope_sw'] = _row(_pad_lanes(_swap_halves(g[MLA_NOPE_DIM:])))
    hm_ = MLA_ROPE_DIM // 2
    f_m = ROPE_BASE ** (-jnp.arange(hm_, dtype=F32) / hm_)
    p['freq_m'] = _row(_pad_lanes(jnp.concatenate([f_m, f_m])))
    p['sign_m'] = _row(_pad_lanes(jnp.concatenate([-jnp.ones(hm_, F32), jnp.ones(hm_, F32)])))
    hr_ = RET_HEAD_DIM // 2
    f_r = ROPE_BASE ** (-jnp.arange(hr_, dtype=F32) / hr_)
    p['freq_r'] = _row(jnp.concatenate([f_r, f_r]))
    p['sign_r'] = _row(jnp.concatenate([-jnp.ones(hr_, F32), jnp.ones(hr_, F32)]))
    p['ret_gn_g'] = _row(ret_gn_g[l])
    half = MLA_HEADS * MLA_V_DIM
    p['w_out_a'] = w_out[l][:half].astype(BF16)
    p['w_out_r'] = w_out[l][half:].astype(BF16)
    p['cross_norm_g'] = _row(cross_norm_g[l])
    p['mem_norm_g'] = _row(mem_norm_g[l])
    p['cross_w_q'] = cross_w_q[l].astype(BF16)
    p['cross_w_kv'] = cross_w_kv[l].astype(BF16)
    p['cross_q_g'] = _row(cross_q_qk_g[l])
    p['cross_k_g'] = _row(cross_k_qk_g[l])
    p['cross_w_o'] = cross_w_o[l].astype(BF16)
    p['moe_norm_g'] = _row(moe_norm_g[l])
    w_r = _pad_lanes(jnp.concatenate([router_w_expert[l], router_w_group[l]], axis=1)).astype(F32)
    w_r_hi = w_r.astype(BF16)
    w_r_lo = (w_r - w_r_hi.astype(F32)).astype(BF16)
    p['w_router'] = jnp.concatenate([w_r_hi, w_r_hi, w_r_lo], axis=0)
    p['b_router'] = _row(_pad_lanes(jnp.concatenate([router_b_expert[l], router_b_group[l]])))
    return p


PREP_TILE = 512
PREP_CHAINS = 1
FLASH_TQ = 1024
FLASH_TK = 1024
FLASH_HEADS_PER_STEP = 4
MIX_TILE = 512
MIX_CHAINS = 2
SORT_ALIGN = 16
MOE_CHUNK = 144
SORT_SLOTS = -(-(MIX_TILE + N_GROUPS * (SORT_ALIGN - 1) + MOE_CHUNK) // LANES) * LANES
MOE_TILES_PER_BLOCK = 4
MOE_EXPERTS_PER_STEP = 4


def kernel(x, mem, positions, attn_norm_g, w_in, mla_q_norm_g, mla_w_uq, mla_kv_norm_g, mla_w_ukv, mla_q_qk_g, mla_k_qk_g, ret_gn_g, w_out, cross_norm_g, mem_norm_g, cross_w_q, cross_w_kv, cross_q_qk_g, cross_k_qk_g, cross_w_o, moe_norm_g, router_w_group, router_b_group, router_w_expert, router_b_expert, expert_w_gate, expert_w_up, expert_w_down):
    batch, seq, d = x.shape
    depth = w_in.shape[0]
    pos2d = positions.reshape(batch * seq, 1).astype(jnp.int32)
    x2d = x.reshape(batch * seq, d)
    for l in range(depth):
        p = _layout_params(l, attn_norm_g, w_in, mla_q_norm_g, mla_w_uq, mla_kv_norm_g, mla_w_ukv, mla_q_qk_g,
                           mla_k_qk_g, ret_gn_g, w_out, cross_norm_g, mem_norm_g, cross_w_q, cross_w_kv,
                           cross_q_qk_g, cross_k_qk_g, cross_w_o, moe_norm_g, router_w_group, router_b_group,
                           router_w_expert, router_b_expert)
        w_gu = jnp.concatenate([expert_w_gate[l], expert_w_up[l]], axis=-1).astype(BF16)
        w_d = expert_w_down[l].astype(BF16)

        q, k, v, rq, rk, rv, rg = _prep(x2d, pos2d, p, batch, seq, PREP_TILE)
        a_out = _flash(q, k, v, FLASH_TQ, FLASH_TK, FLASH_HEADS_PER_STEP)
        shp = (batch, seq, RET_WIDTH)
        r_out = _retention(rq.reshape(shp), rk.reshape(shp), rv.reshape(shp), rg.reshape(shp), p['ret_gn_g'])
        kc, vc = _memkv(mem, p['mem_norm_g'], p['cross_w_kv'], p['cross_k_g'])
        x2, hms, pos, meta = _mix(x2d, a_out.reshape(batch * seq, -1), r_out.reshape(batch * seq, -1), kc, vc, p,
                                  batch, seq, MIX_TILE)
        starts = meta[:, 0, :N_GROUPS].reshape(-1)
        counts = meta[:, 1, :N_GROUPS].reshape(-1)
        ys = _moe(hms, starts, counts, w_gu, w_d, MOE_TILES_PER_BLOCK, MOE_EXPERTS_PER_STEP, MOE_CHUNK)
        x2d = _unsort(x2, pos, ys, MIX_TILE)
    return x2d.reshape(batch, seq, d)
```

```python
import functools
import math

import numpy as np
import jax
import jax.numpy as jnp
from jax import lax
from jax.experimental import pallas as pl
from jax.experimental.pallas import tpu as pltpu

F32 = jnp.float32
BF16 = jnp.bfloat16

LANES = 128
VMEM_LIMIT_BYTES = 56 * 1024 * 1024

MLA_HEADS = 4
MLA_NOPE_DIM = 128
MLA_ROPE_DIM = 64
MLA_QK_DIM = MLA_NOPE_DIM + MLA_ROPE_DIM
MLA_V_DIM = 128
MLA_Q_RANK = 256
MLA_KV_RANK = 128
MLA_HEAD_PAD = 256
RET_HEADS = 4
RET_HEAD_DIM = 128
RET_WIDTH = RET_HEADS * RET_HEAD_DIM
RET_CHUNK = 256
CROSS_HEADS = 4
N_GROUPS = 4
EXPERTS_PER_GROUP = 8
N_EXPERTS = 32
ROPE_BASE = 10000.0
EPS = 1e-6
NEG_INF = -1e30

_C_Q = 0
_C_KV = _C_Q + MLA_Q_RANK
_RQ = _C_KV + MLA_KV_RANK
_RK = _RQ + RET_WIDTH
_RV = _RK + RET_WIDTH
_RG = _RV + RET_WIDTH
_KPE = _RG + RET_WIDTH
_KPE_SW = _KPE + LANES
IN_COLS_PAD = _KPE_SW + LANES
Q_HEAD_COLS = 3 * LANES


def _rms(x, g):
    return x * lax.rsqrt(jnp.mean(x * x, axis=-1, keepdims=True) + EPS) * g


def _const_spec(shape):
    nd = len(shape)
    return pl.BlockSpec(shape, lambda *_: (0,) * nd, pipeline_mode=pl.Buffered(1))


def _prep_rows(rs, x_ref, pos_ref, g_ref, win_ref, qng_ref, wuq_ref, kvng_ref, wukv_ref,
               qgn_ref, qgr_ref, qgs_ref, kgn_ref, kgr_ref, kgs_ref,
               fm_ref, sm_ref, fr_ref, sr_ref,
               q_out, k_out, v_out, rq_out, rk_out, rv_out, rg_out):
    x = x_ref[rs, :]
    h = _rms(x, g_ref[...]).astype(BF16)
    proj = jnp.dot(h, win_ref[...], preferred_element_type=F32)

    posf = pos_ref[rs, :].astype(F32)
    ang_m = posf * fm_ref[...]
    cos_m = jnp.cos(ang_m)
    sin_m = jnp.sin(ang_m) * sm_ref[...]
    ang_r = posf * fr_ref[...]
    cos_r = jnp.cos(ang_r)
    sin_r = jnp.sin(ang_r) * sr_ref[...]

    c_q = proj[:, _C_Q:_C_Q + MLA_Q_RANK]
    qall = jnp.dot(_rms(c_q, qng_ref[...]).astype(BF16), wuq_ref[...], preferred_element_type=F32)
    q_scale = MLA_QK_DIM ** -0.5 * math.log2(math.e)
    for hd in range(MLA_HEADS):
        base = hd * Q_HEAD_COLS
        nope = qall[:, base:base + LANES]
        rope = qall[:, base + LANES:base + 2 * LANES]
        rope_sw = qall[:, base + 2 * LANES:base + 3 * LANES]
        ss = jnp.sum(nope * nope, axis=-1, keepdims=True) + jnp.sum(rope * rope, axis=-1, keepdims=True)
        r = lax.rsqrt(ss * (1.0 / MLA_QK_DIM) + EPS) * q_scale
        roped = rope * qgr_ref[...] * cos_m + rope_sw * qgs_ref[...] * sin_m
        q_out[0, hd, rs,0:LANES] = (nope * qgn_ref[...] * r).astype(BF16)
        q_out[0, hd, rs,LANES:2 * LANES] = (roped * r).astype(BF16)

    c_kv = proj[:, _C_KV:_C_KV + MLA_KV_RANK]
    kv = jnp.dot(_rms(c_kv, kvng_ref[...]).astype(BF16), wukv_ref[...], preferred_element_type=F32)
    k_pe = proj[:, _KPE:_KPE + LANES]
    k_pe_sw = proj[:, _KPE_SW:_KPE_SW + LANES]
    kr = k_pe * kgr_ref[...] * cos_m + k_pe_sw * kgs_ref[...] * sin_m
    ss_pe = jnp.sum(k_pe * k_pe, axis=-1, keepdims=True)
    for hd in range(MLA_HEADS):
        k_nope = kv[:, hd * LANES:(hd + 1) * LANES]
        ss = jnp.sum(k_nope * k_nope, axis=-1, keepdims=True) + ss_pe
        r = lax.rsqrt(ss * (1.0 / MLA_QK_DIM) + EPS)
        k_out[0, hd, rs,0:LANES] = (k_nope * kgn_ref[...] * r).astype(BF16)
        k_out[0, hd, rs,LANES:2 * LANES] = (kr * r).astype(BF16)
        v_out[0, hd, rs, :] = kv[:, (MLA_HEADS + hd) * LANES:(MLA_HEADS + hd + 1) * LANES].astype(BF16)

    k_scale = RET_HEAD_DIM ** -0.5
    for hd in range(RET_HEADS):
        sl = slice(hd * LANES, (hd + 1) * LANES)
        rq = proj[:, _RQ + hd * LANES:_RQ + (hd + 1) * LANES]
        rk = proj[:, _RK + hd * LANES:_RK + (hd + 1) * LANES]
        rq_out[rs, sl] =(rq * cos_r + pltpu.roll(rq, LANES // 2, axis=1) * sin_r).astype(BF16)
        rk_out[rs, sl] =((rk * cos_r + pltpu.roll(rk, LANES // 2, axis=1) * sin_r) * k_scale).astype(BF16)
    rv_out[rs, :] = proj[:, _RV:_RV + RET_WIDTH].astype(BF16)
    rg_out[rs, :] = proj[:, _RG:_RG + RET_WIDTH].astype(BF16)


def _prep_kernel(x_ref, *refs, chains):
    rows = x_ref.shape[0] // chains
    for c in range(chains):
        _prep_rows(slice(c * rows, (c + 1) * rows), x_ref, *refs)


def _prep(x2d, pos2d, p, batch, seq, tile):
    n_tok, d = x2d.shape
    tiles_per_b = seq // tile
    tok = lambda i: (i, 0)
    head = lambda i: (i // tiles_per_b, 0, i % tiles_per_b, 0)
    consts = [p['attn_norm_g'], p['w_in'], p['q_norm_g'], p['w_uq'], p['kv_norm_g'], p['w_ukv'],
              p['q_g_nope'], p['q_g_rope'], p['q_g_rope_sw'], p['k_g_nope'], p['k_g_rope'], p['k_g_rope_sw'],
              p['freq_m'], p['sign_m'], p['freq_r'], p['sign_r']]
    out_shape = (
        jax.ShapeDtypeStruct((batch, MLA_HEADS, seq, MLA_HEAD_PAD), BF16),
        jax.ShapeDtypeStruct((batch, MLA_HEADS, seq, MLA_HEAD_PAD), BF16),
        jax.ShapeDtypeStruct((batch, MLA_HEADS, seq, MLA_V_DIM), BF16),
        jax.ShapeDtypeStruct((n_tok, RET_WIDTH), BF16),
        jax.ShapeDtypeStruct((n_tok, RET_WIDTH), BF16),
        jax.ShapeDtypeStruct((n_tok, RET_WIDTH), BF16),
        jax.ShapeDtypeStruct((n_tok, RET_WIDTH), BF16),
    )
    return pl.pallas_call(
        functools.partial(_prep_kernel, chains=PREP_CHAINS),
        grid=(n_tok // tile,),
        in_specs=[pl.BlockSpec((tile, d), tok), pl.BlockSpec((tile, 1), tok)]
                 + [_const_spec(c.shape) for c in consts],
        out_specs=(
            pl.BlockSpec((1, MLA_HEADS, tile, MLA_HEAD_PAD), head),
            pl.BlockSpec((1, MLA_HEADS, tile, MLA_HEAD_PAD), head),
            pl.BlockSpec((1, MLA_HEADS, tile, MLA_V_DIM), head),
            pl.BlockSpec((tile, RET_WIDTH), tok),
            pl.BlockSpec((tile, RET_WIDTH), tok),
            pl.BlockSpec((tile, RET_WIDTH), tok),
            pl.BlockSpec((tile, RET_WIDTH), tok),
        ),
        out_shape=out_shape,
        compiler_params=pltpu.CompilerParams(dimension_semantics=("parallel",),
                                             vmem_limit_bytes=VMEM_LIMIT_BYTES),
        name="prep",
    )(x2d, pos2d, *consts)


def _flash_kernel(q_ref, k_ref, v_ref, o_ref, m_sc, l_sc, acc_sc, *, tq, tk, hb):
    qi = pl.program_id(2)
    m_sc[...] = jnp.full(m_sc.shape, NEG_INF, F32)
    l_sc[...] = jnp.zeros(l_sc.shape, F32)
    acc_sc[...] = jnp.zeros(acc_sc.shape, F32)

    def step(off, width, rows, mask):
        reps = width // LANES
        for hd in range(hb):
            k = k_ref[0, hd, pl.ds(off, width), :]
            v = v_ref[0, hd, pl.ds(off, width), :]
            s = lax.dot_general(q_ref[0, hd, rows, :], k, (((1,), (1,)), ((), ())), preferred_element_type=F32)
            if mask is not None:
                s = jnp.where(mask, s, NEG_INF)
            m_prev = m_sc[hd, rows, :]
            m_new = jnp.maximum(m_prev, jnp.max(s, axis=-1, keepdims=True))
            alpha = jnp.exp2(m_prev - m_new)
            p = jnp.exp2(s - jnp.concatenate([m_new] * reps, axis=1))
            p_cols = p[:, 0:LANES]
            for c in range(1, reps):
                p_cols = p_cols + p[:, c * LANES:(c + 1) * LANES]
            l_sc[hd, rows, :] = alpha * l_sc[hd, rows, :] + p_cols
            acc_sc[hd, rows, :] = alpha * acc_sc[hd, rows, :] + jnp.dot(p.astype(BF16), v,
                                                                        preferred_element_type=F32)
            m_sc[hd, rows, :] = m_new

    def body(j, c):
        step(pl.multiple_of(j * tk, tk), tk, slice(0, tq), None)
        return c

    lax.fori_loop(0, qi, body, 0)
    half = tk // 2
    base = pl.multiple_of(qi * tk, tk)
    def causal(n_rows):
        return (lax.broadcasted_iota(jnp.int32, (n_rows, half), 1)
                <= lax.broadcasted_iota(jnp.int32, (n_rows, half), 0))

    step(base, half, slice(0, tq), causal(tq))
    step(base + half, half, slice(half, tq), causal(tq - half))
    for hd in range(hb):
        l = jnp.sum(l_sc[hd], axis=-1, keepdims=True)
        o_ref[0, :, hd * LANES:(hd + 1) * LANES] = (acc_sc[hd] / l).astype(o_ref.dtype)


def _flash(q, k, v, tq, tk, hb):
    batch, heads, seq, dqk = q.shape
    dv = v.shape[-1]
    assert tq == tk and seq % tq == 0 and heads % hb == 0
    return pl.pallas_call(
        functools.partial(_flash_kernel, tq=tq, tk=tk, hb=hb),
        grid=(batch, heads // hb, seq // tq),
        in_specs=[
            pl.BlockSpec((1, hb, tq, dqk), lambda b, h, i: (b, h, i, 0)),
            pl.BlockSpec((1, hb, seq, dqk), lambda b, h, i: (b, h, 0, 0)),
            pl.BlockSpec((1, hb, seq, dv), lambda b, h, i: (b, h, 0, 0)),
        ],
        out_specs=pl.BlockSpec((1, tq, hb * dv), lambda b, h, i: (b, i, h)),
        out_shape=jax.ShapeDtypeStruct((batch, seq, heads * dv), BF16),
        scratch_shapes=[pltpu.VMEM((hb, tq, LANES), F32), pltpu.VMEM((hb, tq, LANES), F32),
                        pltpu.VMEM((hb, tq, dv), F32)],
        compiler_params=pltpu.CompilerParams(dimension_semantics=("parallel", "parallel", "arbitrary"),
                                             vmem_limit_bytes=VMEM_LIMIT_BYTES),
        name="flash",
    )(q, k, v)


def _ret_kernel(q_ref, k_ref, v_ref, g_ref, gn_ref, o_ref, state_sc):
    c = pl.program_id(1)

    @pl.when(c == 0)
    def _():
        state_sc[...] = jnp.zeros(state_sc.shape, F32)

    n = RET_CHUNK
    ri = lax.broadcasted_iota(jnp.int32, (n, n), 0)
    ci = lax.broadcasted_iota(jnp.int32, (n, n), 1)
    diff = (ri - ci).astype(F32)
    idx = lax.broadcasted_iota(jnp.int32, (n, 1), 0).astype(F32)
    for hd in range(RET_HEADS):
        lg = math.log(1.0 - 2.0 ** (-5.0 - hd))
        intra = jnp.where(diff >= 0, jnp.exp(jnp.maximum(diff, 0.0) * lg), 0.0)
        q_decay = jnp.exp((idx + 1.0) * lg)
        k_decay = jnp.exp((n - 1.0 - idx) * lg)
        chunk_decay = math.exp(n * lg)
        sl = slice(hd * LANES, (hd + 1) * LANES)
        qc = q_ref[0, :, sl]
        kc = k_ref[0, :, sl]
        vc = v_ref[0, :, sl]
        state = state_sc[hd]
        scores = lax.dot_general(qc, kc, (((1,), (1,)), ((), ())), preferred_element_type=F32) * intra
        o = (jnp.dot(scores.astype(BF16), vc, preferred_element_type=F32)
             + jnp.dot((qc.astype(F32) * q_decay).astype(BF16), state.astype(BF16), preferred_element_type=F32))
        kd_t = (kc.astype(F32) * k_decay).T.astype(BF16)
        state_sc[hd] = state * chunk_decay + jnp.dot(kd_t, vc, preferred_element_type=F32)
        mu = jnp.mean(o, axis=-1, keepdims=True)
        cen = o - mu
        var = jnp.mean(cen * cen, axis=-1, keepdims=True)
        gate = g_ref[0, :, sl].astype(F32)
        y = cen * lax.rsqrt(var + EPS) * gn_ref[:, sl] * (gate * jax.nn.sigmoid(gate))
        o_ref[0, :, sl] = y.astype(o_ref.dtype)


def _retention(rq, rk, rv, rg, gn_g):
    batch, seq, width = rq.shape
    blk = pl.BlockSpec((1, RET_CHUNK, width), lambda b, c: (b, c, 0))
    return pl.pallas_call(
        _ret_kernel,
        grid=(batch, seq // RET_CHUNK),
        in_specs=[blk, blk, blk, blk, _const_spec(gn_g.shape)],
        out_specs=blk,
        out_shape=jax.ShapeDtypeStruct((batch, seq, width), BF16),
        scratch_shapes=[pltpu.VMEM((RET_HEADS, RET_HEAD_DIM, RET_HEAD_DIM), F32)],
        compiler_params=pltpu.CompilerParams(dimension_semantics=("parallel", "arbitrary")),
        name="retention",
    )(rq, rk, rv, rg, gn_g)


def _memkv_kernel(mem_ref, g_ref, wkv_ref, kg_ref, k_out, v_out):
    d = mem_ref.shape[-1]
    hdim = d // CROSS_HEADS
    m = _rms(mem_ref[0], g_ref[...]).astype(BF16)
    kv = jnp.dot(m, wkv_ref[...], preferred_element_type=F32)
    for hd in range(CROSS_HEADS):
        sl = slice(hd * hdim, (hd + 1) * hdim)
        k_out[0, :, sl] = _rms(kv[:, sl], kg_ref[...]).astype(BF16)
    v_out[0] = kv[:, d:].astype(BF16)


def _memkv(mem, mem_norm_g, w_kv, k_g):
    batch, m_tok, d = mem.shape
    blk = pl.BlockSpec((1, m_tok, d), lambda b: (b, 0, 0))
    return pl.pallas_call(
        _memkv_kernel,
        grid=(batch,),
        in_specs=[blk, _const_spec(mem_norm_g.shape), _const_spec(w_kv.shape), _const_spec(k_g.shape)],
        out_specs=(blk, blk),
        out_shape=(jax.ShapeDtypeStruct((batch, m_tok, d), BF16), jax.ShapeDtypeStruct((batch, m_tok, d), BF16)),
        compiler_params=pltpu.CompilerParams(dimension_semantics=("parallel",),
                                             vmem_limit_bytes=VMEM_LIMIT_BYTES),
        name="memkv",
    )(mem, mem_norm_g, w_kv, k_g)


def _mix_rows(x, a, r, kc_ref, vc_ref, woa_ref, wor_ref, cng_ref, wq_ref, qg_ref, wo_ref, mng_ref, wr_ref,
              br_ref):
    d = x.shape[-1]
    hdim = d // CROSS_HEADS
    x1 = (x
          + jnp.dot(a, woa_ref[...], preferred_element_type=F32)
          + jnp.dot(r, wor_ref[...], preferred_element_type=F32))
    hc = _rms(x1, cng_ref[...]).astype(BF16)
    q = jnp.dot(hc, wq_ref[...], preferred_element_type=F32)
    scale = hdim ** -0.5
    heads = []
    for hd in range(CROSS_HEADS):
        sl = slice(hd * hdim, (hd + 1) * hdim)
        qh = (_rms(q[:, sl], qg_ref[...]) * scale).astype(BF16)
        s = lax.dot_general(qh, kc_ref[0, :, sl], (((1,), (1,)), ((), ())), preferred_element_type=F32)
        s = s - jnp.max(s, axis=-1, keepdims=True)
        e = jnp.exp(s)
        pr = e / jnp.sum(e, axis=-1, keepdims=True)
        heads.append(jnp.dot(pr.astype(BF16), vc_ref[0, :, sl], preferred_element_type=F32).astype(BF16))
    o = jnp.concatenate(heads, axis=-1)
    x2 = x1 + jnp.dot(o, wo_ref[...], preferred_element_type=F32)
    hm = _rms(x2, mng_ref[...])

    hm_hi = hm.astype(BF16)
    hm_lo = (hm - hm_hi.astype(F32)).astype(BF16)
    logits = jnp.dot(jnp.concatenate([hm_hi, hm_lo, hm_hi], axis=-1), wr_ref[...], preferred_element_type=F32)
    biased = logits + br_ref[...]
    lane = lax.broadcasted_iota(jnp.int32, logits.shape, 1)
    lane_f = lane.astype(F32)
    big = float(4 * LANES)
    is_g = (lane >= N_EXPERTS) & (lane < N_EXPERTS + N_GROUPS)
    gl = jnp.where(is_g, logits, NEG_INF)
    gexp = jnp.where(is_g, jnp.exp(gl - jnp.max(gl, axis=-1, keepdims=True)), 0.0)
    gb = jnp.where(is_g, biased, NEG_INF)
    g_lane = jnp.min(jnp.where(gb == jnp.max(gb, axis=-1, keepdims=True), lane_f, big), axis=-1, keepdims=True)
    g_w = (jnp.sum(jnp.where(lane_f == g_lane, gexp, 0.0), axis=-1, keepdims=True)
           / jnp.sum(gexp, axis=-1, keepdims=True))
    g_idx = g_lane.astype(jnp.int32) - N_EXPERTS
    in_grp = (lane < N_EXPERTS) & ((lane // EXPERTS_PER_GROUP) == g_idx)
    el = jnp.where(in_grp, logits, NEG_INF)
    eexp = jnp.where(in_grp, jnp.exp(el - jnp.max(el, axis=-1, keepdims=True)), 0.0)
    eb = jnp.where(in_grp, biased, NEG_INF)
    i1 = jnp.min(jnp.where(eb == jnp.max(eb, axis=-1, keepdims=True), lane_f, big), axis=-1, keepdims=True)
    eb2 = jnp.where(lane_f == i1, NEG_INF, eb)
    i2 = jnp.min(jnp.where(eb2 == jnp.max(eb2, axis=-1, keepdims=True), lane_f, big), axis=-1, keepdims=True)
    psel = jnp.where((lane_f == i1) | (lane_f == i2), eexp, 0.0)
    gate = psel / jnp.sum(psel, axis=-1, keepdims=True) * g_w
    grp_onehot = (lane == g_idx).astype(F32)
    return x2, hm.astype(BF16), gate, grp_onehot


def _group_sort(hm, gate, grp, hms_out, pos_out, meta_out):
    t = hm.shape[0]
    ts = hms_out.shape[0]
    earlier = (lax.broadcasted_iota(jnp.int32, (t, t), 1) < lax.broadcasted_iota(jnp.int32, (t, t), 0))
    cnt = jnp.dot(earlier.astype(BF16), grp.astype(BF16), preferred_element_type=F32)
    n = jnp.sum(grp, axis=0, keepdims=True)
    n_al = jnp.floor((n + (SORT_ALIGN - 1.0)) * (1.0 / SORT_ALIGN)) * SORT_ALIGN
    lane1 = lax.broadcasted_iota(jnp.int32, (1, LANES), 1)
    start = jnp.zeros((1, LANES), F32)
    run = jnp.zeros((1, 1), F32)
    for g in range(1, N_GROUPS):
        run = run + jnp.sum(jnp.where(lane1 == g - 1, n_al, 0.0), axis=-1, keepdims=True)
        start = jnp.where(lane1 == g, run, start)
    pos = jnp.sum(grp * (cnt + start), axis=-1, keepdims=True)
    pos_out[...] = pos
    row8 = lax.broadcasted_iota(jnp.int32, (8, LANES), 0)
    meta_out[0] = jnp.where(row8 == 0, start, jnp.where(row8 == 1, n, 0.0)).astype(jnp.int32)
    pos_row = jnp.broadcast_to(pos, (t, LANES)).T[0:1, :]
    perm = (lax.broadcasted_iota(jnp.int32, (ts, t), 0).astype(F32) == pos_row).astype(BF16)
    g_hi = gate.astype(BF16).astype(F32)
    rem = gate - g_hi
    g_mid = rem.astype(BF16).astype(F32)
    g_lo = rem - g_mid
    gate_pieces = (g_hi + pltpu.roll(g_mid, N_EXPERTS, axis=1) + pltpu.roll(g_lo, 2 * N_EXPERTS, axis=1)).astype(BF16)
    hms_out[...] = jnp.dot(perm, jnp.concatenate([hm, gate_pieces], axis=-1),
                           preferred_element_type=F32).astype(BF16)


def _mix_kernel(x_ref, a_ref, r_ref, *rest, chains):
    *w_refs, x2_out, hms_out, pos_out, meta_out, hm_sc, gate_sc, grp_sc = rest
    rows = x_ref.shape[0] // chains
    for c in range(chains):
        sl = slice(c * rows, (c + 1) * rows)
        x2, hm, gate, grp = _mix_rows(x_ref[sl, :], a_ref[sl, :], r_ref[sl, :], *w_refs)
        x2_out[sl, :] = x2
        hm_sc[sl, :] = hm
        gate_sc[sl, :] = gate
        grp_sc[sl, :] = grp
    _group_sort(hm_sc[...], gate_sc[...], grp_sc[...], hms_out, pos_out, meta_out)


def _mix(x2d, a2d, r2d, kc, vc, p, batch, seq, tile):
    n_tok, d = x2d.shape
    half = a2d.shape[-1]
    tiles_per_b = seq // tile
    n_tiles = n_tok // tile
    tok = lambda i: (i, 0)
    mem_blk = pl.BlockSpec((1,) + kc.shape[1:], lambda i: (i // tiles_per_b, 0, 0))
    consts = [p['w_out_a'], p['w_out_r'], p['cross_norm_g'], p['cross_w_q'], p['cross_q_g'], p['cross_w_o'],
              p['moe_norm_g'], p['w_router'], p['b_router']]
    return pl.pallas_call(
        functools.partial(_mix_kernel, chains=MIX_CHAINS),
        grid=(n_tiles,),
        in_specs=[pl.BlockSpec((tile, d), tok), pl.BlockSpec((tile, half), tok), pl.BlockSpec((tile, half), tok),
                  mem_blk, mem_blk] + [_const_spec(c.shape) for c in consts],
        out_specs=(pl.BlockSpec((tile, d), tok), pl.BlockSpec((SORT_SLOTS, d + LANES), tok),
                   pl.BlockSpec((tile, 1), tok), pl.BlockSpec((1, 8, LANES), lambda i: (i, 0, 0))),
        out_shape=(jax.ShapeDtypeStruct((n_tok, d), F32),
                   jax.ShapeDtypeStruct((n_tiles * SORT_SLOTS, d + LANES), BF16),
                   jax.ShapeDtypeStruct((n_tok, 1), F32),
                   jax.ShapeDtypeStruct((n_tiles, 8, LANES), jnp.int32)),
        scratch_shapes=[pltpu.VMEM((tile, d), BF16), pltpu.VMEM((tile, LANES), F32), pltpu.VMEM((tile, LANES), F32)],
        compiler_params=pltpu.CompilerParams(dimension_semantics=("parallel",),
                                             vmem_limit_bytes=VMEM_LIMIT_BYTES),
        name="mix",
    )(x2d, a2d, r2d, kc, vc, *consts)


def _moe_kernel(start_ref, count_ref, hms_ref, wgu_ref, wd_ref, o_ref, *, group, tiles, chunk):
    blk = pl.program_id(0)
    step = pl.program_id(1)
    d = wd_ref.shape[1]
    ff = wgu_ref.shape[2] // 2
    grp = step // (EXPERTS_PER_GROUP // group)

    @pl.when(step == 0)
    def _():
        o_ref[...] = jnp.zeros(o_ref.shape, o_ref.dtype)

    lane = lax.broadcasted_iota(jnp.int32, (chunk, LANES), 1)
    piece_lane = lane < 3 * N_EXPERTS

    def tile_body(tl, carry):
        idx = (blk * tiles + tl) * N_GROUPS + grp
        base = tl * SORT_SLOTS + start_ref[idx]
        n_chunks = (count_ref[idx] + chunk - 1) // chunk

        def chunk_body(c, carry2):
            r0 = pl.multiple_of(base + c * chunk, SORT_ALIGN)
            xs = hms_ref[pl.ds(r0, chunk), 0:d]
            gates = hms_ref[pl.ds(r0, chunk), d:d + LANES].astype(F32)
            acts = []
            for j in range(group):
                e = step * group + j
                au = jnp.dot(xs, wgu_ref[j], preferred_element_type=F32)
                a = au[:, :ff]
                u = au[:, ff:]
                g_e = jnp.sum(jnp.where(piece_lane & ((lane & (N_EXPERTS - 1)) == e), gates, 0.0),
                              axis=-1, keepdims=True)
                acts.append((a * jax.nn.sigmoid(a) * u * g_e).astype(BF16))
            y = jnp.dot(jnp.concatenate(acts, axis=-1), wd_ref[...], preferred_element_type=F32)
            o_ref[pl.ds(r0, chunk), :] = (o_ref[pl.ds(r0, chunk), :].astype(F32) + y).astype(o_ref.dtype)
            return carry2

        lax.fori_loop(0, n_chunks, chunk_body, 0)
        return carry

    lax.fori_loop(0, tiles, tile_body, 0)


def _moe(hms, starts, counts, w_gu, w_d, tiles, group, chunk):
    n_rows, dx = hms.shape
    n_exp, d, ff2 = w_gu.shape
    ff = ff2 // 2
    rows = tiles * SORT_SLOTS
    grid_spec = pltpu.PrefetchScalarGridSpec(
        num_scalar_prefetch=2,
        grid=(n_rows // rows, n_exp // group),
        in_specs=[pl.BlockSpec((rows, dx), lambda i, e, s, c: (i, 0)),
                  pl.BlockSpec((group, d, ff2), lambda i, e, s, c: (e, 0, 0)),
                  pl.BlockSpec((group * ff, d), lambda i, e, s, c: (e, 0))],
        out_specs=pl.BlockSpec((rows, d), lambda i, e, s, c: (i, 0)),
    )
    return pl.pallas_call(
        functools.partial(_moe_kernel, group=group, tiles=tiles, chunk=chunk),
        grid_spec=grid_spec,
        out_shape=jax.ShapeDtypeStruct((n_rows, d), BF16),
        compiler_params=pltpu.CompilerParams(dimension_semantics=("parallel", "arbitrary"),
                                             vmem_limit_bytes=VMEM_LIMIT_BYTES),
        name="moe",
    )(starts, counts, hms, w_gu, w_d.reshape(n_exp * ff, d))


def _unsort_kernel(x2_ref, pos_ref, ys_ref, o_ref):
    t = x2_ref.shape[0]
    ts = ys_ref.shape[0]
    unperm = (lax.broadcasted_iota(jnp.int32, (t, ts), 1).astype(F32) == pos_ref[...]).astype(BF16)
    o_ref[...] = x2_ref[...] + jnp.dot(unperm, ys_ref[...], preferred_element_type=F32)


def _unsort(x2, pos, ys, tile):
    n_tok, d = x2.shape
    tok = lambda i: (i, 0)
    return pl.pallas_call(
        _unsort_kernel,
        grid=(n_tok // tile,),
        in_specs=[pl.BlockSpec((tile, d), tok), pl.BlockSpec((tile, 1), tok), pl.BlockSpec((SORT_SLOTS, d), tok)],
        out_specs=pl.BlockSpec((tile, d), tok),
        out_shape=jax.ShapeDtypeStruct((n_tok, d), F32),
        compiler_params=pltpu.CompilerParams(dimension_semantics=("parallel",)),
        name="unsort",
    )(x2, pos, ys)


def _row(v):
    return v.reshape(1, -1).astype(F32)


def _pad_lanes(v, width=LANES):
    return jnp.pad(v, [(0, 0)] * (v.ndim - 1) + [(0, width - v.shape[-1])])


def _swap_halves(v):
    half = v.shape[-1] // 2
    return jnp.concatenate([v[..., half:], v[..., :half]], axis=-1)


def _layout_params(l, attn_norm_g, w_in, mla_q_norm_g, mla_w_uq, mla_kv_norm_g, mla_w_ukv, mla_q_qk_g,
                   mla_k_qk_g, ret_gn_g, w_out, cross_norm_g, mem_norm_g, cross_w_q, cross_w_kv, cross_q_qk_g,
                   cross_k_qk_g, cross_w_o, moe_norm_g, router_w_group, router_b_group, router_w_expert,
                   router_b_expert):
    p = {}
    d = w_in.shape[1]
    wi = w_in[l]
    c_q, c_kv, k_pe, rq, rk, rv, rg = jnp.split(
        wi, np.cumsum([MLA_Q_RANK, MLA_KV_RANK, MLA_ROPE_DIM, RET_WIDTH, RET_WIDTH, RET_WIDTH]).tolist(), axis=1)
    p['w_in'] = jnp.concatenate([c_q, c_kv, rq, rk, rv, rg, _pad_lanes(k_pe), _pad_lanes(_swap_halves(k_pe))],
                                axis=1).astype(BF16)
    wq = mla_w_uq[l].reshape(MLA_Q_RANK, MLA_HEADS, MLA_QK_DIM)
    wq_rope = wq[:, :, MLA_NOPE_DIM:]
    p['w_uq'] = jnp.concatenate([wq[:, :, :MLA_NOPE_DIM], _pad_lanes(wq_rope), _pad_lanes(_swap_halves(wq_rope))],
                                axis=2).reshape(MLA_Q_RANK, MLA_HEADS * Q_HEAD_COLS).astype(BF16)
    wkv = mla_w_ukv[l].reshape(MLA_KV_RANK, MLA_HEADS, MLA_NOPE_DIM + MLA_V_DIM)
    p['w_ukv'] = jnp.concatenate([wkv[:, :, :MLA_NOPE_DIM].reshape(MLA_KV_RANK, -1),
                                  wkv[:, :, MLA_NOPE_DIM:].reshape(MLA_KV_RANK, -1)], axis=1).astype(BF16)
    p['attn_norm_g'] = _row(attn_norm_g[l])
    p['q_norm_g'] = _row(mla_q_norm_g[l])
    p['kv_norm_g'] = _row(mla_kv_norm_g[l])
    for name, g in (('q', mla_q_qk_g[l]), ('k', mla_k_qk_g[l])):
        p[name + '_g_nope'] = _row(g[:MLA_NOPE_DIM])
        p[name + '_g_rope'] = _row(_pad_lanes(g[MLA_NOPE_DIM:]))
        p[name + '_g_rope_sw'] = _row(_pad_lanes(_swap_halves(g[MLA_NOPE_DIM:])))
    hm_ = MLA_ROPE_DIM // 2
    f_m = ROPE_BASE ** (-jnp.arange(hm_, dtype=F32) / hm_)
    p['freq_m'] = _row(_pad_lanes(jnp.concatenate([f_m, f_m])))
    p['sign_m'] = _row(_pad_lanes(jnp.concatenate([-jnp.ones(hm_, F32), jnp.ones(hm_, F32)])))
    hr_ = RET_HEAD_DIM // 2
    f_r = ROPE_BASE ** (-jnp.arange(hr_, dtype=F32) / hr_)
    p['freq_r'] = _row(jnp.concatenate([f_r, f_r]))
    p['sign_r'] = _row(jnp.concatenate([-jnp.ones(hr_, F32), jnp.ones(hr_, F32)]))
    p['ret_gn_g'] = _row(ret_gn_g[l])
    half = MLA_HEADS * MLA_V_DIM
    p['w_out_a'] = w_out[l][:half].astype(BF16)
    p['w_out_r'] = w_out[l][half:].astype(BF16)
    p['cross_norm_g'] = _row(cross_norm_g[l])
    p['mem_norm_g'] = _row(mem_norm_g[l])
    p['cross_w_q'] = cross_w_q[l].astype(BF16)
    p['cross_w_kv'] = cross_w_kv[l].astype(BF16)
    p['cross_q_g'] = _row(cross_q_qk_g[l])
    p['cross_k_g'] = _row(cross_k_qk_g[l])
    p['cross_w_o'] = cross_w_o[l].astype(BF16)
    p['moe_norm_g'] = _row(moe_norm_g[l])
    w_r = _pad_lanes(jnp.concatenate([router_w_expert[l], router_w_group[l]], axis=1)).astype(F32)
    w_r_hi = w_r.astype(BF16)
    w_r_lo = (w_r - w_r_hi.astype(F32)).astype(BF16)
    p['w_router'] = jnp.concatenate([w_r_hi, w_r_hi, w_r_lo], axis=0)
    p['b_router'] = _row(_pad_lanes(jnp.concatenate([router_b_expert[l], router_b_group[l]])))
    return p


PREP_TILE = 512
PREP_CHAINS = 1
FLASH_TQ = 1024
FLASH_TK = 1024
FLASH_HEADS_PER_STEP = 4
MIX_TILE = 512
MIX_CHAINS = 2
SORT_ALIGN = 16
MOE_CHUNK = 144
SORT_SLOTS = -(-(MIX_TILE + N_GROUPS * (SORT_ALIGN - 1) + MOE_CHUNK) // LANES) * LANES
MOE_TILES_PER_BLOCK = 4
MOE_EXPERTS_PER_STEP = 8


def kernel(x, mem, positions, attn_norm_g, w_in, mla_q_norm_g, mla_w_uq, mla_kv_norm_g, mla_w_ukv, mla_q_qk_g, mla_k_qk_g, ret_gn_g, w_out, cross_norm_g, mem_norm_g, cross_w_q, cross_w_kv, cross_q_qk_g, cross_k_qk_g, cross_w_o, moe_norm_g, router_w_group, router_b_group, router_w_expert, router_b_expert, expert_w_gate, expert_w_up, expert_w_down):
    batch, seq, d = x.shape
    depth = w_in.shape[0]
    pos2d = positions.reshape(batch * seq, 1).astype(jnp.int32)
    x2d = x.reshape(batch * seq, d)
    for l in range(depth):
        p = _layout_params(l, attn_norm_g, w_in, mla_q_norm_g, mla_w_uq, mla_kv_norm_g, mla_w_ukv, mla_q_qk_g,
                           mla_k_qk_g, ret_gn_g, w_out, cross_norm_g, mem_norm_g, cross_w_q, cross_w_kv,
                           cross_q_qk_g, cross_k_qk_g, cross_w_o, moe_norm_g, router_w_group, router_b_group,
                           router_w_expert, router_b_expert)
        w_gu = jnp.concatenate([expert_w_gate[l], expert_w_up[l]], axis=-1).astype(BF16)
        w_d = expert_w_down[l].astype(BF16)

        q, k, v, rq, rk, rv, rg = _prep(x2d, pos2d, p, batch, seq, PREP_TILE)
        a_out = _flash(q, k, v, FLASH_TQ, FLASH_TK, FLASH_HEADS_PER_STEP)
        shp = (batch, seq, RET_WIDTH)
        r_out = _retention(rq.reshape(shp), rk.reshape(shp), rv.reshape(shp), rg.reshape(shp), p['ret_gn_g'])
        kc, vc = _memkv(mem, p['mem_norm_g'], p['cross_w_kv'], p['cross_k_g'])
        x2, hms, pos, meta = _mix(x2d, a_out.reshape(batch * seq, -1), r_out.reshape(batch * seq, -1), kc, vc, p,
                                  batch, seq, MIX_TILE)
        starts = meta[:, 0, :N_GROUPS].reshape(-1)
        counts = meta[:, 1, :N_GROUPS].reshape(-1)
        ys = _moe(hms, starts, counts, w_gu, w_d, MOE_TILES_PER_BLOCK, MOE_EXPERTS_PER_STEP, MOE_CHUNK)
        x2d = _unsort(x2, pos, ys, MIX_TILE)
    return x2d.reshape(batch, seq, d)
```

```python
import functools
import math

import numpy as np
import jax
import jax.numpy as jnp
from jax import lax
from jax.experimental import pallas as pl
from jax.experimental.pallas import tpu as pltpu

F32 = jnp.float32
BF16 = jnp.bfloat16

LANES = 128
VMEM_LIMIT_BYTES = 56 * 1024 * 1024

MLA_HEADS = 4
MLA_NOPE_DIM = 128
MLA_ROPE_DIM = 64
MLA_QK_DIM = MLA_NOPE_DIM + MLA_ROPE_DIM
MLA_V_DIM = 128
MLA_Q_RANK = 256
MLA_KV_RANK = 128
MLA_HEAD_PAD = 256
RET_HEADS = 4
RET_HEAD_DIM = 128
RET_WIDTH = RET_HEADS * RET_HEAD_DIM
RET_CHUNK = 256
CROSS_HEADS = 4
N_GROUPS = 4
EXPERTS_PER_GROUP = 8
N_EXPERTS = 32
ROPE_BASE = 10000.0
EPS = 1e-6
NEG_INF = -1e30

_C_Q = 0
_C_KV = _C_Q + MLA_Q_RANK
_RQ = _C_KV + MLA_KV_RANK
_RK = _RQ + RET_WIDTH
_RV = _RK + RET_WIDTH
_RG = _RV + RET_WIDTH
_KPE = _RG + RET_WIDTH
_KPE_SW = _KPE + LANES
IN_COLS_PAD = _KPE_SW + LANES
Q_HEAD_COLS = 3 * LANES


def _rms(x, g):
    return x * lax.rsqrt(jnp.mean(x * x, axis=-1, keepdims=True) + EPS) * g


def _const_spec(shape):
    nd = len(shape)
    return pl.BlockSpec(shape, lambda *_: (0,) * nd, pipeline_mode=pl.Buffered(1))


def _prep_rows(rs, x_ref, pos_ref, g_ref, win_ref, qng_ref, wuq_ref, kvng_ref, wukv_ref,
               qgn_ref, qgr_ref, qgs_ref, kgn_ref, kgr_ref, kgs_ref,
               fm_ref, sm_ref, fr_ref, sr_ref,
               q_out, k_out, v_out, rq_out, rk_out, rv_out, rg_out):
    x = x_ref[rs, :]
    h = _rms(x, g_ref[...]).astype(BF16)
    proj = jnp.dot(h, win_ref[...], preferred_element_type=F32)

    posf = pos_ref[rs, :].astype(F32)
    ang_m = posf * fm_ref[...]
    cos_m = jnp.cos(ang_m)
    sin_m = jnp.sin(ang_m) * sm_ref[...]
    ang_r = posf * fr_ref[...]
    cos_r = jnp.cos(ang_r)
    sin_r = jnp.sin(ang_r) * sr_ref[...]

    c_q = proj[:, _C_Q:_C_Q + MLA_Q_RANK]
    qall = jnp.dot(_rms(c_q, qng_ref[...]).astype(BF16), wuq_ref[...], preferred_element_type=F32)
    q_scale = MLA_QK_DIM ** -0.5 * math.log2(math.e)
    for hd in range(MLA_HEADS):
        base = hd * Q_HEAD_COLS
        nope = qall[:, base:base + LANES]
        rope = qall[:, base + LANES:base + 2 * LANES]
        rope_sw = qall[:, base + 2 * LANES:base + 3 * LANES]
        ss = jnp.sum(nope * nope, axis=-1, keepdims=True) + jnp.sum(rope * rope, axis=-1, keepdims=True)
        r = lax.rsqrt(ss * (1.0 / MLA_QK_DIM) + EPS) * q_scale
        roped = rope * qgr_ref[...] * cos_m + rope_sw * qgs_ref[...] * sin_m
        q_out[0, hd, rs,0:LANES] = (nope * qgn_ref[...] * r).astype(BF16)
        q_out[0, hd, rs,LANES:2 * LANES] = (roped * r).astype(BF16)

    c_kv = proj[:, _C_KV:_C_KV + MLA_KV_RANK]
    kv = jnp.dot(_rms(c_kv, kvng_ref[...]).astype(BF16), wukv_ref[...], preferred_element_type=F32)
    k_pe = proj[:, _KPE:_KPE + LANES]
    k_pe_sw = proj[:, _KPE_SW:_KPE_SW + LANES]
    kr = k_pe * kgr_ref[...] * cos_m + k_pe_sw * kgs_ref[...] * sin_m
    ss_pe = jnp.sum(k_pe * k_pe, axis=-1, keepdims=True)
    for hd in range(MLA_HEADS):
        k_nope = kv[:, hd * LANES:(hd + 1) * LANES]
        ss = jnp.sum(k_nope * k_nope, axis=-1, keepdims=True) + ss_pe
        r = lax.rsqrt(ss * (1.0 / MLA_QK_DIM) + EPS)
        k_out[0, hd, rs,0:LANES] = (k_nope * kgn_ref[...] * r).astype(BF16)
        k_out[0, hd, rs,LANES:2 * LANES] = (kr * r).astype(BF16)
        v_out[0, hd, rs, :] = kv[:, (MLA_HEADS + hd) * LANES:(MLA_HEADS + hd + 1) * LANES].astype(BF16)

    k_scale = RET_HEAD_DIM ** -0.5
    for hd in range(RET_HEADS):
        sl = slice(hd * LANES, (hd + 1) * LANES)
        rq = proj[:, _RQ + hd * LANES:_RQ + (hd + 1) * LANES]
        rk = proj[:, _RK + hd * LANES:_RK + (hd + 1) * LANES]
        rq_out[rs, sl] =(rq * cos_r + pltpu.roll(rq, LANES // 2, axis=1) * sin_r).astype(BF16)
        rk_out[rs, sl] =((rk * cos_r + pltpu.roll(rk, LANES // 2, axis=1) * sin_r) * k_scale).astype(BF16)
    rv_out[rs, :] = proj[:, _RV:_RV + RET_WIDTH].astype(BF16)
    rg_out[rs, :] = proj[:, _RG:_RG + RET_WIDTH].astype(BF16)


def _prep_kernel(x_ref, *refs, chains):
    rows = x_ref.shape[0] // chains
    for c in range(chains):
        _prep_rows(slice(c * rows, (c + 1) * rows), x_ref, *refs)


def _prep(x2d, pos2d, p, batch, seq, tile):
    n_tok, d = x2d.shape
    tiles_per_b = seq // tile
    tok = lambda i: (i, 0)
    head = lambda i: (i // tiles_per_b, 0, i % tiles_per_b, 0)
    consts = [p['attn_norm_g'], p['w_in'], p['q_norm_g'], p['w_uq'], p['kv_norm_g'], p['w_ukv'],
              p['q_g_nope'], p['q_g_rope'], p['q_g_rope_sw'], p['k_g_nope'], p['k_g_rope'], p['k_g_rope_sw'],
              p['freq_m'], p['sign_m'], p['freq_r'], p['sign_r']]
    out_shape = (
        jax.ShapeDtypeStruct((batch, MLA_HEADS, seq, MLA_HEAD_PAD), BF16),
        jax.ShapeDtypeStruct((batch, MLA_HEADS, seq, MLA_HEAD_PAD), BF16),
        jax.ShapeDtypeStruct((batch, MLA_HEADS, seq, MLA_V_DIM), BF16),
        jax.ShapeDtypeStruct((n_tok, RET_WIDTH), BF16),
        jax.ShapeDtypeStruct((n_tok, RET_WIDTH), BF16),
        jax.ShapeDtypeStruct((n_tok, RET_WIDTH), BF16),
        jax.ShapeDtypeStruct((n_tok, RET_WIDTH), BF16),
    )
    return pl.pallas_call(
        functools.partial(_prep_kernel, chains=PREP_CHAINS),
        grid=(n_tok // tile,),
        in_specs=[pl.BlockSpec((tile, d), tok), pl.BlockSpec((tile, 1), tok)]
                 + [_const_spec(c.shape) for c in consts],
        out_specs=(
            pl.BlockSpec((1, MLA_HEADS, tile, MLA_HEAD_PAD), head),
            pl.BlockSpec((1, MLA_HEADS, tile, MLA_HEAD_PAD), head),
            pl.BlockSpec((1, MLA_HEADS, tile, MLA_V_DIM), head),
            pl.BlockSpec((tile, RET_WIDTH), tok),
            pl.BlockSpec((tile, RET_WIDTH), tok),
            pl.BlockSpec((tile, RET_WIDTH), tok),
            pl.BlockSpec((tile, RET_WIDTH), tok),
        ),
        out_shape=out_shape,
        compiler_params=pltpu.CompilerParams(dimension_semantics=("parallel",),
                                             vmem_limit_bytes=VMEM_LIMIT_BYTES),
        name="prep",
    )(x2d, pos2d, *consts)


def _flash_kernel(q_ref, k_ref, v_ref, o_ref, m_sc, l_sc, acc_sc, *, tq, tk, hb):
    qi = pl.program_id(2)
    m_sc[...] = jnp.full(m_sc.shape, NEG_INF, F32)
    l_sc[...] = jnp.zeros(l_sc.shape, F32)
    acc_sc[...] = jnp.zeros(acc_sc.shape, F32)

    def step(off, width, rows, mask):
        reps = width // LANES
        for hd in range(hb):
            k = k_ref[0, hd, pl.ds(off, width), :]
            v = v_ref[0, hd, pl.ds(off, width), :]
            s = lax.dot_general(q_ref[0, hd, rows, :], k, (((1,), (1,)), ((), ())), preferred_element_type=F32)
            if mask is not None:
                s = jnp.where(mask, s, NEG_INF)
            m_prev = m_sc[hd, rows, :]
            m_new = jnp.maximum(m_prev, jnp.max(s, axis=-1, keepdims=True))
            alpha = jnp.exp2(m_prev - m_new)
            p = jnp.exp2(s - jnp.concatenate([m_new] * reps, axis=1))
            p_cols = p[:, 0:LANES]
            for c in range(1, reps):
                p_cols = p_cols + p[:, c * LANES:(c + 1) * LANES]
            l_sc[hd, rows, :] = alpha * l_sc[hd, rows, :] + p_cols
            acc_sc[hd, rows, :] = alpha * acc_sc[hd, rows, :] + jnp.dot(p.astype(BF16), v,
                                                                        preferred_element_type=F32)
            m_sc[hd, rows, :] = m_new

    def body(j, c):
        step(pl.multiple_of(j * tk, tk), tk, slice(0, tq), None)
        return c

    lax.fori_loop(0, qi, body, 0)
    half = tk // 2
    base = pl.multiple_of(qi * tk, tk)
    def causal(n_rows):
        return (lax.broadcasted_iota(jnp.int32, (n_rows, half), 1)
                <= lax.broadcasted_iota(jnp.int32, (n_rows, half), 0))

    step(base, half, slice(0, tq), causal(tq))
    step(base + half, half, slice(half, tq), causal(tq - half))
    for hd in range(hb):
        l = jnp.sum(l_sc[hd], axis=-1, keepdims=True)
        o_ref[0, :, hd * LANES:(hd + 1) * LANES] = (acc_sc[hd] / l).astype(o_ref.dtype)


def _flash(q, k, v, tq, tk, hb):
    batch, heads, seq, dqk = q.shape
    dv = v.shape[-1]
    assert tq == tk and seq % tq == 0 and heads % hb == 0
    return pl.pallas_call(
        functools.partial(_flash_kernel, tq=tq, tk=tk, hb=hb),
        grid=(batch, heads // hb, seq // tq),
        in_specs=[
            pl.BlockSpec((1, hb, tq, dqk), lambda b, h, i: (b, h, i, 0)),
            pl.BlockSpec((1, hb, seq, dqk), lambda b, h, i: (b, h, 0, 0)),
            pl.BlockSpec((1, hb, seq, dv), lambda b, h, i: (b, h, 0, 0)),
        ],
        out_specs=pl.BlockSpec((1, tq, hb * dv), lambda b, h, i: (b, i, h)),
        out_shape=jax.ShapeDtypeStruct((batch, seq, heads * dv), BF16),
        scratch_shapes=[pltpu.VMEM((hb, tq, LANES), F32), pltpu.VMEM((hb, tq, LANES), F32),
                        pltpu.VMEM((hb, tq, dv), F32)],
        compiler_params=pltpu.CompilerParams(dimension_semantics=("parallel", "parallel", "arbitrary"),
                                             vmem_limit_bytes=VMEM_LIMIT_BYTES),
        name="flash",
    )(q, k, v)


def _ret_kernel(q_ref, k_ref, v_ref, g_ref, gn_ref, o_ref, state_sc, intra_sc):
    c = pl.program_id(1)
    n = RET_CHUNK
    log_decay = [math.log(1.0 - 2.0 ** (-5.0 - hd)) for hd in range(RET_HEADS)]

    @pl.when(c == 0)
    def _():
        state_sc[...] = jnp.zeros(state_sc.shape, F32)
        diff = (lax.broadcasted_iota(jnp.int32, (n, n), 0) - lax.broadcasted_iota(jnp.int32, (n, n), 1)).astype(F32)
        for hd in range(RET_HEADS):
            intra_sc[hd] = jnp.where(diff >= 0, jnp.exp(jnp.maximum(diff, 0.0) * log_decay[hd]), 0.0)

    idx = lax.broadcasted_iota(jnp.int32, (n, 1), 0).astype(F32)
    for hd in range(RET_HEADS):
        lg = log_decay[hd]
        intra = intra_sc[hd]
        q_decay = jnp.exp((idx + 1.0) * lg)
        k_decay = jnp.exp((n - 1.0 - idx) * lg)
        chunk_decay = math.exp(n * lg)
        sl = slice(hd * LANES, (hd + 1) * LANES)
        qc = q_ref[0, :, sl]
        kc = k_ref[0, :, sl]
        vc = v_ref[0, :, sl]
        state = state_sc[hd]
        scores = lax.dot_general(qc, kc, (((1,), (1,)), ((), ())), preferred_element_type=F32) * intra
        o = (jnp.dot(scores.astype(BF16), vc, preferred_element_type=F32)
             + jnp.dot((qc.astype(F32) * q_decay).astype(BF16), state.astype(BF16), preferred_element_type=F32))
        kd_t = (kc.astype(F32) * k_decay).T.astype(BF16)
        state_sc[hd] = state * chunk_decay + jnp.dot(kd_t, vc, preferred_element_type=F32)
        mu = jnp.mean(o, axis=-1, keepdims=True)
        cen = o - mu
        var = jnp.mean(cen * cen, axis=-1, keepdims=True)
        gate = g_ref[0, :, sl].astype(F32)
        y = cen * lax.rsqrt(var + EPS) * gn_ref[:, sl] * (gate * jax.nn.sigmoid(gate))
        o_ref[0, :, sl] = y.astype(o_ref.dtype)


def _retention(rq, rk, rv, rg, gn_g):
    batch, seq, width = rq.shape
    blk = pl.BlockSpec((1, RET_CHUNK, width), lambda b, c: (b, c, 0))
    return pl.pallas_call(
        _ret_kernel,
        grid=(batch, seq // RET_CHUNK),
        in_specs=[blk, blk, blk, blk, _const_spec(gn_g.shape)],
        out_specs=blk,
        out_shape=jax.ShapeDtypeStruct((batch, seq, width), BF16),
        scratch_shapes=[pltpu.VMEM((RET_HEADS, RET_HEAD_DIM, RET_HEAD_DIM), F32),
                        pltpu.VMEM((RET_HEADS, RET_CHUNK, RET_CHUNK), F32)],
        compiler_params=pltpu.CompilerParams(dimension_semantics=("parallel", "arbitrary")),
        name="retention",
    )(rq, rk, rv, rg, gn_g)


def _memkv_kernel(mem_ref, g_ref, wkv_ref, kg_ref, k_out, v_out):
    d = mem_ref.shape[-1]
    hdim = d // CROSS_HEADS
    m = _rms(mem_ref[0], g_ref[...]).astype(BF16)
    kv = jnp.dot(m, wkv_ref[...], preferred_element_type=F32)
    for hd in range(CROSS_HEADS):
        sl = slice(hd * hdim, (hd + 1) * hdim)
        k_out[0, :, sl] = _rms(kv[:, sl], kg_ref[...]).astype(BF16)
    v_out[0] = kv[:, d:].astype(BF16)


def _memkv(mem, mem_norm_g, w_kv, k_g):
    batch, m_tok, d = mem.shape
    blk = pl.BlockSpec((1, m_tok, d), lambda b: (b, 0, 0))
    return pl.pallas_call(
        _memkv_kernel,
        grid=(batch,),
        in_specs=[blk, _const_spec(mem_norm_g.shape), _const_spec(w_kv.shape), _const_spec(k_g.shape)],
        out_specs=(blk, blk),
        out_shape=(jax.ShapeDtypeStruct((batch, m_tok, d), BF16), jax.ShapeDtypeStruct((batch, m_tok, d), BF16)),
        compiler_params=pltpu.CompilerParams(dimension_semantics=("parallel",),
                                             vmem_limit_bytes=VMEM_LIMIT_BYTES),
        name="memkv",
    )(mem, mem_norm_g, w_kv, k_g)


def _mix_rows(x, a, r, kc_ref, vc_ref, woa_ref, wor_ref, cng_ref, wq_ref, qg_ref, wo_ref, mng_ref, wr_ref,
              br_ref):
    d = x.shape[-1]
    hdim = d // CROSS_HEADS
    x1 = (x
          + jnp.dot(a, woa_ref[...], preferred_element_type=F32)
          + jnp.dot(r, wor_ref[...], preferred_element_type=F32))
    hc = _rms(x1, cng_ref[...]).astype(BF16)
    q = jnp.dot(hc, wq_ref[...], preferred_element_type=F32)
    scale = hdim ** -0.5
    heads = []
    for hd in range(CROSS_HEADS):
        sl = slice(hd * hdim, (hd + 1) * hdim)
        qh = (_rms(q[:, sl], qg_ref[...]) * scale).astype(BF16)
        s = lax.dot_general(qh, kc_ref[0, :, sl], (((1,), (1,)), ((), ())), preferred_element_type=F32)
        s = s - jnp.max(s, axis=-1, keepdims=True)
        e = jnp.exp(s)
        pr = e / jnp.sum(e, axis=-1, keepdims=True)
        heads.append(jnp.dot(pr.astype(BF16), vc_ref[0, :, sl], preferred_element_type=F32).astype(BF16))
    o = jnp.concatenate(heads, axis=-1)
    x2 = x1 + jnp.dot(o, wo_ref[...], preferred_element_type=F32)
    hm = _rms(x2, mng_ref[...])

    hm_hi = hm.astype(BF16)
    hm_lo = (hm - hm_hi.astype(F32)).astype(BF16)
    logits = jnp.dot(jnp.concatenate([hm_hi, hm_lo, hm_hi], axis=-1), wr_ref[...], preferred_element_type=F32)
    biased = logits + br_ref[...]
    lane = lax.broadcasted_iota(jnp.int32, logits.shape, 1)
    lane_f = lane.astype(F32)
    big = float(4 * LANES)
    is_g = (lane >= N_EXPERTS) & (lane < N_EXPERTS + N_GROUPS)
    gl = jnp.where(is_g, logits, NEG_INF)
    gexp = jnp.where(is_g, jnp.exp(gl - jnp.max(gl, axis=-1, keepdims=True)), 0.0)
    gb = jnp.where(is_g, biased, NEG_INF)
    g_lane = jnp.min(jnp.where(gb == jnp.max(gb, axis=-1, keepdims=True), lane_f, big), axis=-1, keepdims=True)
    g_w = (jnp.sum(jnp.where(lane_f == g_lane, gexp, 0.0), axis=-1, keepdims=True)
           / jnp.sum(gexp, axis=-1, keepdims=True))
    g_idx = g_lane.astype(jnp.int32) - N_EXPERTS
    in_grp = (lane < N_EXPERTS) & ((lane // EXPERTS_PER_GROUP) == g_idx)
    el = jnp.where(in_grp, logits, NEG_INF)
    eexp = jnp.where(in_grp, jnp.exp(el - jnp.max(el, axis=-1, keepdims=True)), 0.0)
    eb = jnp.where(in_grp, biased, NEG_INF)
    i1 = jnp.min(jnp.where(eb == jnp.max(eb, axis=-1, keepdims=True), lane_f, big), axis=-1, keepdims=True)
    eb2 = jnp.where(lane_f == i1, NEG_INF, eb)
    i2 = jnp.min(jnp.where(eb2 == jnp.max(eb2, axis=-1, keepdims=True), lane_f, big), axis=-1, keepdims=True)
    psel = jnp.where((lane_f == i1) | (lane_f == i2), eexp, 0.0)
    gate = psel / jnp.sum(psel, axis=-1, keepdims=True) * g_w
    grp_onehot = (lane == g_idx).astype(F32)
    return x2, hm.astype(BF16), gate, grp_onehot


def _group_sort(hm, gate, grp, hms_out, pos_out, meta_out, c):
    t = hm.shape[0]
    ts = SORT_SLOTS
    earlier = (lax.broadcasted_iota(jnp.int32, (t, t), 1) < lax.broadcasted_iota(jnp.int32, (t, t), 0))
    cnt = jnp.dot(earlier.astype(BF16), grp.astype(BF16), preferred_element_type=F32)
    n = jnp.sum(grp, axis=0, keepdims=True)
    n_al = jnp.floor((n + (SORT_ALIGN - 1.0)) * (1.0 / SORT_ALIGN)) * SORT_ALIGN
    lane1 = lax.broadcasted_iota(jnp.int32, (1, LANES), 1)
    start = jnp.zeros((1, LANES), F32)
    run = jnp.zeros((1, 1), F32)
    for g in range(1, N_GROUPS):
        run = run + jnp.sum(jnp.where(lane1 == g - 1, n_al, 0.0), axis=-1, keepdims=True)
        start = jnp.where(lane1 == g, run, start)
    pos = jnp.sum(grp * (cnt + start), axis=-1, keepdims=True)
    pos_out[c * t:(c + 1) * t, :] = pos
    row8 = lax.broadcasted_iota(jnp.int32, (8, LANES), 0)
    meta_out[c] = jnp.where(row8 == 0, start, jnp.where(row8 == 1, n, 0.0)).astype(jnp.int32)
    pos_row = jnp.broadcast_to(pos, (t, LANES)).T[0:1, :]
    perm = (lax.broadcasted_iota(jnp.int32, (ts, t), 0).astype(F32) == pos_row).astype(BF16)
    g_hi = gate.astype(BF16).astype(F32)
    rem = gate - g_hi
    g_mid = rem.astype(BF16).astype(F32)
    g_lo = rem - g_mid
    gate_pieces = (g_hi + pltpu.roll(g_mid, N_EXPERTS, axis=1) + pltpu.roll(g_lo, 2 * N_EXPERTS, axis=1)).astype(BF16)
    hms_out[c * ts:(c + 1) * ts, :] = jnp.dot(perm, jnp.concatenate([hm, gate_pieces], axis=-1),
                                              preferred_element_type=F32).astype(BF16)


def _mix_kernel(x_ref, a_ref, r_ref, *rest, chains):
    *w_refs, x2_out, hms_out, pos_out, meta_out = rest
    rows = x_ref.shape[0] // chains
    for c in range(chains):
        sl = slice(c * rows, (c + 1) * rows)
        x2, hm, gate, grp = _mix_rows(x_ref[sl, :], a_ref[sl, :], r_ref[sl, :], *w_refs)
        x2_out[sl, :] = x2
        _group_sort(hm, gate, grp, hms_out, pos_out, meta_out, c)


def _mix(x2d, a2d, r2d, kc, vc, p, batch, seq, tile):
    n_tok, d = x2d.shape
    half = a2d.shape[-1]
    tiles_per_b = seq // tile
    n_tiles = n_tok // tile
    tok = lambda i: (i, 0)
    mem_blk = pl.BlockSpec((1,) + kc.shape[1:], lambda i: (i // tiles_per_b, 0, 0))
    consts = [p['w_out_a'], p['w_out_r'], p['cross_norm_g'], p['cross_w_q'], p['cross_q_g'], p['cross_w_o'],
              p['moe_norm_g'], p['w_router'], p['b_router']]
    return pl.pallas_call(
        functools.partial(_mix_kernel, chains=MIX_CHAINS),
        grid=(n_tiles,),
        in_specs=[pl.BlockSpec((tile, d), tok), pl.BlockSpec((tile, half), tok), pl.BlockSpec((tile, half), tok),
                  mem_blk, mem_blk] + [_const_spec(c.shape) for c in consts],
        out_specs=(pl.BlockSpec((tile, d), tok), pl.BlockSpec((MIX_CHAINS * SORT_SLOTS, d + LANES), tok),
                   pl.BlockSpec((tile, 1), tok), pl.BlockSpec((MIX_CHAINS, 8, LANES), lambda i: (i, 0, 0))),
        out_shape=(jax.ShapeDtypeStruct((n_tok, d), F32),
                   jax.ShapeDtypeStruct((n_tiles * MIX_CHAINS * SORT_SLOTS, d + LANES), BF16),
                   jax.ShapeDtypeStruct((n_tok, 1), F32),
                   jax.ShapeDtypeStruct((n_tiles * MIX_CHAINS, 8, LANES), jnp.int32)),
        compiler_params=pltpu.CompilerParams(dimension_semantics=("parallel",),
                                             vmem_limit_bytes=VMEM_LIMIT_BYTES),
        name="mix",
    )(x2d, a2d, r2d, kc, vc, *consts)


def _moe_kernel(start_ref, count_ref, hms_ref, wgu_ref, wd_ref, o_ref, *, group, tiles, chunk):
    blk = pl.program_id(0)
    step = pl.program_id(1)
    d = wd_ref.shape[1]
    ff = wgu_ref.shape[2] // 2
    grp = step // (EXPERTS_PER_GROUP // group)

    @pl.when(step == 0)
    def _():
        o_ref[...] = jnp.zeros(o_ref.shape, o_ref.dtype)

    lane = lax.broadcasted_iota(jnp.int32, (chunk, LANES), 1)
    piece_lane = lane < 3 * N_EXPERTS

    def tile_body(tl, carry):
        idx = (blk * tiles + tl) * N_GROUPS + grp
        base = tl * SORT_SLOTS + start_ref[idx]
        n_chunks = (count_ref[idx] + chunk - 1) // chunk

        def chunk_body(c, carry2):
            r0 = pl.multiple_of(base + c * chunk, SORT_ALIGN)
            xs = hms_ref[pl.ds(r0, chunk), 0:d]
            gates = hms_ref[pl.ds(r0, chunk), d:d + LANES].astype(F32)
            acts = []
            for j in range(group):
                e = step * group + j
                au = jnp.dot(xs, wgu_ref[j], preferred_element_type=F32)
                a = au[:, :ff]
                u = au[:, ff:]
                g_e = jnp.sum(jnp.where(piece_lane & ((lane & (N_EXPERTS - 1)) == e), gates, 0.0),
                              axis=-1, keepdims=True)
                acts.append((a * jax.nn.sigmoid(a) * u * g_e).astype(BF16))
            y = jnp.dot(jnp.concatenate(acts, axis=-1), wd_ref[...], preferred_element_type=F32)
            o_ref[pl.ds(r0, chunk), :] = (o_ref[pl.ds(r0, chunk), :].astype(F32) + y).astype(o_ref.dtype)
            return carry2

        lax.fori_loop(0, n_chunks, chunk_body, 0)
        return carry

    lax.fori_loop(0, tiles, tile_body, 0)


def _moe(hms, starts, counts, w_gu, w_d, tiles, group, chunk):
    n_rows, dx = hms.shape
    n_exp, d, ff2 = w_gu.shape
    ff = ff2 // 2
    rows = tiles * SORT_SLOTS
    grid_spec = pltpu.PrefetchScalarGridSpec(
        num_scalar_prefetch=2,
        grid=(n_rows // rows, n_exp // group),
        in_specs=[pl.BlockSpec((rows, dx), lambda i, e, s, c: (i, 0)),
                  pl.BlockSpec((group, d, ff2), lambda i, e, s, c: (e, 0, 0)),
                  pl.BlockSpec((group * ff, d), lambda i, e, s, c: (e, 0))],
        out_specs=pl.BlockSpec((rows, d), lambda i, e, s, c: (i, 0)),
    )
    return pl.pallas_call(
        functools.partial(_moe_kernel, group=group, tiles=tiles, chunk=chunk),
        grid_spec=grid_spec,
        out_shape=jax.ShapeDtypeStruct((n_rows, d), BF16),
        compiler_params=pltpu.CompilerParams(dimension_semantics=("parallel", "arbitrary"),
                                             vmem_limit_bytes=VMEM_LIMIT_BYTES),
        name="moe",
    )(starts, counts, hms, w_gu, w_d.reshape(n_exp * ff, d))


def _unsort_kernel(x2_ref, pos_ref, ys_ref, o_ref):
    t = x2_ref.shape[0]
    ts = ys_ref.shape[0]
    unperm = (lax.broadcasted_iota(jnp.int32, (t, ts), 1).astype(F32) == pos_ref[...]).astype(BF16)
    o_ref[...] = x2_ref[...] + jnp.dot(unperm, ys_ref[...], preferred_element_type=F32)


def _unsort(x2, pos, ys, tile):
    n_tok, d = x2.shape
    tok = lambda i: (i, 0)
    return pl.pallas_call(
        _unsort_kernel,
        grid=(n_tok // tile,),
        in_specs=[pl.BlockSpec((tile, d), tok), pl.BlockSpec((tile, 1), tok), pl.BlockSpec((SORT_SLOTS, d), tok)],
        out_specs=pl.BlockSpec((tile, d), tok),
        out_shape=jax.ShapeDtypeStruct((n_tok, d), F32),
        compiler_params=pltpu.CompilerParams(dimension_semantics=("parallel",)),
        name="unsort",
    )(x2, pos, ys)


def _row(v):
    return v.reshape(1, -1).astype(F32)


def _pad_lanes(v, width=LANES):
    return jnp.pad(v, [(0, 0)] * (v.ndim - 1) + [(0, width - v.shape[-1])])


def _swap_halves(v):
    half = v.shape[-1] // 2
    return jnp.concatenate([v[..., half:], v[..., :half]], axis=-1)


def _layout_params(l, attn_norm_g, w_in, mla_q_norm_g, mla_w_uq, mla_kv_norm_g, mla_w_ukv, mla_q_qk_g,
                   mla_k_qk_g, ret_gn_g, w_out, cross_norm_g, mem_norm_g, cross_w_q, cross_w_kv, cross_q_qk_g,
                   cross_k_qk_g, cross_w_o, moe_norm_g, router_w_group, router_b_group, router_w_expert,
                   router_b_expert):
    p = {}
    d = w_in.shape[1]
    wi = w_in[l]
    c_q, c_kv, k_pe, rq, rk, rv, rg = jnp.split(
        wi, np.cumsum([MLA_Q_RANK, MLA_KV_RANK, MLA_ROPE_DIM, RET_WIDTH, RET_WIDTH, RET_WIDTH]).tolist(), axis=1)
    p['w_in'] = jnp.concatenate([c_q, c_kv, rq, rk, rv, rg, _pad_lanes(k_pe), _pad_lanes(_swap_halves(k_pe))],
                                axis=1).astype(BF16)
    wq = mla_w_uq[l].reshape(MLA_Q_RANK, MLA_HEADS, MLA_QK_DIM)
    wq_rope = wq[:, :, MLA_NOPE_DIM:]
    p['w_uq'] = jnp.concatenate([wq[:, :, :MLA_NOPE_DIM], _pad_lanes(wq_rope), _pad_lanes(_swap_halves(wq_rope))],
                                axis=2).reshape(MLA_Q_RANK, MLA_HEADS * Q_HEAD_COLS).astype(BF16)
    wkv = mla_w_ukv[l].reshape(MLA_KV_RANK, MLA_HEADS, MLA_NOPE_DIM + MLA_V_DIM)
    p['w_ukv'] = jnp.concatenate([wkv[:, :, :MLA_NOPE_DIM].reshape(MLA_KV_RANK, -1),
                                  wkv[:, :, MLA_NOPE_DIM:].reshape(MLA_KV_RANK, -1)], axis=1).astype(BF16)
    p['attn_norm_g'] = _row(attn_norm_g[l])
    p['q_norm_g'] = _row(mla_q_norm_g[l])
    p['kv_norm_g'] = _row(mla_kv_norm_g[l])
    for name, g in (('q', mla_q_qk_g[l]), ('k', mla_k_qk_g[l])):
        p[name + '_g_nope'] = _row(g[:MLA_NOPE_DIM])
        p[name + '_g_rope'] = _row(_pad_lanes(g[MLA_NOPE_DIM:]))
        p[name + '_g_rope_sw'] = _row(_pad_lanes(_swap_halves(g[MLA_NOPE_DIM:])))
    hm_ = MLA_ROPE_DIM // 2
    f_m = ROPE_BASE ** (-jnp.arange(hm_, dtype=F32) / hm_)
    p['freq_m'] = _row(_pad_lanes(jnp.concatenate([f_m, f_m])))
    p['sign_m'] = _row(_pad_lanes(jnp.concatenate([-jnp.ones(hm_, F32), jnp.ones(hm_, F32)])))
    hr_ = RET_HEAD_DIM // 2
    f_r = ROPE_BASE ** (-jnp.arange(hr_, dtype=F32) / hr_)
    p['freq_r'] = _row(jnp.concatenate([f_r, f_r]))
    p['sign_r'] = _row(jnp.concatenate([-jnp.ones(hr_, F32), jnp.ones(hr_, F32)]))
    p['ret_gn_g'] = _row(ret_gn_g[l])
    half = MLA_HEADS * MLA_V_DIM
    p['w_out_a'] = w_out[l][:half].astype(BF16)
    p['w_out_r'] = w_out[l][half:].astype(BF16)
    p['cross_norm_g'] = _row(cross_norm_g[l])
    p['mem_norm_g'] = _row(mem_norm_g[l])
    p['cross_w_q'] = cross_w_q[l].astype(BF16)
    p['cross_w_kv'] = cross_w_kv[l].astype(BF16)
    p['cross_q_g'] = _row(cross_q_qk_g[l])
    p['cross_k_g'] = _row(cross_k_qk_g[l])
    p['cross_w_o'] = cross_w_o[l].astype(BF16)
    p['moe_norm_g'] = _row(moe_norm_g[l])
    w_r = _pad_lanes(jnp.concatenate([router_w_expert[l], router_w_group[l]], axis=1)).astype(F32)
    w_r_hi = w_r.astype(BF16)
    w_r_lo = (w_r - w_r_hi.astype(F32)).astype(BF16)
    p['w_router'] = jnp.concatenate([w_r_hi, w_r_hi, w_r_lo], axis=0)
    p['b_router'] = _row(_pad_lanes(jnp.concatenate([router_b_expert[l], router_b_group[l]])))
    return p


PREP_TILE = 1024
PREP_CHAINS = 1
FLASH_TQ = 1024
FLASH_TK = 1024
FLASH_HEADS_PER_STEP = 4
SORT_TILE = 512
MIX_CHAINS = 2
MIX_TILE = SORT_TILE * MIX_CHAINS
SORT_ALIGN = 16
MOE_CHUNK = 144
SORT_SLOTS = -(-(SORT_TILE + N_GROUPS * (SORT_ALIGN - 1) + MOE_CHUNK) // LANES) * LANES
MOE_TILES_PER_BLOCK = 4
MOE_EXPERTS_PER_STEP = 8


def kernel(x, mem, positions, attn_norm_g, w_in, mla_q_norm_g, mla_w_uq, mla_kv_norm_g, mla_w_ukv, mla_q_qk_g, mla_k_qk_g, ret_gn_g, w_out, cross_norm_g, mem_norm_g, cross_w_q, cross_w_kv, cross_q_qk_g, cross_k_qk_g, cross_w_o, moe_norm_g, router_w_group, router_b_group, router_w_expert, router_b_expert, expert_w_gate, expert_w_up, expert_w_down):
    batch, seq, d = x.shape
    depth = w_in.shape[0]
    pos2d = positions.reshape(batch * seq, 1).astype(jnp.int32)
    x2d = x.reshape(batch * seq, d)
    for l in range(depth):
        p = _layout_params(l, attn_norm_g, w_in, mla_q_norm_g, mla_w_uq, mla_kv_norm_g, mla_w_ukv, mla_q_qk_g,
                           mla_k_qk_g, ret_gn_g, w_out, cross_norm_g, mem_norm_g, cross_w_q, cross_w_kv,
                           cross_q_qk_g, cross_k_qk_g, cross_w_o, moe_norm_g, router_w_group, router_b_group,
                           router_w_expert, router_b_expert)
        w_gu = jnp.concatenate([expert_w_gate[l], expert_w_up[l]], axis=-1).astype(BF16)
        w_d = expert_w_down[l].astype(BF16)

        q, k, v, rq, rk, rv, rg = _prep(x2d, pos2d, p, batch, seq, PREP_TILE)
        a_out = _flash(q, k, v, FLASH_TQ, FLASH_TK, FLASH_HEADS_PER_STEP)
        shp = (batch, seq, RET_WIDTH)
        r_out = _retention(rq.reshape(shp), rk.reshape(shp), rv.reshape(shp), rg.reshape(shp), p['ret_gn_g'])
        kc, vc = _memkv(mem, p['mem_norm_g'], p['cross_w_kv'], p['cross_k_g'])
        x2, hms, pos, meta = _mix(x2d, a_out.reshape(batch * seq, -1), r_out.reshape(batch * seq, -1), kc, vc, p,
                                  batch, seq, MIX_TILE)
        starts = meta[:, 0, :N_GROUPS].reshape(-1)
        counts = meta[:, 1, :N_GROUPS].reshape(-1)
        ys = _moe(hms, starts, counts, w_gu, w_d, MOE_TILES_PER_BLOCK, MOE_EXPERTS_PER_STEP, MOE_CHUNK)
        x2d = _unsort(x2, pos, ys, SORT_TILE)
    return x2d.reshape(batch, seq, d)
```

```python
import functools
import math

import numpy as np
import jax
import jax.numpy as jnp
from jax import lax
from jax.experimental import pallas as pl
from jax.experimental.pallas import tpu as pltpu

F32 = jnp.float32
BF16 = jnp.bfloat16

LANES = 128
VMEM_LIMIT_BYTES = 56 * 1024 * 1024

MLA_HEADS = 4
MLA_NOPE_DIM = 128
MLA_ROPE_DIM = 64
MLA_QK_DIM = MLA_NOPE_DIM + MLA_ROPE_DIM
MLA_V_DIM = 128
MLA_Q_RANK = 256
MLA_KV_RANK = 128
MLA_HEAD_PAD = 256
RET_HEADS = 4
RET_HEAD_DIM = 128
RET_WIDTH = RET_HEADS * RET_HEAD_DIM
RET_CHUNK = 256
CROSS_HEADS = 4
N_GROUPS = 4
EXPERTS_PER_GROUP = 8
N_EXPERTS = 32
ROPE_BASE = 10000.0
EPS = 1e-6
NEG_INF = -1e30

_C_Q = 0
_C_KV = _C_Q + MLA_Q_RANK
_RQ = _C_KV + MLA_KV_RANK
_RK = _RQ + RET_WIDTH
_RV = _RK + RET_WIDTH
_RG = _RV + RET_WIDTH
_KPE = _RG + RET_WIDTH
_KPE_SW = _KPE + LANES
IN_COLS_PAD = _KPE_SW + LANES
Q_HEAD_COLS = 3 * LANES


def _rms(x, g):
    return x * lax.rsqrt(jnp.mean(x * x, axis=-1, keepdims=True) + EPS) * g


def _const_spec(shape):
    nd = len(shape)
    return pl.BlockSpec(shape, lambda *_: (0,) * nd, pipeline_mode=pl.Buffered(1))


def _prep_rows(rs, x_ref, pos_ref, g_ref, win_ref, qng_ref, wuq_ref, kvng_ref, wukv_ref,
               qgn_ref, qgr_ref, qgs_ref, kgn_ref, kgr_ref, kgs_ref,
               fm_ref, sm_ref, fr_ref, sr_ref,
               q_out, k_out, v_out, rq_out, rk_out, rv_out, rg_out):
    x = x_ref[rs, :]
    h = _rms(x, g_ref[...]).astype(BF16)
    proj = jnp.dot(h, win_ref[...], preferred_element_type=F32)

    posf = pos_ref[rs, :].astype(F32)
    ang_m = posf * fm_ref[...]
    cos_m = jnp.cos(ang_m)
    sin_m = jnp.sin(ang_m) * sm_ref[...]
    ang_r = posf * fr_ref[...]
    cos_r = jnp.cos(ang_r)
    sin_r = jnp.sin(ang_r) * sr_ref[...]

    c_q = proj[:, _C_Q:_C_Q + MLA_Q_RANK]
    qall = jnp.dot(_rms(c_q, qng_ref[...]).astype(BF16), wuq_ref[...], preferred_element_type=F32)
    q_scale = MLA_QK_DIM ** -0.5 * math.log2(math.e)
    for hd in range(MLA_HEADS):
        base = hd * Q_HEAD_COLS
        nope = qall[:, base:base + LANES]
        rope = qall[:, base + LANES:base + 2 * LANES]
        rope_sw = qall[:, base + 2 * LANES:base + 3 * LANES]
        ss = jnp.sum(nope * nope, axis=-1, keepdims=True) + jnp.sum(rope * rope, axis=-1, keepdims=True)
        r = lax.rsqrt(ss * (1.0 / MLA_QK_DIM) + EPS) * q_scale
        roped = rope * qgr_ref[...] * cos_m + rope_sw * qgs_ref[...] * sin_m
        q_out[0, hd, rs,0:LANES] = (nope * qgn_ref[...] * r).astype(BF16)
        q_out[0, hd, rs,LANES:2 * LANES] = (roped * r).astype(BF16)

    c_kv = proj[:, _C_KV:_C_KV + MLA_KV_RANK]
    kv = jnp.dot(_rms(c_kv, kvng_ref[...]).astype(BF16), wukv_ref[...], preferred_element_type=F32)
    k_pe = proj[:, _KPE:_KPE + LANES]
    k_pe_sw = proj[:, _KPE_SW:_KPE_SW + LANES]
    kr = k_pe * kgr_ref[...] * cos_m + k_pe_sw * kgs_ref[...] * sin_m
    ss_pe = jnp.sum(k_pe * k_pe, axis=-1, keepdims=True)
    for hd in range(MLA_HEADS):
        k_nope = kv[:, hd * LANES:(hd + 1) * LANES]
        ss = jnp.sum(k_nope * k_nope, axis=-1, keepdims=True) + ss_pe
        r = lax.rsqrt(ss * (1.0 / MLA_QK_DIM) + EPS)
        k_out[0, hd, rs,0:LANES] = (k_nope * kgn_ref[...] * r).astype(BF16)
        k_out[0, hd, rs,LANES:2 * LANES] = (kr * r).astype(BF16)
        v_out[0, hd, rs, :] = kv[:, (MLA_HEADS + hd) * LANES:(MLA_HEADS + hd + 1) * LANES].astype(BF16)

    k_scale = RET_HEAD_DIM ** -0.5
    for hd in range(RET_HEADS):
        sl = slice(hd * LANES, (hd + 1) * LANES)
        rq = proj[:, _RQ + hd * LANES:_RQ + (hd + 1) * LANES]
        rk = proj[:, _RK + hd * LANES:_RK + (hd + 1) * LANES]
        rq_out[rs, sl] =(rq * cos_r + pltpu.roll(rq, LANES // 2, axis=1) * sin_r).astype(BF16)
        rk_out[rs, sl] =((rk * cos_r + pltpu.roll(rk, LANES // 2, axis=1) * sin_r) * k_scale).astype(BF16)
    rv_out[rs, :] = proj[:, _RV:_RV + RET_WIDTH].astype(BF16)
    rg_out[rs, :] = proj[:, _RG:_RG + RET_WIDTH].astype(BF16)


def _prep_kernel(x_ref, *refs, chains):
    rows = x_ref.shape[0] // chains
    for c in range(chains):
        _prep_rows(slice(c * rows, (c + 1) * rows), x_ref, *refs)


def _prep(x2d, pos2d, p, batch, seq, tile):
    n_tok, d = x2d.shape
    tiles_per_b = seq // tile
    tok = lambda i: (i, 0)
    head = lambda i: (i // tiles_per_b, 0, i % tiles_per_b, 0)
    consts = [p['attn_norm_g'], p['w_in'], p['q_norm_g'], p['w_uq'], p['kv_norm_g'], p['w_ukv'],
              p['q_g_nope'], p['q_g_rope'], p['q_g_rope_sw'], p['k_g_nope'], p['k_g_rope'], p['k_g_rope_sw'],
              p['freq_m'], p['sign_m'], p['freq_r'], p['sign_r']]
    out_shape = (
        jax.ShapeDtypeStruct((batch, MLA_HEADS, seq, MLA_HEAD_PAD), BF16),
        jax.ShapeDtypeStruct((batch, MLA_HEADS, seq, MLA_HEAD_PAD), BF16),
        jax.ShapeDtypeStruct((batch, MLA_HEADS, seq, MLA_V_DIM), BF16),
        jax.ShapeDtypeStruct((n_tok, RET_WIDTH), BF16),
        jax.ShapeDtypeStruct((n_tok, RET_WIDTH), BF16),
        jax.ShapeDtypeStruct((n_tok, RET_WIDTH), BF16),
        jax.ShapeDtypeStruct((n_tok, RET_WIDTH), BF16),
    )
    return pl.pallas_call(
        functools.partial(_prep_kernel, chains=PREP_CHAINS),
        grid=(n_tok // tile,),
        in_specs=[pl.BlockSpec((tile, d), tok), pl.BlockSpec((tile, 1), tok)]
                 + [_const_spec(c.shape) for c in consts],
        out_specs=(
            pl.BlockSpec((1, MLA_HEADS, tile, MLA_HEAD_PAD), head),
            pl.BlockSpec((1, MLA_HEADS, tile, MLA_HEAD_PAD), head),
            pl.BlockSpec((1, MLA_HEADS, tile, MLA_V_DIM), head),
            pl.BlockSpec((tile, RET_WIDTH), tok),
            pl.BlockSpec((tile, RET_WIDTH), tok),
            pl.BlockSpec((tile, RET_WIDTH), tok),
            pl.BlockSpec((tile, RET_WIDTH), tok),
        ),
        out_shape=out_shape,
        compiler_params=pltpu.CompilerParams(dimension_semantics=("parallel",),
                                             vmem_limit_bytes=VMEM_LIMIT_BYTES),
        name="prep",
    )(x2d, pos2d, *consts)


def _flash_kernel(q_ref, k_ref, v_ref, o_ref, m_sc, l_sc, acc_sc, *, tq, tk, hb):
    qi = pl.program_id(2)
    m_sc[...] = jnp.full(m_sc.shape, NEG_INF, F32)
    l_sc[...] = jnp.zeros(l_sc.shape, F32)
    acc_sc[...] = jnp.zeros(acc_sc.shape, F32)

    def step(off, width, rows, mask):
        reps = width // LANES
        for hd in range(hb):
            k = k_ref[0, hd, pl.ds(off, width), :]
            v = v_ref[0, hd, pl.ds(off, width), :]
            s = lax.dot_general(q_ref[0, hd, rows, :], k, (((1,), (1,)), ((), ())), preferred_element_type=F32)
            if mask is not None:
                s = jnp.where(mask, s, NEG_INF)
            m_prev = m_sc[hd, rows, :]
            m_new = jnp.maximum(m_prev, jnp.max(s, axis=-1, keepdims=True))
            alpha = jnp.exp2(m_prev - m_new)
            p = jnp.exp2(s - jnp.concatenate([m_new] * reps, axis=1))
            p_cols = p[:, 0:LANES]
            for c in range(1, reps):
                p_cols = p_cols + p[:, c * LANES:(c + 1) * LANES]
            l_sc[hd, rows, :] = alpha * l_sc[hd, rows, :] + p_cols
            acc_sc[hd, rows, :] = alpha * acc_sc[hd, rows, :] + jnp.dot(p.astype(BF16), v,
                                                                        preferred_element_type=F32)
            m_sc[hd, rows, :] = m_new

    def body(j, c):
        step(pl.multiple_of(j * tk, tk), tk, slice(0, tq), None)
        return c

    lax.fori_loop(0, qi, body, 0)
    half = tk // 2
    base = pl.multiple_of(qi * tk, tk)
    def causal(n_rows):
        return (lax.broadcasted_iota(jnp.int32, (n_rows, half), 1)
                <= lax.broadcasted_iota(jnp.int32, (n_rows, half), 0))

    step(base, half, slice(0, tq), causal(tq))
    step(base + half, half, slice(half, tq), causal(tq - half))
    for hd in range(hb):
        l = jnp.sum(l_sc[hd], axis=-1, keepdims=True)
        o_ref[0, :, hd * LANES:(hd + 1) * LANES] = (acc_sc[hd] / l).astype(o_ref.dtype)


def _flash(q, k, v, tq, tk, hb):
    batch, heads, seq, dqk = q.shape
    dv = v.shape[-1]
    assert tq == tk and seq % tq == 0 and heads % hb == 0
    return pl.pallas_call(
        functools.partial(_flash_kernel, tq=tq, tk=tk, hb=hb),
        grid=(batch, heads // hb, seq // tq),
        in_specs=[
            pl.BlockSpec((1, hb, tq, dqk), lambda b, h, i: (b, h, i, 0)),
            pl.BlockSpec((1, hb, seq, dqk), lambda b, h, i: (b, h, 0, 0)),
            pl.BlockSpec((1, hb, seq, dv), lambda b, h, i: (b, h, 0, 0)),
        ],
        out_specs=pl.BlockSpec((1, tq, hb * dv), lambda b, h, i: (b, i, h)),
        out_shape=jax.ShapeDtypeStruct((batch, seq, heads * dv), BF16),
        scratch_shapes=[pltpu.VMEM((hb, tq, LANES), F32), pltpu.VMEM((hb, tq, LANES), F32),
                        pltpu.VMEM((hb, tq, dv), F32)],
        compiler_params=pltpu.CompilerParams(dimension_semantics=("parallel", "parallel", "arbitrary"),
                                             vmem_limit_bytes=VMEM_LIMIT_BYTES),
        name="flash",
    )(q, k, v)


def _ret_kernel(q_ref, k_ref, v_ref, g_ref, gn_ref, o_ref, state_sc, intra_sc):
    c = pl.program_id(1)
    n = RET_CHUNK
    log_decay = [math.log(1.0 - 2.0 ** (-5.0 - hd)) for hd in range(RET_HEADS)]

    @pl.when(c == 0)
    def _():
        state_sc[...] = jnp.zeros(state_sc.shape, F32)
        diff = (lax.broadcasted_iota(jnp.int32, (n, n), 0) - lax.broadcasted_iota(jnp.int32, (n, n), 1)).astype(F32)
        for hd in range(RET_HEADS):
            intra_sc[hd] = jnp.where(diff >= 0, jnp.exp(jnp.maximum(diff, 0.0) * log_decay[hd]), 0.0)

    idx = lax.broadcasted_iota(jnp.int32, (n, 1), 0).astype(F32)
    for hd in range(RET_HEADS):
        lg = log_decay[hd]
        intra = intra_sc[hd]
        q_decay = jnp.exp((idx + 1.0) * lg)
        k_decay = jnp.exp((n - 1.0 - idx) * lg)
        chunk_decay = math.exp(n * lg)
        sl = slice(hd * LANES, (hd + 1) * LANES)
        qc = q_ref[0, :, sl]
        kc = k_ref[0, :, sl]
        vc = v_ref[0, :, sl]
        state = state_sc[hd]
        scores = lax.dot_general(qc, kc, (((1,), (1,)), ((), ())), preferred_element_type=F32) * intra
        o = (jnp.dot(scores.astype(BF16), vc, preferred_element_type=F32)
             + jnp.dot((qc.astype(F32) * q_decay).astype(BF16), state.astype(BF16), preferred_element_type=F32))
        kd_t = (kc.astype(F32) * k_decay).T.astype(BF16)
        state_sc[hd] = state * chunk_decay + jnp.dot(kd_t, vc, preferred_element_type=F32)
        mu = jnp.mean(o, axis=-1, keepdims=True)
        cen = o - mu
        var = jnp.mean(cen * cen, axis=-1, keepdims=True)
        gate = g_ref[0, :, sl].astype(F32)
        y = cen * lax.rsqrt(var + EPS) * gn_ref[:, sl] * (gate * jax.nn.sigmoid(gate))
        o_ref[0, :, sl] = y.astype(o_ref.dtype)


def _retention(rq, rk, rv, rg, gn_g):
    batch, seq, width = rq.shape
    blk = pl.BlockSpec((1, RET_CHUNK, width), lambda b, c: (b, c, 0))
    return pl.pallas_call(
        _ret_kernel,
        grid=(batch, seq // RET_CHUNK),
        in_specs=[blk, blk, blk, blk, _const_spec(gn_g.shape)],
        out_specs=blk,
        out_shape=jax.ShapeDtypeStruct((batch, seq, width), BF16),
        scratch_shapes=[pltpu.VMEM((RET_HEADS, RET_HEAD_DIM, RET_HEAD_DIM), F32),
                        pltpu.VMEM((RET_HEADS, RET_CHUNK, RET_CHUNK), F32)],
        compiler_params=pltpu.CompilerParams(dimension_semantics=("parallel", "arbitrary")),
        name="retention",
    )(rq, rk, rv, rg, gn_g)


def _memkv_kernel(mem_ref, g_ref, wkv_ref, kg_ref, k_out, v_out):
    d = mem_ref.shape[-1]
    hdim = d // CROSS_HEADS
    m = _rms(mem_ref[0], g_ref[...]).astype(BF16)
    kv = jnp.dot(m, wkv_ref[...], preferred_element_type=F32)
    for hd in range(CROSS_HEADS):
        sl = slice(hd * hdim, (hd + 1) * hdim)
        k_out[0, :, sl] = _rms(kv[:, sl], kg_ref[...]).astype(BF16)
    v_out[0] = kv[:, d:].astype(BF16)


def _memkv(mem, mem_norm_g, w_kv, k_g):
    batch, m_tok, d = mem.shape
    blk = pl.BlockSpec((1, m_tok, d), lambda b: (b, 0, 0))
    return pl.pallas_call(
        _memkv_kernel,
        grid=(batch,),
        in_specs=[blk, _const_spec(mem_norm_g.shape), _const_spec(w_kv.shape), _const_spec(k_g.shape)],
        out_specs=(blk, blk),
        out_shape=(jax.ShapeDtypeStruct((batch, m_tok, d), BF16), jax.ShapeDtypeStruct((batch, m_tok, d), BF16)),
        compiler_params=pltpu.CompilerParams(dimension_semantics=("parallel",),
                                             vmem_limit_bytes=VMEM_LIMIT_BYTES),
        name="memkv",
    )(mem, mem_norm_g, w_kv, k_g)


def _mix_rows(x, a, r, kc_ref, vc_ref, woa_ref, wor_ref, cng_ref, wq_ref, qg_ref, wo_ref, mng_ref, wr_ref,
              br_ref):
    d = x.shape[-1]
    hdim = d // CROSS_HEADS
    x1 = (x
          + jnp.dot(a, woa_ref[...], preferred_element_type=F32)
          + jnp.dot(r, wor_ref[...], preferred_element_type=F32))
    hc = _rms(x1, cng_ref[...]).astype(BF16)
    q = jnp.dot(hc, wq_ref[...], preferred_element_type=F32)
    scale = hdim ** -0.5
    heads = []
    for hd in range(CROSS_HEADS):
        sl = slice(hd * hdim, (hd + 1) * hdim)
        qh = (_rms(q[:, sl], qg_ref[...]) * scale).astype(BF16)
        s = lax.dot_general(qh, kc_ref[0, :, sl], (((1,), (1,)), ((), ())), preferred_element_type=F32)
        s = s - jnp.max(s, axis=-1, keepdims=True)
        e = jnp.exp(s)
        pr = e / jnp.sum(e, axis=-1, keepdims=True)
        heads.append(jnp.dot(pr.astype(BF16), vc_ref[0, :, sl], preferred_element_type=F32).astype(BF16))
    o = jnp.concatenate(heads, axis=-1)
    x2 = x1 + jnp.dot(o, wo_ref[...], preferred_element_type=F32)
    hm = _rms(x2, mng_ref[...])

    hm_hi = hm.astype(BF16)
    hm_lo = (hm - hm_hi.astype(F32)).astype(BF16)
    logits = jnp.dot(jnp.concatenate([hm_hi, hm_lo, hm_hi], axis=-1), wr_ref[...], preferred_element_type=F32)
    biased = logits + br_ref[...]
    lane = lax.broadcasted_iota(jnp.int32, logits.shape, 1)
    lane_f = lane.astype(F32)
    big = float(4 * LANES)
    is_g = (lane >= N_EXPERTS) & (lane < N_EXPERTS + N_GROUPS)
    gl = jnp.where(is_g, logits, NEG_INF)
    gexp = jnp.where(is_g, jnp.exp(gl - jnp.max(gl, axis=-1, keepdims=True)), 0.0)
    gb = jnp.where(is_g, biased, NEG_INF)
    g_lane = jnp.min(jnp.where(gb == jnp.max(gb, axis=-1, keepdims=True), lane_f, big), axis=-1, keepdims=True)
    g_w = (jnp.sum(jnp.where(lane_f == g_lane, gexp, 0.0), axis=-1, keepdims=True)
           / jnp.sum(gexp, axis=-1, keepdims=True))
    g_idx = g_lane.astype(jnp.int32) - N_EXPERTS
    in_grp = (lane < N_EXPERTS) & ((lane // EXPERTS_PER_GROUP) == g_idx)
    el = jnp.where(in_grp, logits, NEG_INF)
    eexp = jnp.where(in_grp, jnp.exp(el - jnp.max(el, axis=-1, keepdims=True)), 0.0)
    eb = jnp.where(in_grp, biased, NEG_INF)
    i1 = jnp.min(jnp.where(eb == jnp.max(eb, axis=-1, keepdims=True), lane_f, big), axis=-1, keepdims=True)
    eb2 = jnp.where(lane_f == i1, NEG_INF, eb)
    i2 = jnp.min(jnp.where(eb2 == jnp.max(eb2, axis=-1, keepdims=True), lane_f, big), axis=-1, keepdims=True)
    psel = jnp.where((lane_f == i1) | (lane_f == i2), eexp, 0.0)
    gate = psel / jnp.sum(psel, axis=-1, keepdims=True) * g_w
    grp_onehot = (lane == g_idx).astype(F32)
    return x2, hm.astype(BF16), gate, grp_onehot


def _group_sort(hm, gate, grp, hms_out, pos_out, meta_out, c):
    t = hm.shape[0]
    ts = SORT_SLOTS
    earlier = (lax.broadcasted_iota(jnp.int32, (t, t), 1) < lax.broadcasted_iota(jnp.int32, (t, t), 0))
    cnt = jnp.dot(earlier.astype(BF16), grp.astype(BF16), preferred_element_type=F32)
    n = jnp.sum(grp, axis=0, keepdims=True)
    n_al = jnp.floor((n + (SORT_ALIGN - 1.0)) * (1.0 / SORT_ALIGN)) * SORT_ALIGN
    lane1 = lax.broadcasted_iota(jnp.int32, (1, LANES), 1)
    start = jnp.zeros((1, LANES), F32)
    run = jnp.zeros((1, 1), F32)
    for g in range(1, N_GROUPS):
        run = run + jnp.sum(jnp.where(lane1 == g - 1, n_al, 0.0), axis=-1, keepdims=True)
        start = jnp.where(lane1 == g, run, start)
    pos = jnp.sum(grp * (cnt + start), axis=-1, keepdims=True)
    pos_out[c * t:(c + 1) * t, :] = pos
    row8 = lax.broadcasted_iota(jnp.int32, (8, LANES), 0)
    meta_out[c] = jnp.where(row8 == 0, start, jnp.where(row8 == 1, n, 0.0)).astype(jnp.int32)
    pos_row = jnp.broadcast_to(pos, (t, LANES)).T[0:1, :]
    perm = (lax.broadcasted_iota(jnp.int32, (ts, t), 0).astype(F32) == pos_row).astype(BF16)
    g_hi = gate.astype(BF16).astype(F32)
    rem = gate - g_hi
    g_mid = rem.astype(BF16).astype(F32)
    g_lo = rem - g_mid
    gate_pieces = (g_hi + pltpu.roll(g_mid, N_EXPERTS, axis=1) + pltpu.roll(g_lo, 2 * N_EXPERTS, axis=1)).astype(BF16)
    hms_out[c * ts:(c + 1) * ts, :] = jnp.dot(perm, jnp.concatenate([hm, gate_pieces], axis=-1),
                                              preferred_element_type=F32).astype(BF16)


def _mix_kernel(x_ref, a_ref, r_ref, *rest, chains):
    *w_refs, x2_out, hms_out, pos_out, meta_out = rest
    rows = x_ref.shape[0] // chains
    for c in range(chains):
        sl = slice(c * rows, (c + 1) * rows)
        x2, hm, gate, grp = _mix_rows(x_ref[sl, :], a_ref[sl, :], r_ref[sl, :], *w_refs)
        x2_out[sl, :] = x2
        _group_sort(hm, gate, grp, hms_out, pos_out, meta_out, c)


def _mix(x2d, a2d, r2d, kc, vc, p, batch, seq, tile):
    n_tok, d = x2d.shape
    half = a2d.shape[-1]
    tiles_per_b = seq // tile
    n_tiles = n_tok // tile
    tok = lambda i: (i, 0)
    mem_blk = pl.BlockSpec((1,) + kc.shape[1:], lambda i: (i // tiles_per_b, 0, 0))
    consts = [p['w_out_a'], p['w_out_r'], p['cross_norm_g'], p['cross_w_q'], p['cross_q_g'], p['cross_w_o'],
              p['moe_norm_g'], p['w_router'], p['b_router']]
    return pl.pallas_call(
        functools.partial(_mix_kernel, chains=MIX_CHAINS),
        grid=(n_tiles,),
        in_specs=[pl.BlockSpec((tile, d), tok), pl.BlockSpec((tile, half), tok), pl.BlockSpec((tile, half), tok),
                  mem_blk, mem_blk] + [_const_spec(c.shape) for c in consts],
        out_specs=(pl.BlockSpec((tile, d), tok), pl.BlockSpec((MIX_CHAINS * SORT_SLOTS, d + LANES), tok),
                   pl.BlockSpec((tile, 1), tok), pl.BlockSpec((MIX_CHAINS, 8, LANES), lambda i: (i, 0, 0))),
        out_shape=(jax.ShapeDtypeStruct((n_tok, d), F32),
                   jax.ShapeDtypeStruct((n_tiles * MIX_CHAINS * SORT_SLOTS, d + LANES), BF16),
                   jax.ShapeDtypeStruct((n_tok, 1), F32),
                   jax.ShapeDtypeStruct((n_tiles * MIX_CHAINS, 8, LANES), jnp.int32)),
        compiler_params=pltpu.CompilerParams(dimension_semantics=("parallel",),
                                             vmem_limit_bytes=VMEM_LIMIT_BYTES),
        name="mix",
    )(x2d, a2d, r2d, kc, vc, *consts)


def _moe_kernel(start_ref, count_ref, hms_ref, wgu_ref, wd_ref, o_ref, *, group, tiles, chunk):
    blk = pl.program_id(0)
    step = pl.program_id(1)
    d = wd_ref.shape[1]
    ff = wgu_ref.shape[2] // 2
    grp = step // (EXPERTS_PER_GROUP // group)

    @pl.when(step == 0)
    def _():
        o_ref[...] = jnp.zeros(o_ref.shape, o_ref.dtype)

    def run_rows(r0, rows):
        lane = lax.broadcasted_iota(jnp.int32, (rows, LANES), 1)
        piece_lane = lane < 3 * N_EXPERTS
        xs = hms_ref[pl.ds(r0, rows), 0:d]
        gates = hms_ref[pl.ds(r0, rows), d:d + LANES].astype(F32)
        acts = []
        for j in range(group):
            e = step * group + j
            au = jnp.dot(xs, wgu_ref[j], preferred_element_type=F32)
            a = au[:, :ff]
            u = au[:, ff:]
            g_e = jnp.sum(jnp.where(piece_lane & ((lane & (N_EXPERTS - 1)) == e), gates, 0.0),
                          axis=-1, keepdims=True)
            acts.append((a * jax.nn.sigmoid(a) * u * g_e).astype(BF16))
        y = jnp.dot(jnp.concatenate(acts, axis=-1), wd_ref[...], preferred_element_type=F32)
        o_ref[pl.ds(r0, rows), :] = (o_ref[pl.ds(r0, rows), :].astype(F32) + y).astype(o_ref.dtype)

    def tile_body(tl, carry):
        idx = (blk * tiles + tl) * N_GROUPS + grp
        base = pl.multiple_of(tl * SORT_SLOTS + start_ref[idx], SORT_ALIGN)
        n = count_ref[idx]

        @pl.when((n > 0) & (n <= MOE_SMALL_CHUNK))
        def _():
            run_rows(base, MOE_SMALL_CHUNK)

        @pl.when(n > MOE_SMALL_CHUNK)
        def _():
            def chunk_body(c, carry2):
                run_rows(pl.multiple_of(base + c * chunk, SORT_ALIGN), chunk)
                return carry2

            lax.fori_loop(0, (n + chunk - 1) // chunk, chunk_body, 0)

        return carry

    lax.fori_loop(0, tiles, tile_body, 0)


def _moe(hms, starts, counts, w_gu, w_d, tiles, group, chunk):
    n_rows, dx = hms.shape
    n_exp, d, ff2 = w_gu.shape
    ff = ff2 // 2
    rows = tiles * SORT_SLOTS
    grid_spec = pltpu.PrefetchScalarGridSpec(
        num_scalar_prefetch=2,
        grid=(n_rows // rows, n_exp // group),
        in_specs=[pl.BlockSpec((rows, dx), lambda i, e, s, c: (i, 0)),
                  pl.BlockSpec((group, d, ff2), lambda i, e, s, c: (e, 0, 0)),
                  pl.BlockSpec((group * ff, d), lambda i, e, s, c: (e, 0))],
        out_specs=pl.BlockSpec((rows, d), lambda i, e, s, c: (i, 0)),
    )
    return pl.pallas_call(
        functools.partial(_moe_kernel, group=group, tiles=tiles, chunk=chunk),
        grid_spec=grid_spec,
        out_shape=jax.ShapeDtypeStruct((n_rows, d), BF16),
        compiler_params=pltpu.CompilerParams(dimension_semantics=("parallel", "arbitrary"),
                                             vmem_limit_bytes=VMEM_LIMIT_BYTES),
        name="moe",
    )(starts, counts, hms, w_gu, w_d)


def _unsort_kernel(x2_ref, pos_ref, ys_ref, o_ref):
    t = x2_ref.shape[0]
    ts = ys_ref.shape[0]
    unperm = (lax.broadcasted_iota(jnp.int32, (t, ts), 1).astype(F32) == pos_ref[...]).astype(BF16)
    o_ref[...] = x2_ref[...] + jnp.dot(unperm, ys_ref[...], preferred_element_type=F32)


def _unsort(x2, pos, ys, tile):
    n_tok, d = x2.shape
    tok = lambda i: (i, 0)
    return pl.pallas_call(
        _unsort_kernel,
        grid=(n_tok // tile,),
        in_specs=[pl.BlockSpec((tile, d), tok), pl.BlockSpec((tile, 1), tok), pl.BlockSpec((SORT_SLOTS, d), tok)],
        out_specs=pl.BlockSpec((tile, d), tok),
        out_shape=jax.ShapeDtypeStruct((n_tok, d), F32),
        compiler_params=pltpu.CompilerParams(dimension_semantics=("parallel",)),
        name="unsort",
    )(x2, pos, ys)


def _row(v):
    return v.reshape(1, -1).astype(F32)


def _pad_lanes(v, width=LANES):
    return jnp.pad(v, [(0, 0)] * (v.ndim - 1) + [(0, width - v.shape[-1])])


def _swap_halves(v):
    half = v.shape[-1] // 2
    return jnp.concatenate([v[..., half:], v[..., :half]], axis=-1)


def _layout_params(l, attn_norm_g, w_in, mla_q_norm_g, mla_w_uq, mla_kv_norm_g, mla_w_ukv, mla_q_qk_g,
                   mla_k_qk_g, ret_gn_g, w_out, cross_norm_g, mem_norm_g, cross_w_q, cross_w_kv, cross_q_qk_g,
                   cross_k_qk_g, cross_w_o, moe_norm_g, router_w_group, router_b_group, router_w_expert,
                   router_b_expert):
    p = {}
    d = w_in.shape[1]
    wi = w_in[l]
    c_q, c_kv, k_pe, rq, rk, rv, rg = jnp.split(
        wi, np.cumsum([MLA_Q_RANK, MLA_KV_RANK, MLA_ROPE_DIM, RET_WIDTH, RET_WIDTH, RET_WIDTH]).tolist(), axis=1)
    p['w_in'] = jnp.concatenate([c_q, c_kv, rq, rk, rv, rg, _pad_lanes(k_pe), _pad_lanes(_swap_halves(k_pe))],
                                axis=1).astype(BF16)
    wq = mla_w_uq[l].reshape(MLA_Q_RANK, MLA_HEADS, MLA_QK_DIM)
    wq_rope = wq[:, :, MLA_NOPE_DIM:]
    p['w_uq'] = jnp.concatenate([wq[:, :, :MLA_NOPE_DIM], _pad_lanes(wq_rope), _pad_lanes(_swap_halves(wq_rope))],
                                axis=2).reshape(MLA_Q_RANK, MLA_HEADS * Q_HEAD_COLS).astype(BF16)
    wkv = mla_w_ukv[l].reshape(MLA_KV_RANK, MLA_HEADS, MLA_NOPE_DIM + MLA_V_DIM)
    p['w_ukv'] = jnp.concatenate([wkv[:, :, :MLA_NOPE_DIM].reshape(MLA_KV_RANK, -1),
                                  wkv[:, :, MLA_NOPE_DIM:].reshape(MLA_KV_RANK, -1)], axis=1).astype(BF16)
    p['attn_norm_g'] = _row(attn_norm_g[l])
    p['q_norm_g'] = _row(mla_q_norm_g[l])
    p['kv_norm_g'] = _row(mla_kv_norm_g[l])
    for name, g in (('q', mla_q_qk_g[l]), ('k', mla_k_qk_g[l])):
        p[name + '_g_nope'] = _row(g[:MLA_NOPE_DIM])
        p[name + '_g_rope'] = _row(_pad_lanes(g[MLA_NOPE_DIM:]))
        p[name + '_g_rope_sw'] = _row(_pad_lanes(_swap_halves(g[MLA_NOPE_DIM:])))
    hm_ = MLA_ROPE_DIM // 2
    f_m = ROPE_BASE ** (-jnp.arange(hm_, dtype=F32) / hm_)
    p['freq_m'] = _row(_pad_lanes(jnp.concatenate([f_m, f_m])))
    p['sign_m'] = _row(_pad_lanes(jnp.concatenate([-jnp.ones(hm_, F32), jnp.ones(hm_, F32)])))
    hr_ = RET_HEAD_DIM // 2
    f_r = ROPE_BASE ** (-jnp.arange(hr_, dtype=F32) / hr_)
    p['freq_r'] = _row(jnp.concatenate([f_r, f_r]))
    p['sign_r'] = _row(jnp.concatenate([-jnp.ones(hr_, F32), jnp.ones(hr_, F32)]))
    p['ret_gn_g'] = _row(ret_gn_g[l])
    half = MLA_HEADS * MLA_V_DIM
    p['w_out_a'] = w_out[l][:half].astype(BF16)
    p['w_out_r'] = w_out[l][half:].astype(BF16)
    p['cross_norm_g'] = _row(cross_norm_g[l])
    p['mem_norm_g'] = _row(mem_norm_g[l])
    p['cross_w_q'] = cross_w_q[l].astype(BF16)
    p['cross_w_kv'] = cross_w_kv[l].astype(BF16)
    p['cross_q_g'] = _row(cross_q_qk_g[l])
    p['cross_k_g'] = _row(cross_k_qk_g[l])
    p['cross_w_o'] = cross_w_o[l].astype(BF16)
    p['moe_norm_g'] = _row(moe_norm_g[l])
    w_r = _pad_lanes(jnp.concatenate([router_w_expert[l], router_w_group[l]], axis=1)).astype(F32)
    w_r_hi = w_r.astype(BF16)
    w_r_lo = (w_r - w_r_hi.astype(F32)).astype(BF16)
    p['w_router'] = jnp.concatenate([w_r_hi, w_r_hi, w_r_lo], axis=0)
    p['b_router'] = _row(_pad_lanes(jnp.concatenate([router_b_expert[l], router_b_group[l]])))
    return p


PREP_TILE = 1024
PREP_CHAINS = 1
FLASH_TQ = 1024
FLASH_TK = 1024
FLASH_HEADS_PER_STEP = 4
SORT_TILE = 512
MIX_CHAINS = 2
MIX_TILE = SORT_TILE * MIX_CHAINS
SORT_ALIGN = 16
MOE_CHUNK = 144
MOE_SMALL_CHUNK = 128
SORT_SLOTS = -(-(SORT_TILE + N_GROUPS * (SORT_ALIGN - 1) + MOE_CHUNK) // LANES) * LANES
MOE_TILES_PER_BLOCK = 4
MOE_EXPERTS_PER_STEP = 8


def kernel(x, mem, positions, attn_norm_g, w_in, mla_q_norm_g, mla_w_uq, mla_kv_norm_g, mla_w_ukv, mla_q_qk_g, mla_k_qk_g, ret_gn_g, w_out, cross_norm_g, mem_norm_g, cross_w_q, cross_w_kv, cross_q_qk_g, cross_k_qk_g, cross_w_o, moe_norm_g, router_w_group, router_b_group, router_w_expert, router_b_expert, expert_w_gate, expert_w_up, expert_w_down):
    batch, seq, d = x.shape
    depth = w_in.shape[0]
    pos2d = positions.reshape(batch * seq, 1).astype(jnp.int32)
    x2d = x.reshape(batch * seq, d)
    for l in range(depth):
        p = _layout_params(l, attn_norm_g, w_in, mla_q_norm_g, mla_w_uq, mla_kv_norm_g, mla_w_ukv, mla_q_qk_g,
                           mla_k_qk_g, ret_gn_g, w_out, cross_norm_g, mem_norm_g, cross_w_q, cross_w_kv,
                           cross_q_qk_g, cross_k_qk_g, cross_w_o, moe_norm_g, router_w_group, router_b_group,
                           router_w_expert, router_b_expert)
        w_gu = jnp.concatenate([expert_w_gate[l], expert_w_up[l]], axis=-1).astype(BF16)
        w_d = expert_w_down[l].reshape(-1, d).astype(BF16)

        q, k, v, rq, rk, rv, rg = _prep(x2d, pos2d, p, batch, seq, PREP_TILE)
        a_out = _flash(q, k, v, FLASH_TQ, FLASH_TK, FLASH_HEADS_PER_STEP)
        shp = (batch, seq, RET_WIDTH)
        r_out = _retention(rq.reshape(shp), rk.reshape(shp), rv.reshape(shp), rg.reshape(shp), p['ret_gn_g'])
        kc, vc = _memkv(mem, p['mem_norm_g'], p['cross_w_kv'], p['cross_k_g'])
        x2, hms, pos, meta = _mix(x2d, a_out.reshape(batch * seq, -1), r_out.reshape(batch * seq, -1), kc, vc, p,
                                  batch, seq, MIX_TILE)
        starts = meta[:, 0, :N_GROUPS].reshape(-1)
        counts = meta[:, 1, :N_GROUPS].reshape(-1)
        ys = _moe(hms, starts, counts, w_gu, w_d, MOE_TILES_PER_BLOCK, MOE_EXPERTS_PER_STEP, MOE_CHUNK)
        x2d = _unsort(x2, pos, ys, SORT_TILE)
    return x2d.reshape(batch, seq, d)
```

```python
import functools
import math

import numpy as np
import jax
import jax.numpy as jnp
from jax import lax
from jax.experimental import pallas as pl
from jax.experimental.pallas import tpu as pltpu

F32 = jnp.float32
BF16 = jnp.bfloat16

LANES = 128
VMEM_LIMIT_BYTES = 56 * 1024 * 1024

MLA_HEADS = 4
MLA_NOPE_DIM = 128
MLA_ROPE_DIM = 64
MLA_QK_DIM = MLA_NOPE_DIM + MLA_ROPE_DIM
MLA_V_DIM = 128
MLA_Q_RANK = 256
MLA_KV_RANK = 128
MLA_HEAD_PAD = 256
RET_HEADS = 4
RET_HEAD_DIM = 128
RET_WIDTH = RET_HEADS * RET_HEAD_DIM
RET_CHUNK = 256
CROSS_HEADS = 4
N_GROUPS = 4
EXPERTS_PER_GROUP = 8
N_EXPERTS = 32
ROPE_BASE = 10000.0
EPS = 1e-6
NEG_INF = -1e30

_C_Q = 0
_C_KV = _C_Q + MLA_Q_RANK
_RQ = _C_KV + MLA_KV_RANK
_RK = _RQ + RET_WIDTH
_RV = _RK + RET_WIDTH
_RG = _RV + RET_WIDTH
_KPE = _RG + RET_WIDTH
_KPE_SW = _KPE + LANES
IN_COLS_PAD = _KPE_SW + LANES
Q_HEAD_COLS = 3 * LANES


def _rms(x, g):
    return x * lax.rsqrt(jnp.mean(x * x, axis=-1, keepdims=True) + EPS) * g


def _const_spec(shape):
    nd = len(shape)
    return pl.BlockSpec(shape, lambda *_: (0,) * nd, pipeline_mode=pl.Buffered(1))


def _prep_rows(rs, x_ref, pos_ref, g_ref, win_ref, qng_ref, wuq_ref, kvng_ref, wukv_ref,
               qgn_ref, qgr_ref, qgs_ref, kgn_ref, kgr_ref, kgs_ref,
               fm_ref, sm_ref, fr_ref, sr_ref,
               q_out, k_out, v_out, rq_out, rk_out, rv_out, rg_out):
    x = x_ref[rs, :]
    h = _rms(x, g_ref[...]).astype(BF16)
    proj = jnp.dot(h, win_ref[...], preferred_element_type=F32)

    posf = pos_ref[rs, :].astype(F32)
    ang_m = posf * fm_ref[...]
    cos_m = jnp.cos(ang_m)
    sin_m = jnp.sin(ang_m) * sm_ref[...]
    ang_r = posf * fr_ref[...]
    cos_r = jnp.cos(ang_r)
    sin_r = jnp.sin(ang_r) * sr_ref[...]

    c_q = proj[:, _C_Q:_C_Q + MLA_Q_RANK]
    qall = jnp.dot(_rms(c_q, qng_ref[...]).astype(BF16), wuq_ref[...], preferred_element_type=F32)
    q_scale = MLA_QK_DIM ** -0.5 * math.log2(math.e)
    for hd in range(MLA_HEADS):
        base = hd * Q_HEAD_COLS
        nope = qall[:, base:base + LANES]
        rope = qall[:, base + LANES:base + 2 * LANES]
        rope_sw = qall[:, base + 2 * LANES:base + 3 * LANES]
        ss = jnp.sum(nope * nope, axis=-1, keepdims=True) + jnp.sum(rope * rope, axis=-1, keepdims=True)
        r = lax.rsqrt(ss * (1.0 / MLA_QK_DIM) + EPS) * q_scale
        roped = rope * qgr_ref[...] * cos_m + rope_sw * qgs_ref[...] * sin_m
        q_out[0, hd, rs, 0:LANES] = (nope * qgn_ref[...] * r).astype(BF16)
        q_out[0, hd, rs, LANES:2 * LANES] = (roped * r).astype(BF16)

    c_kv = proj[:, _C_KV:_C_KV + MLA_KV_RANK]
    kv = jnp.dot(_rms(c_kv, kvng_ref[...]).astype(BF16), wukv_ref[...], preferred_element_type=F32)
    k_pe = proj[:, _KPE:_KPE + LANES]
    k_pe_sw = proj[:, _KPE_SW:_KPE_SW + LANES]
    kr = k_pe * kgr_ref[...] * cos_m + k_pe_sw * kgs_ref[...] * sin_m
    ss_pe = jnp.sum(k_pe * k_pe, axis=-1, keepdims=True)
    for hd in range(MLA_HEADS):
        k_nope = kv[:, hd * LANES:(hd + 1) * LANES]
        ss = jnp.sum(k_nope * k_nope, axis=-1, keepdims=True) + ss_pe
        r = lax.rsqrt(ss * (1.0 / MLA_QK_DIM) + EPS)
        k_out[0, hd, rs, 0:LANES] = (k_nope * kgn_ref[...] * r).astype(BF16)
        k_out[0, hd, rs, LANES:2 * LANES] = (kr * r).astype(BF16)
        v_out[0, hd, rs, :] = kv[:, (MLA_HEADS + hd) * LANES:(MLA_HEADS + hd + 1) * LANES].astype(BF16)

    k_scale = RET_HEAD_DIM ** -0.5
    for hd in range(RET_HEADS):
        sl = slice(hd * LANES, (hd + 1) * LANES)
        rq = proj[:, _RQ + hd * LANES:_RQ + (hd + 1) * LANES]
        rk = proj[:, _RK + hd * LANES:_RK + (hd + 1) * LANES]
        rq_out[rs, sl] = (rq * cos_r + pltpu.roll(rq, LANES // 2, axis=1) * sin_r).astype(BF16)
        rk_out[rs, sl] = ((rk * cos_r + pltpu.roll(rk, LANES // 2, axis=1) * sin_r) * k_scale).astype(BF16)
    rv_out[rs, :] = proj[:, _RV:_RV + RET_WIDTH].astype(BF16)
    rg_out[rs, :] = proj[:, _RG:_RG + RET_WIDTH].astype(BF16)


def _prep_kernel(x_ref, *refs, chains):
    rows = x_ref.shape[0] // chains
    for c in range(chains):
        _prep_rows(slice(c * rows, (c + 1) * rows), x_ref, *refs)


def _prep(x2d, pos2d, p, batch, seq, tile):
    n_tok, d = x2d.shape
    tiles_per_b = seq // tile
    tok = lambda i: (i, 0)
    head = lambda i: (i // tiles_per_b, 0, i % tiles_per_b, 0)
    consts = [p['attn_norm_g'], p['w_in'], p['q_norm_g'], p['w_uq'], p['kv_norm_g'], p['w_ukv'],
              p['q_g_nope'], p['q_g_rope'], p['q_g_rope_sw'], p['k_g_nope'], p['k_g_rope'], p['k_g_rope_sw'],
              p['freq_m'], p['sign_m'], p['freq_r'], p['sign_r']]
    out_shape = (
        jax.ShapeDtypeStruct((batch, MLA_HEADS, seq, MLA_HEAD_PAD), BF16),
        jax.ShapeDtypeStruct((batch, MLA_HEADS, seq, MLA_HEAD_PAD), BF16),
        jax.ShapeDtypeStruct((batch, MLA_HEADS, seq, MLA_V_DIM), BF16),
        jax.ShapeDtypeStruct((n_tok, RET_WIDTH), BF16),
        jax.ShapeDtypeStruct((n_tok, RET_WIDTH), BF16),
        jax.ShapeDtypeStruct((n_tok, RET_WIDTH), BF16),
        jax.ShapeDtypeStruct((n_tok, RET_WIDTH), BF16),
    )
    return pl.pallas_call(
        functools.partial(_prep_kernel, chains=PREP_CHAINS),
        grid=(n_tok // tile,),
        in_specs=[pl.BlockSpec((tile, d), tok), pl.BlockSpec((tile, 1), tok)]
                 + [_const_spec(c.shape) for c in consts],
        out_specs=(
            pl.BlockSpec((1, MLA_HEADS, tile, MLA_HEAD_PAD), head),
            pl.BlockSpec((1, MLA_HEADS, tile, MLA_HEAD_PAD), head),
            pl.BlockSpec((1, MLA_HEADS, tile, MLA_V_DIM), head),
            pl.BlockSpec((tile, RET_WIDTH), tok),
            pl.BlockSpec((tile, RET_WIDTH), tok),
            pl.BlockSpec((tile, RET_WIDTH), tok),
            pl.BlockSpec((tile, RET_WIDTH), tok),
        ),
        out_shape=out_shape,
        compiler_params=pltpu.CompilerParams(dimension_semantics=("parallel",),
                                             vmem_limit_bytes=VMEM_LIMIT_BYTES),
        name="prep",
    )(x2d, pos2d, *consts)


def _flash_kernel(q_ref, k_ref, v_ref, o_ref, m_sc, l_sc, acc_sc, *, tq, tk, hb):
    qi = pl.program_id(2)
    m_sc[...] = jnp.full(m_sc.shape, NEG_INF, F32)
    l_sc[...] = jnp.zeros(l_sc.shape, F32)
    acc_sc[...] = jnp.zeros(acc_sc.shape, F32)

    def step(off, width, rows, mask):
        reps = width // LANES
        for hd in range(hb):
            k = k_ref[0, hd, pl.ds(off, width), :]
            v = v_ref[0, hd, pl.ds(off, width), :]
            s = lax.dot_general(q_ref[0, hd, rows, :], k, (((1,), (1,)), ((), ())), preferred_element_type=F32)
            if mask is not None:
                s = jnp.where(mask, s, NEG_INF)
            m_prev = m_sc[hd, rows, :]
            m_new = jnp.maximum(m_prev, jnp.max(s, axis=-1, keepdims=True))
            alpha = jnp.exp2(m_prev - m_new)
            p = jnp.exp2(s - jnp.concatenate([m_new] * reps, axis=1))
            p_cols = p[:, 0:LANES]
            for c in range(1, reps):
                p_cols = p_cols + p[:, c * LANES:(c + 1) * LANES]
            l_sc[hd, rows, :] = alpha * l_sc[hd, rows, :] + p_cols
            acc_sc[hd, rows, :] = alpha * acc_sc[hd, rows, :] + jnp.dot(p.astype(BF16), v,
                                                                        preferred_element_type=F32)
            m_sc[hd, rows, :] = m_new

    def body(j, c):
        step(pl.multiple_of(j * tk, tk), tk, slice(0, tq), None)
        return c

    lax.fori_loop(0, qi, body, 0)
    half = tk // 2
    base = pl.multiple_of(qi * tk, tk)

    def causal(n_rows):
        return (lax.broadcasted_iota(jnp.int32, (n_rows, half), 1)
                <= lax.broadcasted_iota(jnp.int32, (n_rows, half), 0))

    step(base, half, slice(0, tq), causal(tq))
    step(base + half, half, slice(half, tq), causal(tq - half))
    for hd in range(hb):
        l = jnp.sum(l_sc[hd], axis=-1, keepdims=True)
        o_ref[0, :, hd * LANES:(hd + 1) * LANES] = (acc_sc[hd] / l).astype(o_ref.dtype)


def _flash(q, k, v, tq, tk, hb):
    batch, heads, seq, dqk = q.shape
    dv = v.shape[-1]
    assert tq == tk and seq % tq == 0 and heads % hb == 0
    return pl.pallas_call(
        functools.partial(_flash_kernel, tq=tq, tk=tk, hb=hb),
        grid=(batch, heads // hb, seq // tq),
        in_specs=[
            pl.BlockSpec((1, hb, tq, dqk), lambda b, h, i: (b, h, i, 0)),
            pl.BlockSpec((1, hb, seq, dqk), lambda b, h, i: (b, h, 0, 0)),
            pl.BlockSpec((1, hb, seq, dv), lambda b, h, i: (b, h, 0, 0)),
        ],
        out_specs=pl.BlockSpec((1, tq, hb * dv), lambda b, h, i: (b, i, h)),
        out_shape=jax.ShapeDtypeStruct((batch, seq, heads * dv), BF16),
        scratch_shapes=[pltpu.VMEM((hb, tq, LANES), F32), pltpu.VMEM((hb, tq, LANES), F32),
                        pltpu.VMEM((hb, tq, dv), F32)],
        compiler_params=pltpu.CompilerParams(dimension_semantics=("parallel", "parallel", "arbitrary"),
                                             vmem_limit_bytes=VMEM_LIMIT_BYTES),
        name="flash",
    )(q, k, v)


def _ret_kernel(q_ref, k_ref, v_ref, g_ref, gn_ref, o_ref, state_sc, intra_sc):
    c = pl.program_id(1)
    n = RET_CHUNK
    log_decay = [math.log(1.0 - 2.0 ** (-5.0 - hd)) for hd in range(RET_HEADS)]

    @pl.when(c == 0)
    def _():
        state_sc[...] = jnp.zeros(state_sc.shape, F32)
        diff = (lax.broadcasted_iota(jnp.int32, (n, n), 0) - lax.broadcasted_iota(jnp.int32, (n, n), 1)).astype(F32)
        for hd in range(RET_HEADS):
            intra_sc[hd] = jnp.where(diff >= 0, jnp.exp(jnp.maximum(diff, 0.0) * log_decay[hd]), 0.0)

    idx = lax.broadcasted_iota(jnp.int32, (n, 1), 0).astype(F32)
    for hd in range(RET_HEADS):
        lg = log_decay[hd]
        intra = intra_sc[hd]
        q_decay = jnp.exp((idx + 1.0) * lg)
        k_decay = jnp.exp((n - 1.0 - idx) * lg)
        chunk_decay = math.exp(n * lg)
        sl = slice(hd * LANES, (hd + 1) * LANES)
        qc = q_ref[0, :, sl]
        kc = k_ref[0, :, sl]
        vc = v_ref[0, :, sl]
        state = state_sc[hd]
        scores = lax.dot_general(qc, kc, (((1,), (1,)), ((), ())), preferred_element_type=F32) * intra
        o = (jnp.dot(scores.astype(BF16), vc, preferred_element_type=F32)
             + jnp.dot((qc.astype(F32) * q_decay).astype(BF16), state.astype(BF16), preferred_element_type=F32))
        kd_t = (kc.astype(F32) * k_decay).T.astype(BF16)
        state_sc[hd] = state * chunk_decay + jnp.dot(kd_t, vc, preferred_element_type=F32)
        mu = jnp.mean(o, axis=-1, keepdims=True)
        cen = o - mu
        var = jnp.mean(cen * cen, axis=-1, keepdims=True)
        gate = g_ref[0, :, sl].astype(F32)
        y = cen * lax.rsqrt(var + EPS) * gn_ref[:, sl] * (gate * jax.nn.sigmoid(gate))
        o_ref[0, :, sl] = y.astype(o_ref.dtype)


def _retention(rq, rk, rv, rg, gn_g):
    batch, seq, width = rq.shape
    blk = pl.BlockSpec((1, RET_CHUNK, width), lambda b, c: (b, c, 0))
    return pl.pallas_call(
        _ret_kernel,
        grid=(batch, seq // RET_CHUNK),
        in_specs=[blk, blk, blk, blk, _const_spec(gn_g.shape)],
        out_specs=blk,
        out_shape=jax.ShapeDtypeStruct((batch, seq, width), BF16),
        scratch_shapes=[pltpu.VMEM((RET_HEADS, RET_HEAD_DIM, RET_HEAD_DIM), F32),
                        pltpu.VMEM((RET_HEADS, RET_CHUNK, RET_CHUNK), F32)],
        compiler_params=pltpu.CompilerParams(dimension_semantics=("parallel", "arbitrary")),
        name="retention",
    )(rq, rk, rv, rg, gn_g)


def _memkv_kernel(mem_ref, g_ref, wkv_ref, kg_ref, k_out, v_out):
    d = mem_ref.shape[-1]
    hdim = d // CROSS_HEADS
    m = _rms(mem_ref[0], g_ref[...]).astype(BF16)
    kv = jnp.dot(m, wkv_ref[...], preferred_element_type=F32)
    for hd in range(CROSS_HEADS):
        sl = slice(hd * hdim, (hd + 1) * hdim)
        k_out[0, :, sl] = _rms(kv[:, sl], kg_ref[...]).astype(BF16)
    v_out[0] = kv[:, d:].astype(BF16)


def _memkv(mem, mem_norm_g, w_kv, k_g):
    batch, m_tok, d = mem.shape
    blk = pl.BlockSpec((1, m_tok, d), lambda b: (b, 0, 0))
    return pl.pallas_call(
        _memkv_kernel,
        grid=(batch,),
        in_specs=[blk, _const_spec(mem_norm_g.shape), _const_spec(w_kv.shape), _const_spec(k_g.shape)],
        out_specs=(blk, blk),
        out_shape=(jax.ShapeDtypeStruct((batch, m_tok, d), BF16), jax.ShapeDtypeStruct((batch, m_tok, d), BF16)),
        compiler_params=pltpu.CompilerParams(dimension_semantics=("parallel",),
                                             vmem_limit_bytes=VMEM_LIMIT_BYTES),
        name="memkv",
    )(mem, mem_norm_g, w_kv, k_g)


def _mix_rows(x, a, r, kc_ref, vc_ref, woa_ref, wor_ref, cng_ref, wq_ref, qg_ref, wo_ref, mng_ref, wr_ref,
              br_ref):
    d = x.shape[-1]
    hdim = d // CROSS_HEADS
    x1 = (x
          + jnp.dot(a, woa_ref[...], preferred_element_type=F32)
          + jnp.dot(r, wor_ref[...], preferred_element_type=F32))
    hc = _rms(x1, cng_ref[...]).astype(BF16)
    q = jnp.dot(hc, wq_ref[...], preferred_element_type=F32)
    scale = hdim ** -0.5
    heads = []
    for hd in range(CROSS_HEADS):
        sl = slice(hd * hdim, (hd + 1) * hdim)
        qh = (_rms(q[:, sl], qg_ref[...]) * scale).astype(BF16)
        s = lax.dot_general(qh, kc_ref[0, :, sl], (((1,), (1,)), ((), ())), preferred_element_type=F32)
        s = s - jnp.max(s, axis=-1, keepdims=True)
        e = jnp.exp(s)
        pr = e / jnp.sum(e, axis=-1, keepdims=True)
        heads.append(jnp.dot(pr.astype(BF16), vc_ref[0, :, sl], preferred_element_type=F32).astype(BF16))
    o = jnp.concatenate(heads, axis=-1)
    x2 = x1 + jnp.dot(o, wo_ref[...], preferred_element_type=F32)
    hm = _rms(x2, mng_ref[...])

    hm_hi = hm.astype(BF16)
    hm_lo = (hm - hm_hi.astype(F32)).astype(BF16)
    hi_hl = jnp.dot(hm_hi, wr_ref[...], preferred_element_type=F32)
    logits = (hi_hl[:, :LANES] + hi_hl[:, LANES:]
              + jnp.dot(hm_lo, wr_ref[:, :LANES], preferred_element_type=F32))
    biased = logits + br_ref[...]
    lane = lax.broadcasted_iota(jnp.int32, logits.shape, 1)
    lane_f = lane.astype(F32)
    big = float(4 * LANES)
    is_g = (lane >= N_EXPERTS) & (lane < N_EXPERTS + N_GROUPS)
    gl = jnp.where(is_g, logits, NEG_INF)
    gexp = jnp.where(is_g, jnp.exp(gl - jnp.max(gl, axis=-1, keepdims=True)), 0.0)
    gb = jnp.where(is_g, biased, NEG_INF)
    g_lane = jnp.min(jnp.where(gb == jnp.max(gb, axis=-1, keepdims=True), lane_f, big), axis=-1, keepdims=True)
    g_w = (jnp.sum(jnp.where(lane_f == g_lane, gexp, 0.0), axis=-1, keepdims=True)
           / jnp.sum(gexp, axis=-1, keepdims=True))
    g_idx = g_lane.astype(jnp.int32) - N_EXPERTS
    in_grp = (lane < N_EXPERTS) & ((lane // EXPERTS_PER_GROUP) == g_idx)
    el = jnp.where(in_grp, logits, NEG_INF)
    eexp = jnp.where(in_grp, jnp.exp(el - jnp.max(el, axis=-1, keepdims=True)), 0.0)
    eb = jnp.where(in_grp, biased, NEG_INF)
    i1 = jnp.min(jnp.where(eb == jnp.max(eb, axis=-1, keepdims=True), lane_f, big), axis=-1, keepdims=True)
    eb2 = jnp.where(lane_f == i1, NEG_INF, eb)
    i2 = jnp.min(jnp.where(eb2 == jnp.max(eb2, axis=-1, keepdims=True), lane_f, big), axis=-1, keepdims=True)
    psel = jnp.where((lane_f == i1) | (lane_f == i2), eexp, 0.0)
    gate = psel / jnp.sum(psel, axis=-1, keepdims=True) * g_w
    grp_onehot = (lane == g_idx).astype(F32)
    return x2, hm.astype(BF16), gate, grp_onehot


def _group_sort(hm, gate, grp, hms_out, pos_out, meta_out, c):
    t = hm.shape[0]
    ts = SORT_SLOTS
    earlier = (lax.broadcasted_iota(jnp.int32, (t, t), 1) < lax.broadcasted_iota(jnp.int32, (t, t), 0))
    cnt = jnp.dot(earlier.astype(BF16), grp.astype(BF16), preferred_element_type=F32)
    n = jnp.sum(grp, axis=0, keepdims=True)
    n_al = jnp.floor((n + (SORT_ALIGN - 1.0)) * (1.0 / SORT_ALIGN)) * SORT_ALIGN
    lane1 = lax.broadcasted_iota(jnp.int32, (1, LANES), 1)
    start = jnp.zeros((1, LANES), F32)
    run = jnp.zeros((1, 1), F32)
    for g in range(1, N_GROUPS):
        run = run + jnp.sum(jnp.where(lane1 == g - 1, n_al, 0.0), axis=-1, keepdims=True)
        start = jnp.where(lane1 == g, run, start)
    pos = jnp.sum(grp * (cnt + start), axis=-1, keepdims=True)
    pos_out[c * t:(c + 1) * t, :] = pos
    row8 = lax.broadcasted_iota(jnp.int32, (8, LANES), 0)
    meta_out[c] = jnp.where(row8 == 0, start, jnp.where(row8 == 1, n, 0.0)).astype(jnp.int32)
    pos_row = jnp.broadcast_to(pos, (t, LANES)).T[0:1, :]
    perm = (lax.broadcasted_iota(jnp.int32, (ts, t), 0).astype(F32) == pos_row).astype(BF16)
    g_hi = gate.astype(BF16).astype(F32)
    rem = gate - g_hi
    g_mid = rem.astype(BF16).astype(F32)
    g_lo = rem - g_mid
    gate_pieces = (g_hi + pltpu.roll(g_mid, N_EXPERTS, axis=1) + pltpu.roll(g_lo, 2 * N_EXPERTS, axis=1)).astype(BF16)
    hms_out[c * ts:(c + 1) * ts, :] = jnp.dot(perm, jnp.concatenate([hm, gate_pieces], axis=-1),
                                              preferred_element_type=F32).astype(BF16)


def _mix_kernel(x_ref, a_ref, r_ref, *rest, chains):
    *w_refs, x2_out, hms_out, pos_out, meta_out = rest
    rows = x_ref.shape[0] // chains
    for c in range(chains):
        sl = slice(c * rows, (c + 1) * rows)
        x2, hm, gate, grp = _mix_rows(x_ref[sl, :], a_ref[sl, :], r_ref[sl, :], *w_refs)
        x2_out[sl, :] = x2
        _group_sort(hm, gate, grp, hms_out, pos_out, meta_out, c)


def _mix(x2d, a2d, r2d, kc, vc, p, batch, seq, tile):
    n_tok, d = x2d.shape
    half = a2d.shape[-1]
    tiles_per_b = seq // tile
    n_tiles = n_tok // tile
    tok = lambda i: (i, 0)
    mem_blk = pl.BlockSpec((1,) + kc.shape[1:], lambda i: (i // tiles_per_b, 0, 0))
    consts = [p['w_out_a'], p['w_out_r'], p['cross_norm_g'], p['cross_w_q'], p['cross_q_g'], p['cross_w_o'],
              p['moe_norm_g'], p['w_router'], p['b_router']]
    return pl.pallas_call(
        functools.partial(_mix_kernel, chains=MIX_CHAINS),
        grid=(n_tiles,),
        in_specs=[pl.BlockSpec((tile, d), tok), pl.BlockSpec((tile, half), tok), pl.BlockSpec((tile, half), tok),
                  mem_blk, mem_blk] + [_const_spec(c.shape) for c in consts],
        out_specs=(pl.BlockSpec((tile, d), tok), pl.BlockSpec((MIX_CHAINS * SORT_SLOTS, d + LANES), tok),
                   pl.BlockSpec((tile, 1), tok), pl.BlockSpec((MIX_CHAINS, 8, LANES), lambda i: (i, 0, 0))),
        out_shape=(jax.ShapeDtypeStruct((n_tok, d), F32),
                   jax.ShapeDtypeStruct((n_tiles * MIX_CHAINS * SORT_SLOTS, d + LANES), BF16),
                   jax.ShapeDtypeStruct((n_tok, 1), F32),
                   jax.ShapeDtypeStruct((n_tiles * MIX_CHAINS, 8, LANES), jnp.int32)),
        compiler_params=pltpu.CompilerParams(dimension_semantics=("parallel",),
                                             vmem_limit_bytes=VMEM_LIMIT_BYTES),
        name="mix",
    )(x2d, a2d, r2d, kc, vc, *consts)


def _moe_kernel(start_ref, count_ref, hms_ref, wgu_ref, wd_ref, o_ref, *, group, tiles, chunk):
    blk = pl.program_id(0)
    step = pl.program_id(1)
    d = wd_ref.shape[1]
    ff = wgu_ref.shape[2] // 2
    grp = step // (EXPERTS_PER_GROUP // group)

    @pl.when(step == 0)
    def _():
        o_ref[...] = jnp.zeros(o_ref.shape, o_ref.dtype)

    def run_rows(r0, rows):
        lane = lax.broadcasted_iota(jnp.int32, (rows, LANES), 1)
        piece_lane = lane < 3 * N_EXPERTS
        xs = hms_ref[pl.ds(r0, rows), 0:d]
        gates = hms_ref[pl.ds(r0, rows), d:d + LANES].astype(F32)
        acts = []
        for j in range(group):
            e = step * group + j
            au = jnp.dot(xs, wgu_ref[j], preferred_element_type=F32)
            a = au[:, :ff]
            u = au[:, ff:]
            g_e = jnp.sum(jnp.where(piece_lane & ((lane & (N_EXPERTS - 1)) == e), gates, 0.0),
                          axis=-1, keepdims=True)
            acts.append((a * jax.nn.sigmoid(a) * u * g_e).astype(BF16))
        y = jnp.dot(jnp.concatenate(acts, axis=-1), wd_ref[...], preferred_element_type=F32)
        o_ref[pl.ds(r0, rows), :] = (o_ref[pl.ds(r0, rows), :].astype(F32) + y).astype(o_ref.dtype)

    def tile_body(tl, carry):
        idx = (blk * tiles + tl) * N_GROUPS + grp
        base = pl.multiple_of(tl * SORT_SLOTS + start_ref[idx], SORT_ALIGN)
        n = count_ref[idx]

        @pl.when((n > 0) & (n <= MOE_SMALL_CHUNK))
        def _():
            run_rows(base, MOE_SMALL_CHUNK)

        @pl.when(n > MOE_SMALL_CHUNK)
        def _():
            def chunk_body(c, carry2):
                run_rows(pl.multiple_of(base + c * chunk, SORT_ALIGN), chunk)
                return carry2

            lax.fori_loop(0, (n + chunk - 1) // chunk, chunk_body, 0)

        return carry

    lax.fori_loop(0, tiles, tile_body, 0)


def _moe(hms, starts, counts, w_gu, w_d, tiles, group, chunk):
    n_rows, dx = hms.shape
    n_exp, d, ff2 = w_gu.shape
    ff = ff2 // 2
    rows = tiles * SORT_SLOTS
    grid_spec = pltpu.PrefetchScalarGridSpec(
        num_scalar_prefetch=2,
        grid=(n_rows // rows, n_exp // group),
        in_specs=[pl.BlockSpec((rows, dx), lambda i, e, s, c: (i, 0)),
                  pl.BlockSpec((group, d, ff2), lambda i, e, s, c: (e, 0, 0)),
                  pl.BlockSpec((group * ff, d), lambda i, e, s, c: (e, 0))],
        out_specs=pl.BlockSpec((rows, d), lambda i, e, s, c: (i, 0)),
    )
    return pl.pallas_call(
        functools.partial(_moe_kernel, group=group, tiles=tiles, chunk=chunk),
        grid_spec=grid_spec,
        out_shape=jax.ShapeDtypeStruct((n_rows, d), BF16),
        compiler_params=pltpu.CompilerParams(dimension_semantics=("parallel", "arbitrary"),
                                             vmem_limit_bytes=VMEM_LIMIT_BYTES),
        name="moe",
    )(starts, counts, hms, w_gu, w_d)


def _unsort_kernel(x2_ref, pos_ref, ys_ref, o_ref):
    t = x2_ref.shape[0]
    ts = ys_ref.shape[0]
    unperm = (lax.broadcasted_iota(jnp.int32, (t, ts), 1).astype(F32) == pos_ref[...]).astype(BF16)
    o_ref[...] = x2_ref[...] + jnp.dot(unperm, ys_ref[...], preferred_element_type=F32)


def _unsort(x2, pos, ys, tile):
    n_tok, d = x2.shape
    tok = lambda i: (i, 0)
    return pl.pallas_call(
        _unsort_kernel,
        grid=(n_tok // tile,),
        in_specs=[pl.BlockSpec((tile, d), tok), pl.BlockSpec((tile, 1), tok), pl.BlockSpec((SORT_SLOTS, d), tok)],
        out_specs=pl.BlockSpec((tile, d), tok),
        out_shape=jax.ShapeDtypeStruct((n_tok, d), F32),
        compiler_params=pltpu.CompilerParams(dimension_semantics=("parallel",)),
        name="unsort",
    )(x2, pos, ys)


def _row(v):
    return v.reshape(1, -1).astype(F32)


def _pad_lanes(v, width=LANES):
    return jnp.pad(v, [(0, 0)] * (v.ndim - 1) + [(0, width - v.shape[-1])])


def _swap_halves(v):
    half = v.shape[-1] // 2
    return jnp.concatenate([v[..., half:], v[..., :half]], axis=-1)


def _layout_params(l, attn_norm_g, w_in, mla_q_norm_g, mla_w_uq, mla_kv_norm_g, mla_w_ukv, mla_q_qk_g,
                   mla_k_qk_g, ret_gn_g, w_out, cross_norm_g, mem_norm_g, cross_w_q, cross_w_kv, cross_q_qk_g,
                   cross_k_qk_g, cross_w_o, moe_norm_g, router_w_group, router_b_group, router_w_expert,
                   router_b_expert):
    p = {}
    d = w_in.shape[1]
    wi = w_in[l]
    c_q, c_kv, k_pe, rq, rk, rv, rg = jnp.split(
        wi, np.cumsum([MLA_Q_RANK, MLA_KV_RANK, MLA_ROPE_DIM, RET_WIDTH, RET_WIDTH, RET_WIDTH]).tolist(), axis=1)
    p['w_in'] = jnp.concatenate([c_q, c_kv, rq, rk, rv, rg, _pad_lanes(k_pe), _pad_lanes(_swap_halves(k_pe))],
                                axis=1).astype(BF16)
    wq = mla_w_uq[l].reshape(MLA_Q_RANK, MLA_HEADS, MLA_QK_DIM)
    wq_rope = wq[:, :, MLA_NOPE_DIM:]
    p['w_uq'] = jnp.concatenate([wq[:, :, :MLA_NOPE_DIM], _pad_lanes(wq_rope), _pad_lanes(_swap_halves(wq_rope))],
                                axis=2).reshape(MLA_Q_RANK, MLA_HEADS * Q_HEAD_COLS).astype(BF16)
    wkv = mla_w_ukv[l].reshape(MLA_KV_RANK, MLA_HEADS, MLA_NOPE_DIM + MLA_V_DIM)
    p['w_ukv'] = jnp.concatenate([wkv[:, :, :MLA_NOPE_DIM].reshape(MLA_KV_RANK, -1),
                                  wkv[:, :, MLA_NOPE_DIM:].reshape(MLA_KV_RANK, -1)], axis=1).astype(BF16)
    p['attn_norm_g'] = _row(attn_norm_g[l])
    p['q_norm_g'] = _row(mla_q_norm_g[l])
    p['kv_norm_g'] = _row(mla_kv_norm_g[l])
    for name, g in (('q', mla_q_qk_g[l]), ('k', mla_k_qk_g[l])):
        p[name + '_g_nope'] = _row(g[:MLA_NOPE_DIM])
        p[name + '_g_rope'] = _row(_pad_lanes(g[MLA_NOPE_DIM:]))
        p[name + '_g_rope_sw'] = _row(_pad_lanes(_swap_halves(g[MLA_NOPE_DIM:])))
    hm_ = MLA_ROPE_DIM // 2
    f_m = ROPE_BASE ** (-jnp.arange(hm_, dtype=F32) / hm_)
    p['freq_m'] = _row(_pad_lanes(jnp.concatenate([f_m, f_m])))
    p['sign_m'] = _row(_pad_lanes(jnp.concatenate([-jnp.ones(hm_, F32), jnp.ones(hm_, F32)])))
    hr_ = RET_HEAD_DIM // 2
    f_r = ROPE_BASE ** (-jnp.arange(hr_, dtype=F32) / hr_)
    p['freq_r'] = _row(jnp.concatenate([f_r, f_r]))
    p['sign_r'] = _row(jnp.concatenate([-jnp.ones(hr_, F32), jnp.ones(hr_, F32)]))
    p['ret_gn_g'] = _row(ret_gn_g[l])
    half = MLA_HEADS * MLA_V_DIM
    p['w_out_a'] = w_out[l][:half].astype(BF16)
    p['w_out_r'] = w_out[l][half:].astype(BF16)
    p['cross_norm_g'] = _row(cross_norm_g[l])
    p['mem_norm_g'] = _row(mem_norm_g[l])
    p['cross_w_q'] = cross_w_q[l].astype(BF16)
    p['cross_w_kv'] = cross_w_kv[l].astype(BF16)
    p['cross_q_g'] = _row(cross_q_qk_g[l])
    p['cross_k_g'] = _row(cross_k_qk_g[l])
    p['cross_w_o'] = cross_w_o[l].astype(BF16)
    p['moe_norm_g'] = _row(moe_norm_g[l])
    w_r = _pad_lanes(jnp.concatenate([router_w_expert[l], router_w_group[l]], axis=1)).astype(F32)
    w_r_hi = w_r.astype(BF16)
    w_r_lo = (w_r - w_r_hi.astype(F32)).astype(BF16)
    p['w_router'] = jnp.concatenate([w_r_hi, w_r_lo], axis=1)
    p['b_router'] = _row(_pad_lanes(jnp.concatenate([router_b_expert[l], router_b_group[l]])))
    return p


PREP_TILE = 1024
PREP_CHAINS = 1
FLASH_TQ = 1024
FLASH_TK = 1024
FLASH_HEADS_PER_STEP = 4
SORT_TILE = 512
MIX_CHAINS = 2
MIX_TILE = SORT_TILE * MIX_CHAINS
SORT_ALIGN = 16
MOE_CHUNK = 144
MOE_SMALL_CHUNK = 128
SORT_SLOTS = -(-(SORT_TILE + N_GROUPS * (SORT_ALIGN - 1) + MOE_CHUNK) // LANES) * LANES
MOE_TILES_PER_BLOCK = 4
MOE_EXPERTS_PER_STEP = 8


def kernel(x, mem, positions, attn_norm_g, w_in, mla_q_norm_g, mla_w_uq, mla_kv_norm_g, mla_w_ukv, mla_q_qk_g, mla_k_qk_g, ret_gn_g, w_out, cross_norm_g, mem_norm_g, cross_w_q, cross_w_kv, cross_q_qk_g, cross_k_qk_g, cross_w_o, moe_norm_g, router_w_group, router_b_group, router_w_expert, router_b_expert, expert_w_gate, expert_w_up, expert_w_down):
    batch, seq, d = x.shape
    depth = w_in.shape[0]
    pos2d = positions.reshape(batch * seq, 1).astype(jnp.int32)
    x2d = x.reshape(batch * seq, d)
    for l in range(depth):
        p = _layout_params(l, attn_norm_g, w_in, mla_q_norm_g, mla_w_uq, mla_kv_norm_g, mla_w_ukv, mla_q_qk_g,
                           mla_k_qk_g, ret_gn_g, w_out, cross_norm_g, mem_norm_g, cross_w_q, cross_w_kv,
                           cross_q_qk_g, cross_k_qk_g, cross_w_o, moe_norm_g, router_w_group, router_b_group,
                           router_w_expert, router_b_expert)
        w_gu = jnp.concatenate([expert_w_gate[l], expert_w_up[l]], axis=-1).astype(BF16)
        w_d = expert_w_down[l].reshape(-1, d).astype(BF16)

        q, k, v, rq, rk, rv, rg = _prep(x2d, pos2d, p, batch, seq, PREP_TILE)
        a_out = _flash(q, k, v, FLASH_TQ, FLASH_TK, FLASH_HEADS_PER_STEP)
        shp = (batch, seq, RET_WIDTH)
        r_out = _retention(rq.reshape(shp), rk.reshape(shp), rv.reshape(shp), rg.reshape(shp), p['ret_gn_g'])
        kc, vc = _memkv(mem, p['mem_norm_g'], p['cross_w_kv'], p['cross_k_g'])
        x2, hms, pos, meta = _mix(x2d, a_out.reshape(batch * seq, -1), r_out.reshape(batch * seq, -1), kc, vc, p,
                                  batch, seq, MIX_TILE)
        starts = meta[:, 0, :N_GROUPS].reshape(-1)
        counts = meta[:, 1, :N_GROUPS].reshape(-1)
        ys = _moe(hms, starts, counts, w_gu, w_d, MOE_TILES_PER_BLOCK, MOE_EXPERTS_PER_STEP, MOE_CHUNK)
        x2d = _unsort(x2, pos, ys, SORT_TILE)
    return x2d.reshape(batch, seq, d)
```

```python
import functools
import math

import numpy as np
import jax
import jax.numpy as jnp
from jax import lax
from jax.experimental import pallas as pl
from jax.experimental.pallas import tpu as pltpu

F32 = jnp.float32
BF16 = jnp.bfloat16

LANES = 128
VMEM_LIMIT_BYTES = 56 * 1024 * 1024

MLA_HEADS = 4
MLA_NOPE_DIM = 128
MLA_ROPE_DIM = 64
MLA_QK_DIM = MLA_NOPE_DIM + MLA_ROPE_DIM
MLA_V_DIM = 128
MLA_Q_RANK = 256
MLA_KV_RANK = 128
MLA_HEAD_PAD = 256
RET_HEADS = 4
RET_HEAD_DIM = 128
RET_WIDTH = RET_HEADS * RET_HEAD_DIM
RET_ROWS_PER_STEP = 2
RET_CHUNK = 256
CROSS_HEADS = 4
N_GROUPS = 4
EXPERTS_PER_GROUP = 8
N_EXPERTS = 32
ROPE_BASE = 10000.0
EPS = 1e-6
NEG_INF = -1e30

_C_Q = 0
_C_KV = _C_Q + MLA_Q_RANK
_RQ = _C_KV + MLA_KV_RANK
_RK = _RQ + RET_WIDTH
_RV = _RK + RET_WIDTH
_RG = _RV + RET_WIDTH
_KPE = _RG + RET_WIDTH
_KPE_SW = _KPE + LANES
IN_COLS_PAD = _KPE_SW + LANES
Q_HEAD_COLS = 3 * LANES


def _rms(x, g):
    return x * lax.rsqrt(jnp.mean(x * x, axis=-1, keepdims=True) + EPS) * g


def _const_spec(shape):
    nd = len(shape)
    return pl.BlockSpec(shape, lambda *_: (0,) * nd, pipeline_mode=pl.Buffered(1))


def _prep_rows(rs, x_ref, pos_ref, g_ref, win_ref, qng_ref, wuq_ref, kvng_ref, wukv_ref,
               qgn_ref, qgr_ref, qgs_ref, kgn_ref, kgr_ref, kgs_ref,
               fm_ref, sm_ref, fr_ref, sr_ref,
               q_out, k_out, v_out, rq_out, rk_out, rv_out, rg_out):
    x = x_ref[rs, :]
    h = _rms(x, g_ref[...]).astype(BF16)
    proj = jnp.dot(h, win_ref[...], preferred_element_type=F32)

    posf = pos_ref[rs, :].astype(F32)
    ang_m = posf * fm_ref[...]
    cos_m = jnp.cos(ang_m)
    sin_m = jnp.sin(ang_m) * sm_ref[...]
    ang_r = posf * fr_ref[...]
    cos_r = jnp.cos(ang_r)
    sin_r = jnp.sin(ang_r) * sr_ref[...]

    c_q = proj[:, _C_Q:_C_Q + MLA_Q_RANK]
    qall = jnp.dot(_rms(c_q, qng_ref[...]).astype(BF16), wuq_ref[...], preferred_element_type=F32)
    q_scale = MLA_QK_DIM ** -0.5 * math.log2(math.e)
    for hd in range(MLA_HEADS):
        base = hd * Q_HEAD_COLS
        nope = qall[:, base:base + LANES]
        rope = qall[:, base + LANES:base + 2 * LANES]
        rope_sw = qall[:, base + 2 * LANES:base + 3 * LANES]
        ss = jnp.sum(nope * nope, axis=-1, keepdims=True) + jnp.sum(rope * rope, axis=-1, keepdims=True)
        r = lax.rsqrt(ss * (1.0 / MLA_QK_DIM) + EPS) * q_scale
        roped = rope * qgr_ref[...] * cos_m + rope_sw * qgs_ref[...] * sin_m
        q_out[0, hd, rs, 0:LANES] = (nope * qgn_ref[...] * r).astype(BF16)
        q_out[0, hd, rs, LANES:2 * LANES] = (roped * r).astype(BF16)

    c_kv = proj[:, _C_KV:_C_KV + MLA_KV_RANK]
    kv = jnp.dot(_rms(c_kv, kvng_ref[...]).astype(BF16), wukv_ref[...], preferred_element_type=F32)
    k_pe = proj[:, _KPE:_KPE + LANES]
    k_pe_sw = proj[:, _KPE_SW:_KPE_SW + LANES]
    kr = k_pe * kgr_ref[...] * cos_m + k_pe_sw * kgs_ref[...] * sin_m
    ss_pe = jnp.sum(k_pe * k_pe, axis=-1, keepdims=True)
    for hd in range(MLA_HEADS):
        k_nope = kv[:, hd * LANES:(hd + 1) * LANES]
        ss = jnp.sum(k_nope * k_nope, axis=-1, keepdims=True) + ss_pe
        r = lax.rsqrt(ss * (1.0 / MLA_QK_DIM) + EPS)
        k_out[0, hd, rs, 0:LANES] = (k_nope * kgn_ref[...] * r).astype(BF16)
        k_out[0, hd, rs, LANES:2 * LANES] = (kr * r).astype(BF16)
        v_out[0, hd, rs, :] = kv[:, (MLA_HEADS + hd) * LANES:(MLA_HEADS + hd + 1) * LANES].astype(BF16)

    k_scale = RET_HEAD_DIM ** -0.5
    for hd in range(RET_HEADS):
        sl = slice(hd * LANES, (hd + 1) * LANES)
        rq = proj[:, _RQ + hd * LANES:_RQ + (hd + 1) * LANES]
        rk = proj[:, _RK + hd * LANES:_RK + (hd + 1) * LANES]
        rq_out[rs, sl] = (rq * cos_r + pltpu.roll(rq, LANES // 2, axis=1) * sin_r).astype(BF16)
        rk_out[rs, sl] = ((rk * cos_r + pltpu.roll(rk, LANES // 2, axis=1) * sin_r) * k_scale).astype(BF16)
    rv_out[rs, :] = proj[:, _RV:_RV + RET_WIDTH].astype(BF16)
    rg_out[rs, :] = proj[:, _RG:_RG + RET_WIDTH].astype(BF16)


def _prep_kernel(x_ref, *refs, chains):
    rows = x_ref.shape[0] // chains
    for c in range(chains):
        _prep_rows(slice(c * rows, (c + 1) * rows), x_ref, *refs)


def _prep(x2d, pos2d, p, batch, seq, tile):
    n_tok, d = x2d.shape
    tiles_per_b = seq // tile
    tok = lambda i: (i, 0)
    head = lambda i: (i // tiles_per_b, 0, i % tiles_per_b, 0)
    consts = [p['attn_norm_g'], p['w_in'], p['q_norm_g'], p['w_uq'], p['kv_norm_g'], p['w_ukv'],
              p['q_g_nope'], p['q_g_rope'], p['q_g_rope_sw'], p['k_g_nope'], p['k_g_rope'], p['k_g_rope_sw'],
              p['freq_m'], p['sign_m'], p['freq_r'], p['sign_r']]
    out_shape = (
        jax.ShapeDtypeStruct((batch, MLA_HEADS, seq, MLA_HEAD_PAD), BF16),
        jax.ShapeDtypeStruct((batch, MLA_HEADS, seq, MLA_HEAD_PAD), BF16),
        jax.ShapeDtypeStruct((batch, MLA_HEADS, seq, MLA_V_DIM), BF16),
        jax.ShapeDtypeStruct((n_tok, RET_WIDTH), BF16),
        jax.ShapeDtypeStruct((n_tok, RET_WIDTH), BF16),
        jax.ShapeDtypeStruct((n_tok, RET_WIDTH), BF16),
        jax.ShapeDtypeStruct((n_tok, RET_WIDTH), BF16),
    )
    return pl.pallas_call(
        functools.partial(_prep_kernel, chains=PREP_CHAINS),
        grid=(n_tok // tile,),
        in_specs=[pl.BlockSpec((tile, d), tok), pl.BlockSpec((tile, 1), tok)]
                 + [_const_spec(c.shape) for c in consts],
        out_specs=(
            pl.BlockSpec((1, MLA_HEADS, tile, MLA_HEAD_PAD), head),
            pl.BlockSpec((1, MLA_HEADS, tile, MLA_HEAD_PAD), head),
            pl.BlockSpec((1, MLA_HEADS, tile, MLA_V_DIM), head),
            pl.BlockSpec((tile, RET_WIDTH), tok),
            pl.BlockSpec((tile, RET_WIDTH), tok),
            pl.BlockSpec((tile, RET_WIDTH), tok),
            pl.BlockSpec((tile, RET_WIDTH), tok),
        ),
        out_shape=out_shape,
        compiler_params=pltpu.CompilerParams(dimension_semantics=("parallel",),
                                             vmem_limit_bytes=VMEM_LIMIT_BYTES),
        name="prep",
    )(x2d, pos2d, *consts)


def _flash_kernel(q_ref, k_ref, v_ref, o_ref, m_sc, l_sc, acc_sc, *, tq, tk, hb):
    qi = pl.program_id(2)
    m_sc[...] = jnp.full(m_sc.shape, NEG_INF, F32)
    l_sc[...] = jnp.zeros(l_sc.shape, F32)
    acc_sc[...] = jnp.zeros(acc_sc.shape, F32)

    def step(off, width, rows, mask):
        reps = width // LANES
        for hd in range(hb):
            k = k_ref[0, hd, pl.ds(off, width), :]
            v = v_ref[0, hd, pl.ds(off, width), :]
            s = lax.dot_general(q_ref[0, hd, rows, :], k, (((1,), (1,)), ((), ())), preferred_element_type=F32)
            if mask is not None:
                s = jnp.where(mask, s, NEG_INF)
            m_prev = m_sc[hd, rows, :]
            m_new = jnp.maximum(m_prev, jnp.max(s, axis=-1, keepdims=True))
            alpha = jnp.exp2(m_prev - m_new)
            p = jnp.exp2(s - jnp.concatenate([m_new] * reps, axis=1))
            p_cols = p[:, 0:LANES]
            for c in range(1, reps):
                p_cols = p_cols + p[:, c * LANES:(c + 1) * LANES]
            l_sc[hd, rows, :] = alpha * l_sc[hd, rows, :] + p_cols
            acc_sc[hd, rows, :] = alpha * acc_sc[hd, rows, :] + jnp.dot(p.astype(BF16), v,
                                                                        preferred_element_type=F32)
            m_sc[hd, rows, :] = m_new

    def body(j, c):
        step(pl.multiple_of(j * tk, tk), tk, slice(0, tq), None)
        return c

    lax.fori_loop(0, qi, body, 0)
    half = tk // 2
    base = pl.multiple_of(qi * tk, tk)

    def causal(n_rows):
        return (lax.broadcasted_iota(jnp.int32, (n_rows, half), 1)
                <= lax.broadcasted_iota(jnp.int32, (n_rows, half), 0))

    step(base, half, slice(0, tq), causal(tq))
    step(base + half, half, slice(half, tq), causal(tq - half))
    for hd in range(hb):
        l = jnp.sum(l_sc[hd], axis=-1, keepdims=True)
        o_ref[0, :, hd * LANES:(hd + 1) * LANES] = (acc_sc[hd] / l).astype(o_ref.dtype)


def _flash(q, k, v, tq, tk, hb):
    batch, heads, seq, dqk = q.shape
    dv = v.shape[-1]
    assert tq == tk and seq % tq == 0 and heads % hb == 0
    return pl.pallas_call(
        functools.partial(_flash_kernel, tq=tq, tk=tk, hb=hb),
        grid=(batch, heads // hb, seq // tq),
        in_specs=[
            pl.BlockSpec((1, hb, tq, dqk), lambda b, h, i: (b, h, i, 0)),
            pl.BlockSpec((1, hb, seq, dqk), lambda b, h, i: (b, h, 0, 0)),
            pl.BlockSpec((1, hb, seq, dv), lambda b, h, i: (b, h, 0, 0)),
        ],
        out_specs=pl.BlockSpec((1, tq, hb * dv), lambda b, h, i: (b, i, h)),
        out_shape=jax.ShapeDtypeStruct((batch, seq, heads * dv), BF16),
        scratch_shapes=[pltpu.VMEM((hb, tq, LANES), F32), pltpu.VMEM((hb, tq, LANES), F32),
                        pltpu.VMEM((hb, tq, dv), F32)],
        compiler_params=pltpu.CompilerParams(dimension_semantics=("parallel", "parallel", "arbitrary"),
                                             vmem_limit_bytes=VMEM_LIMIT_BYTES),
        name="flash",
    )(q, k, v)


def _ret_kernel(q_ref, k_ref, v_ref, g_ref, gn_ref, o_ref, state_sc, intra_sc):
    c = pl.program_id(1)
    n = RET_CHUNK
    log_decay = [math.log(1.0 - 2.0 ** (-5.0 - hd)) for hd in range(RET_HEADS)]

    @pl.when(c == 0)
    def _():
        state_sc[...] = jnp.zeros(state_sc.shape, F32)
        diff = (lax.broadcasted_iota(jnp.int32, (n, n), 0) - lax.broadcasted_iota(jnp.int32, (n, n), 1)).astype(F32)
        for hd in range(RET_HEADS):
            intra_sc[hd] = jnp.where(diff >= 0, jnp.exp(jnp.maximum(diff, 0.0) * log_decay[hd]), 0.0)

    idx = lax.broadcasted_iota(jnp.int32, (n, 1), 0).astype(F32)
    for bb, hd in [(bb, hd) for bb in range(q_ref.shape[0]) for hd in range(RET_HEADS)]:
        lg = log_decay[hd]
        intra = intra_sc[hd]
        q_decay = jnp.exp((idx + 1.0) * lg)
        k_decay = jnp.exp((n - 1.0 - idx) * lg)
        chunk_decay = math.exp(n * lg)
        sl = slice(hd * LANES, (hd + 1) * LANES)
        qc = q_ref[bb, :, sl]
        kc = k_ref[bb, :, sl]
        vc = v_ref[bb, :, sl]
        state = state_sc[bb * RET_HEADS + hd]
        scores = lax.dot_general(qc, kc, (((1,), (1,)), ((), ())), preferred_element_type=F32) * intra
        o = (jnp.dot(scores.astype(BF16), vc, preferred_element_type=F32)
             + jnp.dot((qc.astype(F32) * q_decay).astype(BF16), state.astype(BF16), preferred_element_type=F32))
        kd_t = (kc.astype(F32) * k_decay).T.astype(BF16)
        state_sc[bb * RET_HEADS + hd] = state * chunk_decay + jnp.dot(kd_t, vc, preferred_element_type=F32)
        mu = jnp.mean(o, axis=-1, keepdims=True)
        cen = o - mu
        var = jnp.mean(cen * cen, axis=-1, keepdims=True)
        gate = g_ref[bb, :, sl].astype(F32)
        y = cen * lax.rsqrt(var + EPS) * gn_ref[:, sl] * (gate * jax.nn.sigmoid(gate))
        o_ref[bb, :, sl] = y.astype(o_ref.dtype)


def _retention(rq, rk, rv, rg, gn_g):
    batch, seq, width = rq.shape
    blk = pl.BlockSpec((RET_ROWS_PER_STEP, RET_CHUNK, width), lambda b, c: (b, c, 0))
    return pl.pallas_call(
        _ret_kernel,
        grid=(batch // RET_ROWS_PER_STEP, seq // RET_CHUNK),
        in_specs=[blk, blk, blk, blk, _const_spec(gn_g.shape)],
        out_specs=blk,
        out_shape=jax.ShapeDtypeStruct((batch, seq, width), BF16),
        scratch_shapes=[pltpu.VMEM((RET_ROWS_PER_STEP * RET_HEADS, RET_HEAD_DIM, RET_HEAD_DIM), F32),
                        pltpu.VMEM((RET_HEADS, RET_CHUNK, RET_CHUNK), F32)],
        compiler_params=pltpu.CompilerParams(dimension_semantics=("parallel", "arbitrary")),
        name="retention",
    )(rq, rk, rv, rg, gn_g)


def _memkv_kernel(mem_ref, g_ref, wkv_ref, kg_ref, k_out, v_out):
    d = mem_ref.shape[-1]
    hdim = d // CROSS_HEADS
    m = _rms(mem_ref[0], g_ref[...]).astype(BF16)
    kv = jnp.dot(m, wkv_ref[...], preferred_element_type=F32)
    for hd in range(CROSS_HEADS):
        sl = slice(hd * hdim, (hd + 1) * hdim)
        k_out[0, :, sl] = _rms(kv[:, sl], kg_ref[...]).astype(BF16)
    v_out[0] = kv[:, d:].astype(BF16)


def _memkv(mem, mem_norm_g, w_kv, k_g):
    batch, m_tok, d = mem.shape
    blk = pl.BlockSpec((1, m_tok, d), lambda b: (b, 0, 0))
    return pl.pallas_call(
        _memkv_kernel,
        grid=(batch,),
        in_specs=[blk, _const_spec(mem_norm_g.shape), _const_spec(w_kv.shape), _const_spec(k_g.shape)],
        out_specs=(blk, blk),
        out_shape=(jax.ShapeDtypeStruct((batch, m_tok, d), BF16), jax.ShapeDtypeStruct((batch, m_tok, d), BF16)),
        compiler_params=pltpu.CompilerParams(dimension_semantics=("parallel",),
                                             vmem_limit_bytes=VMEM_LIMIT_BYTES),
        name="memkv",
    )(mem, mem_norm_g, w_kv, k_g)


def _mix_rows(x, a, r, kc_ref, vc_ref, woa_ref, wor_ref, cng_ref, wq_ref, qg_ref, wo_ref, mng_ref, wr_ref,
              br_ref):
    d = x.shape[-1]
    hdim = d // CROSS_HEADS
    x1 = (x
          + jnp.dot(a, woa_ref[...], preferred_element_type=F32)
          + jnp.dot(r, wor_ref[...], preferred_element_type=F32))
    hc = _rms(x1, cng_ref[...]).astype(BF16)
    q = jnp.dot(hc, wq_ref[...], preferred_element_type=F32)
    scale = hdim ** -0.5
    heads = []
    for hd in range(CROSS_HEADS):
        sl = slice(hd * hdim, (hd + 1) * hdim)
        qh = (_rms(q[:, sl], qg_ref[...]) * scale).astype(BF16)
        s = lax.dot_general(qh, kc_ref[0, :, sl], (((1,), (1,)), ((), ())), preferred_element_type=F32)
        s = s - jnp.max(s, axis=-1, keepdims=True)
        e = jnp.exp(s)
        pr = e / jnp.sum(e, axis=-1, keepdims=True)
        heads.append(jnp.dot(pr.astype(BF16), vc_ref[0, :, sl], preferred_element_type=F32).astype(BF16))
    o = jnp.concatenate(heads, axis=-1)
    x2 = x1 + jnp.dot(o, wo_ref[...], preferred_element_type=F32)
    hm = _rms(x2, mng_ref[...])

    hm_hi = hm.astype(BF16)
    hm_lo = (hm - hm_hi.astype(F32)).astype(BF16)
    hi_hl = jnp.dot(hm_hi, wr_ref[...], preferred_element_type=F32)
    logits = (hi_hl[:, :LANES] + hi_hl[:, LANES:]
              + jnp.dot(hm_lo, wr_ref[:, :LANES], preferred_element_type=F32))
    biased = logits + br_ref[...]
    lane = lax.broadcasted_iota(jnp.int32, logits.shape, 1)
    lane_f = lane.astype(F32)
    big = float(4 * LANES)
    is_g = (lane >= N_EXPERTS) & (lane < N_EXPERTS + N_GROUPS)
    gl = jnp.where(is_g, logits, NEG_INF)
    gexp = jnp.where(is_g, jnp.exp(gl - jnp.max(gl, axis=-1, keepdims=True)), 0.0)
    gb = jnp.where(is_g, biased, NEG_INF)
    g_lane = jnp.min(jnp.where(gb == jnp.max(gb, axis=-1, keepdims=True), lane_f, big), axis=-1, keepdims=True)
    g_w = (jnp.sum(jnp.where(lane_f == g_lane, gexp, 0.0), axis=-1, keepdims=True)
           / jnp.sum(gexp, axis=-1, keepdims=True))
    g_idx = g_lane.astype(jnp.int32) - N_EXPERTS
    in_grp = (lane < N_EXPERTS) & ((lane // EXPERTS_PER_GROUP) == g_idx)
    el = jnp.where(in_grp, logits, NEG_INF)
    eexp = jnp.where(in_grp, jnp.exp(el - jnp.max(el, axis=-1, keepdims=True)), 0.0)
    eb = jnp.where(in_grp, biased, NEG_INF)
    i1 = jnp.min(jnp.where(eb == jnp.max(eb, axis=-1, keepdims=True), lane_f, big), axis=-1, keepdims=True)
    eb2 = jnp.where(lane_f == i1, NEG_INF, eb)
    i2 = jnp.min(jnp.where(eb2 == jnp.max(eb2, axis=-1, keepdims=True), lane_f, big), axis=-1, keepdims=True)
    psel = jnp.where((lane_f == i1) | (lane_f == i2), eexp, 0.0)
    gate = psel / jnp.sum(psel, axis=-1, keepdims=True) * g_w
    grp_onehot = (lane == g_idx).astype(F32)
    return x2, hm.astype(BF16), gate, grp_onehot


def _group_sort(hm, gate, grp, hms_out, pos_out, meta_out, c):
    t = hm.shape[0]
    ts = SORT_SLOTS
    earlier = (lax.broadcasted_iota(jnp.int32, (t, t), 1) < lax.broadcasted_iota(jnp.int32, (t, t), 0))
    cnt = jnp.dot(earlier.astype(BF16), grp.astype(BF16), preferred_element_type=F32)
    n = jnp.sum(grp, axis=0, keepdims=True)
    n_al = jnp.floor((n + (SORT_ALIGN - 1.0)) * (1.0 / SORT_ALIGN)) * SORT_ALIGN
    lane1 = lax.broadcasted_iota(jnp.int32, (1, LANES), 1)
    start = jnp.zeros((1, LANES), F32)
    run = jnp.zeros((1, 1), F32)
    for g in range(1, N_GROUPS):
        run = run + jnp.sum(jnp.where(lane1 == g - 1, n_al, 0.0), axis=-1, keepdims=True)
        start = jnp.where(lane1 == g, run, start)
    pos = jnp.sum(grp * (cnt + start), axis=-1, keepdims=True)
    pos_out[c * t:(c + 1) * t, :] = pos
    row8 = lax.broadcasted_iota(jnp.int32, (8, LANES), 0)
    meta_out[c] = jnp.where(row8 == 0, start, jnp.where(row8 == 1, n, 0.0)).astype(jnp.int32)
    pos_row = jnp.broadcast_to(pos, (t, LANES)).T[0:1, :]
    perm = (lax.broadcasted_iota(jnp.int32, (ts, t), 0).astype(F32) == pos_row).astype(BF16)
    g_hi = gate.astype(BF16).astype(F32)
    rem = gate - g_hi
    g_mid = rem.astype(BF16).astype(F32)
    g_lo = rem - g_mid
    gate_pieces = (g_hi + pltpu.roll(g_mid, N_EXPERTS, axis=1) + pltpu.roll(g_lo, 2 * N_EXPERTS, axis=1)).astype(BF16)
    hms_out[c * ts:(c + 1) * ts, :] = jnp.dot(perm, jnp.concatenate([hm, gate_pieces], axis=-1),
                                              preferred_element_type=F32).astype(BF16)


def _mix_kernel(x_ref, a_ref, r_ref, *rest, chains):
    *w_refs, x2_out, hms_out, pos_out, meta_out = rest
    rows = x_ref.shape[0] // chains
    for c in range(chains):
        sl = slice(c * rows, (c + 1) * rows)
        x2, hm, gate, grp = _mix_rows(x_ref[sl, :], a_ref[sl, :], r_ref[sl, :], *w_refs)
        x2_out[sl, :] = x2
        _group_sort(hm, gate, grp, hms_out, pos_out, meta_out, c)


def _mix(x2d, a2d, r2d, kc, vc, p, batch, seq, tile):
    n_tok, d = x2d.shape
    half = a2d.shape[-1]
    tiles_per_b = seq // tile
    n_tiles = n_tok // tile
    tok = lambda i: (i, 0)
    mem_blk = pl.BlockSpec((1,) + kc.shape[1:], lambda i: (i // tiles_per_b, 0, 0))
    consts = [p['w_out_a'], p['w_out_r'], p['cross_norm_g'], p['cross_w_q'], p['cross_q_g'], p['cross_w_o'],
              p['moe_norm_g'], p['w_router'], p['b_router']]
    return pl.pallas_call(
        functools.partial(_mix_kernel, chains=MIX_CHAINS),
        grid=(n_tiles,),
        in_specs=[pl.BlockSpec((tile, d), tok), pl.BlockSpec((tile, half), tok), pl.BlockSpec((tile, half), tok),
                  mem_blk, mem_blk] + [_const_spec(c.shape) for c in consts],
        out_specs=(pl.BlockSpec((tile, d), tok), pl.BlockSpec((MIX_CHAINS * SORT_SLOTS, d + LANES), tok),
                   pl.BlockSpec((tile, 1), tok), pl.BlockSpec((MIX_CHAINS, 8, LANES), lambda i: (i, 0, 0))),
        out_shape=(jax.ShapeDtypeStruct((n_tok, d), F32),
                   jax.ShapeDtypeStruct((n_tiles * MIX_CHAINS * SORT_SLOTS, d + LANES), BF16),
                   jax.ShapeDtypeStruct((n_tok, 1), F32),
                   jax.ShapeDtypeStruct((n_tiles * MIX_CHAINS, 8, LANES), jnp.int32)),
        compiler_params=pltpu.CompilerParams(dimension_semantics=("parallel",),
                                             vmem_limit_bytes=VMEM_LIMIT_BYTES),
        name="mix",
    )(x2d, a2d, r2d, kc, vc, *consts)


def _moe_kernel(start_ref, count_ref, hms_ref, wgu_ref, wd_ref, o_ref, *, group, tiles, chunk):
    blk = pl.program_id(0)
    step = pl.program_id(1)
    d = wd_ref.shape[1]
    ff = wgu_ref.shape[2] // 2
    grp = step // (EXPERTS_PER_GROUP // group)

    @pl.when(step == 0)
    def _():
        o_ref[...] = jnp.zeros(o_ref.shape, o_ref.dtype)

    def run_rows(r0, rows):
        lane = lax.broadcasted_iota(jnp.int32, (rows, LANES), 1)
        piece_lane = lane < 3 * N_EXPERTS
        xs = hms_ref[pl.ds(r0, rows), 0:d]
        gates = hms_ref[pl.ds(r0, rows), d:d + LANES].astype(F32)
        acts = []
        for j in range(group):
            e = step * group + j
            au = jnp.dot(xs, wgu_ref[j], preferred_element_type=F32)
            a = au[:, :ff]
            u = au[:, ff:]
            g_e = jnp.sum(jnp.where(piece_lane & ((lane & (N_EXPERTS - 1)) == e), gates, 0.0),
                          axis=-1, keepdims=True)
            acts.append((a * jax.nn.sigmoid(a) * u * g_e).astype(BF16))
        y = jnp.dot(jnp.concatenate(acts, axis=-1), wd_ref[...], preferred_element_type=F32)
        o_ref[pl.ds(r0, rows), :] = (o_ref[pl.ds(r0, rows), :].astype(F32) + y).astype(o_ref.dtype)

    def tile_body(tl, carry):
        idx = (blk * tiles + tl) * N_GROUPS + grp
        base = pl.multiple_of(tl * SORT_SLOTS + start_ref[idx], SORT_ALIGN)
        n = count_ref[idx]

        @pl.when((n > 0) & (n <= MOE_SMALL_CHUNK))
        def _():
            run_rows(base, MOE_SMALL_CHUNK)

        @pl.when(n > MOE_SMALL_CHUNK)
        def _():
            def chunk_body(c, carry2):
                run_rows(pl.multiple_of(base + c * chunk, SORT_ALIGN), chunk)
                return carry2

            lax.fori_loop(0, (n + chunk - 1) // chunk, chunk_body, 0)

        return carry

    lax.fori_loop(0, tiles, tile_body, 0)


def _moe(hms, starts, counts, w_gu, w_d, tiles, group, chunk):
    n_rows, dx = hms.shape
    n_exp, d, ff2 = w_gu.shape
    ff = ff2 // 2
    rows = tiles * SORT_SLOTS
    grid_spec = pltpu.PrefetchScalarGridSpec(
        num_scalar_prefetch=2,
        grid=(n_rows // rows, n_exp // group),
        in_specs=[pl.BlockSpec((rows, dx), lambda i, e, s, c: (i, 0)),
                  pl.BlockSpec((group, d, ff2), lambda i, e, s, c: (e, 0, 0)),
                  pl.BlockSpec((group * ff, d), lambda i, e, s, c: (e, 0))],
        out_specs=pl.BlockSpec((rows, d), lambda i, e, s, c: (i, 0)),
    )
    return pl.pallas_call(
        functools.partial(_moe_kernel, group=group, tiles=tiles, chunk=chunk),
        grid_spec=grid_spec,
        out_shape=jax.ShapeDtypeStruct((n_rows, d), BF16),
        compiler_params=pltpu.CompilerParams(dimension_semantics=("parallel", "arbitrary"),
                                             vmem_limit_bytes=VMEM_LIMIT_BYTES),
        name="moe",
    )(starts, counts, hms, w_gu, w_d)


def _unsort_kernel(x2_ref, pos_ref, ys_ref, o_ref):
    t = x2_ref.shape[0]
    ts = ys_ref.shape[0]
    unperm = (lax.broadcasted_iota(jnp.int32, (t, ts), 1).astype(F32) == pos_ref[...]).astype(BF16)
    o_ref[...] = x2_ref[...] + jnp.dot(unperm, ys_ref[...], preferred_element_type=F32)


def _unsort(x2, pos, ys, tile):
    n_tok, d = x2.shape
    tok = lambda i: (i, 0)
    return pl.pallas_call(
        _unsort_kernel,
        grid=(n_tok // tile,),
        in_specs=[pl.BlockSpec((tile, d), tok), pl.BlockSpec((tile, 1), tok), pl.BlockSpec((SORT_SLOTS, d), tok)],
        out_specs=pl.BlockSpec((tile, d), tok),
        out_shape=jax.ShapeDtypeStruct((n_tok, d), F32),
        compiler_params=pltpu.CompilerParams(dimension_semantics=("parallel",)),
        name="unsort",
    )(x2, pos, ys)


def _row(v):
    return v.reshape(1, -1).astype(F32)


def _pad_lanes(v, width=LANES):
    return jnp.pad(v, [(0, 0)] * (v.ndim - 1) + [(0, width - v.shape[-1])])


def _swap_halves(v):
    half = v.shape[-1] // 2
    return jnp.concatenate([v[..., half:], v[..., :half]], axis=-1)


def _layout_params(l, attn_norm_g, w_in, mla_q_norm_g, mla_w_uq, mla_kv_norm_g, mla_w_ukv, mla_q_qk_g,
                   mla_k_qk_g, ret_gn_g, w_out, cross_norm_g, mem_norm_g, cross_w_q, cross_w_kv, cross_q_qk_g,
                   cross_k_qk_g, cross_w_o, moe_norm_g, router_w_group, router_b_group, router_w_expert,
                   router_b_expert):
    p = {}
    d = w_in.shape[1]
    wi = w_in[l]
    c_q, c_kv, k_pe, rq, rk, rv, rg = jnp.split(
        wi, np.cumsum([MLA_Q_RANK, MLA_KV_RANK, MLA_ROPE_DIM, RET_WIDTH, RET_WIDTH, RET_WIDTH]).tolist(), axis=1)
    p['w_in'] = jnp.concatenate([c_q, c_kv, rq, rk, rv, rg, _pad_lanes(k_pe), _pad_lanes(_swap_halves(k_pe))],
                                axis=1).astype(BF16)
    wq = mla_w_uq[l].reshape(MLA_Q_RANK, MLA_HEADS, MLA_QK_DIM)
    wq_rope = wq[:, :, MLA_NOPE_DIM:]
    p['w_uq'] = jnp.concatenate([wq[:, :, :MLA_NOPE_DIM], _pad_lanes(wq_rope), _pad_lanes(_swap_halves(wq_rope))],
                                axis=2).reshape(MLA_Q_RANK, MLA_HEADS * Q_HEAD_COLS).astype(BF16)
    wkv = mla_w_ukv[l].reshape(MLA_KV_RANK, MLA_HEADS, MLA_NOPE_DIM + MLA_V_DIM)
    p['w_ukv'] = jnp.concatenate([wkv[:, :, :MLA_NOPE_DIM].reshape(MLA_KV_RANK, -1),
                                  wkv[:, :, MLA_NOPE_DIM:].reshape(MLA_KV_RANK, -1)], axis=1).astype(BF16)
    p['attn_norm_g'] = _row(attn_norm_g[l])
    p['q_norm_g'] = _row(mla_q_norm_g[l])
    p['kv_norm_g'] = _row(mla_kv_norm_g[l])
    for name, g in (('q', mla_q_qk_g[l]), ('k', mla_k_qk_g[l])):
        p[name + '_g_nope'] = _row(g[:MLA_NOPE_DIM])
        p[name + '_g_rope'] = _row(_pad_lanes(g[MLA_NOPE_DIM:]))
        p[name + '_g_rope_sw'] = _row(_pad_lanes(_swap_halves(g[MLA_NOPE_DIM:])))
    hm_ = MLA_ROPE_DIM // 2
    f_m = ROPE_BASE ** (-jnp.arange(hm_, dtype=F32) / hm_)
    p['freq_m'] = _row(_pad_lanes(jnp.concatenate([f_m, f_m])))
    p['sign_m'] = _row(_pad_lanes(jnp.concatenate([-jnp.ones(hm_, F32), jnp.ones(hm_, F32)])))
    hr_ = RET_HEAD_DIM // 2
    f_r = ROPE_BASE ** (-jnp.arange(hr_, dtype=F32) / hr_)
    p['freq_r'] = _row(jnp.concatenate([f_r, f_r]))
    p['sign_r'] = _row(jnp.concatenate([-jnp.ones(hr_, F32), jnp.ones(hr_, F32)]))
    p['ret_gn_g'] = _row(ret_gn_g[l])
    half = MLA_HEADS * MLA_V_DIM
    p['w_out_a'] = w_out[l][:half].astype(BF16)
    p['w_out_r'] = w_out[l][half:].astype(BF16)
    p['cross_norm_g'] = _row(cross_norm_g[l])
    p['mem_norm_g'] = _row(mem_norm_g[l])
    p['cross_w_q'] = cross_w_q[l].astype(BF16)
    p['cross_w_kv'] = cross_w_kv[l].astype(BF16)
    p['cross_q_g'] = _row(cross_q_qk_g[l])
    p['cross_k_g'] = _row(cross_k_qk_g[l])
    p['cross_w_o'] = cross_w_o[l].astype(BF16)
    p['moe_norm_g'] = _row(moe_norm_g[l])
    w_r = _pad_lanes(jnp.concatenate([router_w_expert[l], router_w_group[l]], axis=1)).astype(F32)
    w_r_hi = w_r.astype(BF16)
    w_r_lo = (w_r - w_r_hi.astype(F32)).astype(BF16)
    p['w_router'] = jnp.concatenate([w_r_hi, w_r_lo], axis=1)
    p['b_router'] = _row(_pad_lanes(jnp.concatenate([router_b_expert[l], router_b_group[l]])))
    return p


PREP_TILE = 1024
PREP_CHAINS = 1
FLASH_TQ = 1024
FLASH_TK = 1024
FLASH_HEADS_PER_STEP = 4
SORT_TILE = 512
MIX_CHAINS = 2
MIX_TILE = SORT_TILE * MIX_CHAINS
SORT_ALIGN = 16
MOE_CHUNK = 144
MOE_SMALL_CHUNK = 128
SORT_SLOTS = -(-(SORT_TILE + N_GROUPS * (SORT_ALIGN - 1) + MOE_CHUNK) // LANES) * LANES
MOE_TILES_PER_BLOCK = 4
MOE_EXPERTS_PER_STEP = 8


def kernel(x, mem, positions, attn_norm_g, w_in, mla_q_norm_g, mla_w_uq, mla_kv_norm_g, mla_w_ukv, mla_q_qk_g, mla_k_qk_g, ret_gn_g, w_out, cross_norm_g, mem_norm_g, cross_w_q, cross_w_kv, cross_q_qk_g, cross_k_qk_g, cross_w_o, moe_norm_g, router_w_group, router_b_group, router_w_expert, router_b_expert, expert_w_gate, expert_w_up, expert_w_down):
    batch, seq, d = x.shape
    depth = w_in.shape[0]
    pos2d = positions.reshape(batch * seq, 1).astype(jnp.int32)
    x2d = x.reshape(batch * seq, d)
    for l in range(depth):
        p = _layout_params(l, attn_norm_g, w_in, mla_q_norm_g, mla_w_uq, mla_kv_norm_g, mla_w_ukv, mla_q_qk_g,
                           mla_k_qk_g, ret_gn_g, w_out, cross_norm_g, mem_norm_g, cross_w_q, cross_w_kv,
                           cross_q_qk_g, cross_k_qk_g, cross_w_o, moe_norm_g, router_w_group, router_b_group,
                           router_w_expert, router_b_expert)
        w_gu = jnp.concatenate([expert_w_gate[l], expert_w_up[l]], axis=-1).astype(BF16)
        w_d = expert_w_down[l].reshape(-1, d).astype(BF16)

        q, k, v, rq, rk, rv, rg = _prep(x2d, pos2d, p, batch, seq, PREP_TILE)
        a_out = _flash(q, k, v, FLASH_TQ, FLASH_TK, FLASH_HEADS_PER_STEP)
        shp = (batch, seq, RET_WIDTH)
        r_out = _retention(rq.reshape(shp), rk.reshape(shp), rv.reshape(shp), rg.reshape(shp), p['ret_gn_g'])
        kc, vc = _memkv(mem, p['mem_norm_g'], p['cross_w_kv'], p['cross_k_g'])
        x2, hms, pos, meta = _mix(x2d, a_out.reshape(batch * seq, -1), r_out.reshape(batch * seq, -1), kc, vc, p,
                                  batch, seq, MIX_TILE)
        starts = meta[:, 0, :N_GROUPS].reshape(-1)
        counts = meta[:, 1, :N_GROUPS].reshape(-1)
        ys = _moe(hms, starts, counts, w_gu, w_d, MOE_TILES_PER_BLOCK, MOE_EXPERTS_PER_STEP, MOE_CHUNK)
        x2d = _unsort(x2, pos, ys, SORT_TILE)
    return x2d.reshape(batch, seq, d)
```

```python
import functools
import math

import numpy as np
import jax
import jax.numpy as jnp
from jax import lax
from jax.experimental import pallas as pl
from jax.experimental.pallas import tpu as pltpu

F32 = jnp.float32
BF16 = jnp.bfloat16

LANES = 128
VMEM_LIMIT_BYTES = 56 * 1024 * 1024

MLA_HEADS = 4
MLA_NOPE_DIM = 128
MLA_ROPE_DIM = 64
MLA_QK_DIM = MLA_NOPE_DIM + MLA_ROPE_DIM
MLA_V_DIM = 128
MLA_Q_RANK = 256
MLA_KV_RANK = 128
MLA_HEAD_PAD = 256
RET_HEADS = 4
RET_HEAD_DIM = 128
RET_WIDTH = RET_HEADS * RET_HEAD_DIM
RET_ROWS_PER_STEP = 4
RET_CHUNK = 256
CROSS_HEADS = 4
N_GROUPS = 4
EXPERTS_PER_GROUP = 8
N_EXPERTS = 32
ROPE_BASE = 10000.0
EPS = 1e-6
NEG_INF = -1e30

_C_Q = 0
_C_KV = _C_Q + MLA_Q_RANK
_RQ = _C_KV + MLA_KV_RANK
_RK = _RQ + RET_WIDTH
_RV = _RK + RET_WIDTH
_RG = _RV + RET_WIDTH
_KPE = _RG + RET_WIDTH
_KPE_SW = _KPE + LANES
IN_COLS_PAD = _KPE_SW + LANES
Q_HEAD_COLS = 3 * LANES


def _rms(x, g):
    return x * lax.rsqrt(jnp.mean(x * x, axis=-1, keepdims=True) + EPS) * g


def _const_spec(shape):
    nd = len(shape)
    return pl.BlockSpec(shape, lambda *_: (0,) * nd, pipeline_mode=pl.Buffered(1))


def _prep_rows(rs, x_ref, pos_ref, g_ref, win_ref, qng_ref, wuq_ref, kvng_ref, wukv_ref,
               qgn_ref, qgr_ref, qgs_ref, kgn_ref, kgr_ref, kgs_ref,
               fm_ref, sm_ref, fr_ref, sr_ref,
               q_out, k_out, v_out, rq_out, rk_out, rv_out, rg_out):
    x = x_ref[rs, :]
    h = _rms(x, g_ref[...]).astype(BF16)
    proj = jnp.dot(h, win_ref[...], preferred_element_type=F32)

    posf = pos_ref[rs, :].astype(F32)
    ang_m = posf * fm_ref[...]
    cos_m = jnp.cos(ang_m)
    sin_m = jnp.sin(ang_m) * sm_ref[...]
    ang_r = posf * fr_ref[...]
    cos_r = jnp.cos(ang_r)
    sin_r = jnp.sin(ang_r) * sr_ref[...]

    c_q = proj[:, _C_Q:_C_Q + MLA_Q_RANK]
    qall = jnp.dot(_rms(c_q, qng_ref[...]).astype(BF16), wuq_ref[...], preferred_element_type=F32)
    q_scale = MLA_QK_DIM ** -0.5 * math.log2(math.e)
    for hd in range(MLA_HEADS):
        base = hd * Q_HEAD_COLS
        nope = qall[:, base:base + LANES]
        rope = qall[:, base + LANES:base + 2 * LANES]
        rope_sw = qall[:, base + 2 * LANES:base + 3 * LANES]
        ss = jnp.sum(nope * nope, axis=-1, keepdims=True) + jnp.sum(rope * rope, axis=-1, keepdims=True)
        r = lax.rsqrt(ss * (1.0 / MLA_QK_DIM) + EPS) * q_scale
        roped = rope * qgr_ref[...] * cos_m + rope_sw * qgs_ref[...] * sin_m
        q_out[0, hd, rs, 0:LANES] = (nope * qgn_ref[...] * r).astype(BF16)
        q_out[0, hd, rs, LANES:2 * LANES] = (roped * r).astype(BF16)

    c_kv = proj[:, _C_KV:_C_KV + MLA_KV_RANK]
    kv = jnp.dot(_rms(c_kv, kvng_ref[...]).astype(BF16), wukv_ref[...], preferred_element_type=F32)
    k_pe = proj[:, _KPE:_KPE + LANES]
    k_pe_sw = proj[:, _KPE_SW:_KPE_SW + LANES]
    kr = k_pe * kgr_ref[...] * cos_m + k_pe_sw * kgs_ref[...] * sin_m
    ss_pe = jnp.sum(k_pe * k_pe, axis=-1, keepdims=True)
    for hd in range(MLA_HEADS):
        k_nope = kv[:, hd * LANES:(hd + 1) * LANES]
        ss = jnp.sum(k_nope * k_nope, axis=-1, keepdims=True) + ss_pe
        r = lax.rsqrt(ss * (1.0 / MLA_QK_DIM) + EPS)
        k_out[0, hd, rs, 0:LANES] = (k_nope * kgn_ref[...] * r).astype(BF16)
        k_out[0, hd, rs, LANES:2 * LANES] = (kr * r).astype(BF16)
        v_out[0, hd, rs, :] = kv[:, (MLA_HEADS + hd) * LANES:(MLA_HEADS + hd + 1) * LANES].astype(BF16)

    k_scale = RET_HEAD_DIM ** -0.5
    for hd in range(RET_HEADS):
        sl = slice(hd * LANES, (hd + 1) * LANES)
        rq = proj[:, _RQ + hd * LANES:_RQ + (hd + 1) * LANES]
        rk = proj[:, _RK + hd * LANES:_RK + (hd + 1) * LANES]
        rq_out[rs, sl] = (rq * cos_r + pltpu.roll(rq, LANES // 2, axis=1) * sin_r).astype(BF16)
        rk_out[rs, sl] = ((rk * cos_r + pltpu.roll(rk, LANES // 2, axis=1) * sin_r) * k_scale).astype(BF16)
    rv_out[rs, :] = proj[:, _RV:_RV + RET_WIDTH].astype(BF16)
    rg_out[rs, :] = proj[:, _RG:_RG + RET_WIDTH].astype(BF16)


def _prep_kernel(x_ref, *refs, chains):
    rows = x_ref.shape[0] // chains
    for c in range(chains):
        _prep_rows(slice(c * rows, (c + 1) * rows), x_ref, *refs)


def _prep(x2d, pos2d, p, batch, seq, tile):
    n_tok, d = x2d.shape
    tiles_per_b = seq // tile
    tok = lambda i: (i, 0)
    head = lambda i: (i // tiles_per_b, 0, i % tiles_per_b, 0)
    consts = [p['attn_norm_g'], p['w_in'], p['q_norm_g'], p['w_uq'], p['kv_norm_g'], p['w_ukv'],
              p['q_g_nope'], p['q_g_rope'], p['q_g_rope_sw'], p['k_g_nope'], p['k_g_rope'], p['k_g_rope_sw'],
              p['freq_m'], p['sign_m'], p['freq_r'], p['sign_r']]
    out_shape = (
        jax.ShapeDtypeStruct((batch, MLA_HEADS, seq, MLA_HEAD_PAD), BF16),
        jax.ShapeDtypeStruct((batch, MLA_HEADS, seq, MLA_HEAD_PAD), BF16),
        jax.ShapeDtypeStruct((batch, MLA_HEADS, seq, MLA_V_DIM), BF16),
        jax.ShapeDtypeStruct((n_tok, RET_WIDTH), BF16),
        jax.ShapeDtypeStruct((n_tok, RET_WIDTH), BF16),
        jax.ShapeDtypeStruct((n_tok, RET_WIDTH), BF16),
        jax.ShapeDtypeStruct((n_tok, RET_WIDTH), BF16),
    )
    return pl.pallas_call(
        functools.partial(_prep_kernel, chains=PREP_CHAINS),
        grid=(n_tok // tile,),
        in_specs=[pl.BlockSpec((tile, d), tok), pl.BlockSpec((tile, 1), tok)]
                 + [_const_spec(c.shape) for c in consts],
        out_specs=(
            pl.BlockSpec((1, MLA_HEADS, tile, MLA_HEAD_PAD), head),
            pl.BlockSpec((1, MLA_HEADS, tile, MLA_HEAD_PAD), head),
            pl.BlockSpec((1, MLA_HEADS, tile, MLA_V_DIM), head),
            pl.BlockSpec((tile, RET_WIDTH), tok),
            pl.BlockSpec((tile, RET_WIDTH), tok),
            pl.BlockSpec((tile, RET_WIDTH), tok),
            pl.BlockSpec((tile, RET_WIDTH), tok),
        ),
        out_shape=out_shape,
        compiler_params=pltpu.CompilerParams(dimension_semantics=("parallel",),
                                             vmem_limit_bytes=VMEM_LIMIT_BYTES),
        name="prep",
    )(x2d, pos2d, *consts)


def _flash_kernel(q_ref, k_ref, v_ref, o_ref, m_sc, l_sc, acc_sc, *, tq, tk, hb):
    qi = pl.program_id(2)
    m_sc[...] = jnp.full(m_sc.shape, NEG_INF, F32)
    l_sc[...] = jnp.zeros(l_sc.shape, F32)
    acc_sc[...] = jnp.zeros(acc_sc.shape, F32)

    def step(off, width, rows, mask):
        reps = width // LANES
        for hd in range(hb):
            k = k_ref[0, hd, pl.ds(off, width), :]
            v = v_ref[0, hd, pl.ds(off, width), :]
            s = lax.dot_general(q_ref[0, hd, rows, :], k, (((1,), (1,)), ((), ())), preferred_element_type=F32)
            if mask is not None:
                s = jnp.where(mask, s, NEG_INF)
            m_prev = m_sc[hd, rows, :]
            m_new = jnp.maximum(m_prev, jnp.max(s, axis=-1, keepdims=True))
            alpha = jnp.exp2(m_prev - m_new)
            p = jnp.exp2(s - jnp.concatenate([m_new] * reps, axis=1))
            p_cols = p[:, 0:LANES]
            for c in range(1, reps):
                p_cols = p_cols + p[:, c * LANES:(c + 1) * LANES]
            l_sc[hd, rows, :] = alpha * l_sc[hd, rows, :] + p_cols
            acc_sc[hd, rows, :] = alpha * acc_sc[hd, rows, :] + jnp.dot(p.astype(BF16), v,
                                                                        preferred_element_type=F32)
            m_sc[hd, rows, :] = m_new

    def body(j, c):
        step(pl.multiple_of(j * tk, tk), tk, slice(0, tq), None)
        return c

    lax.fori_loop(0, qi, body, 0)
    half = tk // 2
    base = pl.multiple_of(qi * tk, tk)

    def causal(n_rows):
        return (lax.broadcasted_iota(jnp.int32, (n_rows, half), 1)
                <= lax.broadcasted_iota(jnp.int32, (n_rows, half), 0))

    step(base, half, slice(0, tq), causal(tq))
    step(base + half, half, slice(half, tq), causal(tq - half))
    for hd in range(hb):
        l = jnp.sum(l_sc[hd], axis=-1, keepdims=True)
        o_ref[0, :, hd * LANES:(hd + 1) * LANES] = (acc_sc[hd] / l).astype(o_ref.dtype)


def _flash(q, k, v, tq, tk, hb):
    batch, heads, seq, dqk = q.shape
    dv = v.shape[-1]
    assert tq == tk and seq % tq == 0 and heads % hb == 0
    return pl.pallas_call(
        functools.partial(_flash_kernel, tq=tq, tk=tk, hb=hb),
        grid=(batch, heads // hb, seq // tq),
        in_specs=[
            pl.BlockSpec((1, hb, tq, dqk), lambda b, h, i: (b, h, i, 0)),
            pl.BlockSpec((1, hb, seq, dqk), lambda b, h, i: (b, h, 0, 0)),
            pl.BlockSpec((1, hb, seq, dv), lambda b, h, i: (b, h, 0, 0)),
        ],
        out_specs=pl.BlockSpec((1, tq, hb * dv), lambda b, h, i: (b, i, h)),
        out_shape=jax.ShapeDtypeStruct((batch, seq, heads * dv), BF16),
        scratch_shapes=[pltpu.VMEM((hb, tq, LANES), F32), pltpu.VMEM((hb, tq, LANES), F32),
                        pltpu.VMEM((hb, tq, dv), F32)],
        compiler_params=pltpu.CompilerParams(dimension_semantics=("parallel", "parallel", "arbitrary"),
                                             vmem_limit_bytes=VMEM_LIMIT_BYTES),
        name="flash",
    )(q, k, v)


def _ret_kernel(q_ref, k_ref, v_ref, g_ref, gn_ref, o_ref, state_sc, intra_sc):
    c = pl.program_id(1)
    n = RET_CHUNK
    log_decay = [math.log(1.0 - 2.0 ** (-5.0 - hd)) for hd in range(RET_HEADS)]

    @pl.when(c == 0)
    def _():
        state_sc[...] = jnp.zeros(state_sc.shape, F32)
        diff = (lax.broadcasted_iota(jnp.int32, (n, n), 0) - lax.broadcasted_iota(jnp.int32, (n, n), 1)).astype(F32)
        for hd in range(RET_HEADS):
            intra_sc[hd] = jnp.where(diff >= 0, jnp.exp(jnp.maximum(diff, 0.0) * log_decay[hd]), 0.0)

    idx = lax.broadcasted_iota(jnp.int32, (n, 1), 0).astype(F32)
    for bb, hd in [(bb, hd) for bb in range(q_ref.shape[0]) for hd in range(RET_HEADS)]:
        lg = log_decay[hd]
        intra = intra_sc[hd]
        q_decay = jnp.exp((idx + 1.0) * lg)
        k_decay = jnp.exp((n - 1.0 - idx) * lg)
        chunk_decay = math.exp(n * lg)
        sl = slice(hd * LANES, (hd + 1) * LANES)
        qc = q_ref[bb, :, sl]
        kc = k_ref[bb, :, sl]
        vc = v_ref[bb, :, sl]
        state = state_sc[bb * RET_HEADS + hd]
        scores = lax.dot_general(qc, kc, (((1,), (1,)), ((), ())), preferred_element_type=F32) * intra
        o = (jnp.dot(scores.astype(BF16), vc, preferred_element_type=F32)
             + jnp.dot((qc.astype(F32) * q_decay).astype(BF16), state.astype(BF16), preferred_element_type=F32))
        kd_t = (kc.astype(F32) * k_decay).T.astype(BF16)
        state_sc[bb * RET_HEADS + hd] = state * chunk_decay + jnp.dot(kd_t, vc, preferred_element_type=F32)
        mu = jnp.mean(o, axis=-1, keepdims=True)
        cen = o - mu
        var = jnp.mean(cen * cen, axis=-1, keepdims=True)
        gate = g_ref[bb, :, sl].astype(F32)
        y = cen * lax.rsqrt(var + EPS) * gn_ref[:, sl] * (gate * jax.nn.sigmoid(gate))
        o_ref[bb, :, sl] = y.astype(o_ref.dtype)


def _retention(rq, rk, rv, rg, gn_g):
    batch, seq, width = rq.shape
    assert batch % RET_ROWS_PER_STEP == 0 and seq % RET_CHUNK == 0
    blk = pl.BlockSpec((RET_ROWS_PER_STEP, RET_CHUNK, width), lambda b, c: (b, c, 0))
    return pl.pallas_call(
        _ret_kernel,
        grid=(batch // RET_ROWS_PER_STEP, seq // RET_CHUNK),
        in_specs=[blk, blk, blk, blk, _const_spec(gn_g.shape)],
        out_specs=blk,
        out_shape=jax.ShapeDtypeStruct((batch, seq, width), BF16),
        scratch_shapes=[pltpu.VMEM((RET_ROWS_PER_STEP * RET_HEADS, RET_HEAD_DIM, RET_HEAD_DIM), F32),
                        pltpu.VMEM((RET_HEADS, RET_CHUNK, RET_CHUNK), F32)],
        compiler_params=pltpu.CompilerParams(dimension_semantics=("parallel", "arbitrary")),
        name="retention",
    )(rq, rk, rv, rg, gn_g)


def _memkv_kernel(mem_ref, g_ref, wkv_ref, kg_ref, k_out, v_out):
    d = mem_ref.shape[-1]
    hdim = d // CROSS_HEADS
    m = _rms(mem_ref[0], g_ref[...]).astype(BF16)
    kv = jnp.dot(m, wkv_ref[...], preferred_element_type=F32)
    for hd in range(CROSS_HEADS):
        sl = slice(hd * hdim, (hd + 1) * hdim)
        k_out[0, :, sl] = _rms(kv[:, sl], kg_ref[...]).astype(BF16)
    v_out[0] = kv[:, d:].astype(BF16)


def _memkv(mem, mem_norm_g, w_kv, k_g):
    batch, m_tok, d = mem.shape
    blk = pl.BlockSpec((1, m_tok, d), lambda b: (b, 0, 0))
    return pl.pallas_call(
        _memkv_kernel,
        grid=(batch,),
        in_specs=[blk, _const_spec(mem_norm_g.shape), _const_spec(w_kv.shape), _const_spec(k_g.shape)],
        out_specs=(blk, blk),
        out_shape=(jax.ShapeDtypeStruct((batch, m_tok, d), BF16), jax.ShapeDtypeStruct((batch, m_tok, d), BF16)),
        compiler_params=pltpu.CompilerParams(dimension_semantics=("parallel",),
                                             vmem_limit_bytes=VMEM_LIMIT_BYTES),
        name="memkv",
    )(mem, mem_norm_g, w_kv, k_g)


def _mix_rows(x, a, r, kc_ref, vc_ref, woa_ref, wor_ref, cng_ref, wq_ref, qg_ref, wo_ref, mng_ref, wr_ref,
              br_ref):
    d = x.shape[-1]
    hdim = d // CROSS_HEADS
    x1 = (x
          + jnp.dot(a, woa_ref[...], preferred_element_type=F32)
          + jnp.dot(r, wor_ref[...], preferred_element_type=F32))
    hc = _rms(x1, cng_ref[...]).astype(BF16)
    q = jnp.dot(hc, wq_ref[...], preferred_element_type=F32)
    scale = hdim ** -0.5
    heads = []
    for hd in range(CROSS_HEADS):
        sl = slice(hd * hdim, (hd + 1) * hdim)
        qh = (_rms(q[:, sl], qg_ref[...]) * scale).astype(BF16)
        s = lax.dot_general(qh, kc_ref[0, :, sl], (((1,), (1,)), ((), ())), preferred_element_type=F32)
        s = s - jnp.max(s, axis=-1, keepdims=True)
        e = jnp.exp(s)
        pr = e / jnp.sum(e, axis=-1, keepdims=True)
        heads.append(jnp.dot(pr.astype(BF16), vc_ref[0, :, sl], preferred_element_type=F32).astype(BF16))
    o = jnp.concatenate(heads, axis=-1)
    x2 = x1 + jnp.dot(o, wo_ref[...], preferred_element_type=F32)
    hm = _rms(x2, mng_ref[...])

    hm_hi = hm.astype(BF16)
    hm_lo = (hm - hm_hi.astype(F32)).astype(BF16)
    hi_hl = jnp.dot(hm_hi, wr_ref[...], preferred_element_type=F32)
    logits = (hi_hl[:, :LANES] + hi_hl[:, LANES:]
              + jnp.dot(hm_lo, wr_ref[:, :LANES], preferred_element_type=F32))
    biased = logits + br_ref[...]
    lane = lax.broadcasted_iota(jnp.int32, logits.shape, 1)
    lane_f = lane.astype(F32)
    big = float(4 * LANES)
    is_g = (lane >= N_EXPERTS) & (lane < N_EXPERTS + N_GROUPS)
    gl = jnp.where(is_g, logits, NEG_INF)
    gexp = jnp.where(is_g, jnp.exp(gl - jnp.max(gl, axis=-1, keepdims=True)), 0.0)
    gb = jnp.where(is_g, biased, NEG_INF)
    g_lane = jnp.min(jnp.where(gb == jnp.max(gb, axis=-1, keepdims=True), lane_f, big), axis=-1, keepdims=True)
    g_w = (jnp.sum(jnp.where(lane_f == g_lane, gexp, 0.0), axis=-1, keepdims=True)
           / jnp.sum(gexp, axis=-1, keepdims=True))
    g_idx = g_lane.astype(jnp.int32) - N_EXPERTS
    in_grp = (lane < N_EXPERTS) & ((lane // EXPERTS_PER_GROUP) == g_idx)
    el = jnp.where(in_grp, logits, NEG_INF)
    eexp = jnp.where(in_grp, jnp.exp(el - jnp.max(el, axis=-1, keepdims=True)), 0.0)
    eb = jnp.where(in_grp, biased, NEG_INF)
    i1 = jnp.min(jnp.where(eb == jnp.max(eb, axis=-1, keepdims=True), lane_f, big), axis=-1, keepdims=True)
    eb2 = jnp.where(lane_f == i1, NEG_INF, eb)
    i2 = jnp.min(jnp.where(eb2 == jnp.max(eb2, axis=-1, keepdims=True), lane_f, big), axis=-1, keepdims=True)
    psel = jnp.where((lane_f == i1) | (lane_f == i2), eexp, 0.0)
    gate = psel / jnp.sum(psel, axis=-1, keepdims=True) * g_w
    grp_onehot = (lane == g_idx).astype(F32)
    return x2, hm.astype(BF16), gate, grp_onehot


def _group_sort(hm, gate, grp, hms_out, pos_out, meta_out, c):
    t = hm.shape[0]
    ts = SORT_SLOTS
    earlier = (lax.broadcasted_iota(jnp.int32, (t, t), 1) < lax.broadcasted_iota(jnp.int32, (t, t), 0))
    cnt = jnp.dot(earlier.astype(BF16), grp.astype(BF16), preferred_element_type=F32)
    n = jnp.sum(grp, axis=0, keepdims=True)
    n_al = jnp.floor((n + (SORT_ALIGN - 1.0)) * (1.0 / SORT_ALIGN)) * SORT_ALIGN
    lane1 = lax.broadcasted_iota(jnp.int32, (1, LANES), 1)
    start = jnp.zeros((1, LANES), F32)
    run = jnp.zeros((1, 1), F32)
    for g in range(1, N_GROUPS):
        run = run + jnp.sum(jnp.where(lane1 == g - 1, n_al, 0.0), axis=-1, keepdims=True)
        start = jnp.where(lane1 == g, run, start)
    pos = jnp.sum(grp * (cnt + start), axis=-1, keepdims=True)
    pos_out[c * t:(c + 1) * t, :] = pos
    row8 = lax.broadcasted_iota(jnp.int32, (8, LANES), 0)
    meta_out[c] = jnp.where(row8 == 0, start, jnp.where(row8 == 1, n, 0.0)).astype(jnp.int32)
    pos_row = jnp.broadcast_to(pos, (t, LANES)).T[0:1, :]
    perm = (lax.broadcasted_iota(jnp.int32, (ts, t), 0).astype(F32) == pos_row).astype(BF16)
    g_hi = gate.astype(BF16).astype(F32)
    rem = gate - g_hi
    g_mid = rem.astype(BF16).astype(F32)
    g_lo = rem - g_mid
    gate_pieces = (g_hi + pltpu.roll(g_mid, N_EXPERTS, axis=1) + pltpu.roll(g_lo, 2 * N_EXPERTS, axis=1)).astype(BF16)
    hms_out[c * ts:(c + 1) * ts, :] = jnp.dot(perm, jnp.concatenate([hm, gate_pieces], axis=-1),
                                              preferred_element_type=F32).astype(BF16)


def _mix_kernel(x_ref, a_ref, r_ref, *rest, chains):
    *w_refs, x2_out, hms_out, pos_out, meta_out = rest
    rows = x_ref.shape[0] // chains
    for c in range(chains):
        sl = slice(c * rows, (c + 1) * rows)
        x2, hm, gate, grp = _mix_rows(x_ref[sl, :], a_ref[sl, :], r_ref[sl, :], *w_refs)
        x2_out[sl, :] = x2
        _group_sort(hm, gate, grp, hms_out, pos_out, meta_out, c)


def _mix(x2d, a2d, r2d, kc, vc, p, batch, seq, tile):
    n_tok, d = x2d.shape
    half = a2d.shape[-1]
    tiles_per_b = seq // tile
    n_tiles = n_tok // tile
    tok = lambda i: (i, 0)
    mem_blk = pl.BlockSpec((1,) + kc.shape[1:], lambda i: (i // tiles_per_b, 0, 0))
    consts = [p['w_out_a'], p['w_out_r'], p['cross_norm_g'], p['cross_w_q'], p['cross_q_g'], p['cross_w_o'],
              p['moe_norm_g'], p['w_router'], p['b_router']]
    return pl.pallas_call(
        functools.partial(_mix_kernel, chains=MIX_CHAINS),
        grid=(n_tiles,),
        in_specs=[pl.BlockSpec((tile, d), tok), pl.BlockSpec((tile, half), tok), pl.BlockSpec((tile, half), tok),
                  mem_blk, mem_blk] + [_const_spec(c.shape) for c in consts],
        out_specs=(pl.BlockSpec((tile, d), tok), pl.BlockSpec((MIX_CHAINS * SORT_SLOTS, d + LANES), tok),
                   pl.BlockSpec((tile, 1), tok), pl.BlockSpec((MIX_CHAINS, 8, LANES), lambda i: (i, 0, 0))),
        out_shape=(jax.ShapeDtypeStruct((n_tok, d), F32),
                   jax.ShapeDtypeStruct((n_tiles * MIX_CHAINS * SORT_SLOTS, d + LANES), BF16),
                   jax.ShapeDtypeStruct((n_tok, 1), F32),
                   jax.ShapeDtypeStruct((n_tiles * MIX_CHAINS, 8, LANES), jnp.int32)),
        compiler_params=pltpu.CompilerParams(dimension_semantics=("parallel",),
                                             vmem_limit_bytes=VMEM_LIMIT_BYTES),
        name="mix",
    )(x2d, a2d, r2d, kc, vc, *consts)


def _moe_kernel(start_ref, count_ref, hms_ref, wgu_ref, wd_ref, o_ref, *, group, tiles, chunk):
    blk = pl.program_id(0)
    step = pl.program_id(1)
    d = wd_ref.shape[1]
    ff = wgu_ref.shape[2] // 2
    grp = step // (EXPERTS_PER_GROUP // group)

    @pl.when(step == 0)
    def _():
        o_ref[...] = jnp.zeros(o_ref.shape, o_ref.dtype)

    def run_rows(r0, rows):
        lane = lax.broadcasted_iota(jnp.int32, (rows, LANES), 1)
        piece_lane = lane < 3 * N_EXPERTS
        xs = hms_ref[pl.ds(r0, rows), 0:d]
        gates = hms_ref[pl.ds(r0, rows), d:d + LANES].astype(F32)
        acts = []
        for j in range(group):
            e = step * group + j
            au = jnp.dot(xs, wgu_ref[j], preferred_element_type=F32)
            a = au[:, :ff]
            u = au[:, ff:]
            g_e = jnp.sum(jnp.where(piece_lane & ((lane & (N_EXPERTS - 1)) == e), gates, 0.0),
                          axis=-1, keepdims=True)
            acts.append((a * jax.nn.sigmoid(a) * u * g_e).astype(BF16))
        y = jnp.dot(jnp.concatenate(acts, axis=-1), wd_ref[...], preferred_element_type=F32)
        o_ref[pl.ds(r0, rows), :] = (o_ref[pl.ds(r0, rows), :].astype(F32) + y).astype(o_ref.dtype)

    def tile_body(tl, carry):
        idx = (blk * tiles + tl) * N_GROUPS + grp
        base = pl.multiple_of(tl * SORT_SLOTS + start_ref[idx], SORT_ALIGN)
        n = count_ref[idx]

        @pl.when((n > 0) & (n <= MOE_SMALL_CHUNK))
        def _():
            run_rows(base, MOE_SMALL_CHUNK)

        @pl.when(n > MOE_SMALL_CHUNK)
        def _():
            def chunk_body(c, carry2):
                run_rows(pl.multiple_of(base + c * chunk, SORT_ALIGN), chunk)
                return carry2

            lax.fori_loop(0, (n + chunk - 1) // chunk, chunk_body, 0)

        return carry

    lax.fori_loop(0, tiles, tile_body, 0)


def _moe(hms, starts, counts, w_gu, w_d, tiles, group, chunk):
    n_rows, dx = hms.shape
    n_exp, d, ff2 = w_gu.shape
    ff = ff2 // 2
    rows = tiles * SORT_SLOTS
    grid_spec = pltpu.PrefetchScalarGridSpec(
        num_scalar_prefetch=2,
        grid=(n_rows // rows, n_exp // group),
        in_specs=[pl.BlockSpec((rows, dx), lambda i, e, s, c: (i, 0)),
                  pl.BlockSpec((group, d, ff2), lambda i, e, s, c: (e, 0, 0)),
                  pl.BlockSpec((group * ff, d), lambda i, e, s, c: (e, 0))],
        out_specs=pl.BlockSpec((rows, d), lambda i, e, s, c: (i, 0)),
    )
    return pl.pallas_call(
        functools.partial(_moe_kernel, group=group, tiles=tiles, chunk=chunk),
        grid_spec=grid_spec,
        out_shape=jax.ShapeDtypeStruct((n_rows, d), BF16),
        compiler_params=pltpu.CompilerParams(dimension_semantics=("parallel", "arbitrary"),
                                             vmem_limit_bytes=VMEM_LIMIT_BYTES),
        name="moe",
    )(starts, counts, hms, w_gu, w_d)


def _unsort_kernel(x2_ref, pos_ref, ys_ref, o_ref):
    t = x2_ref.shape[0]
    ts = ys_ref.shape[0]
    unperm = (lax.broadcasted_iota(jnp.int32, (t, ts), 1).astype(F32) == pos_ref[...]).astype(BF16)
    o_ref[...] = x2_ref[...] + jnp.dot(unperm, ys_ref[...], preferred_element_type=F32)


def _unsort(x2, pos, ys, tile):
    n_tok, d = x2.shape
    tok = lambda i: (i, 0)
    return pl.pallas_call(
        _unsort_kernel,
        grid=(n_tok // tile,),
        in_specs=[pl.BlockSpec((tile, d), tok), pl.BlockSpec((tile, 1), tok), pl.BlockSpec((SORT_SLOTS, d), tok)],
        out_specs=pl.BlockSpec((tile, d), tok),
        out_shape=jax.ShapeDtypeStruct((n_tok, d), F32),
        compiler_params=pltpu.CompilerParams(dimension_semantics=("parallel",)),
        name="unsort",
    )(x2, pos, ys)


def _row(v):
    return v.reshape(1, -1).astype(F32)


def _pad_lanes(v, width=LANES):
    return jnp.pad(v, [(0, 0)] * (v.ndim - 1) + [(0, width - v.shape[-1])])


def _swap_halves(v):
    half = v.shape[-1] // 2
    return jnp.concatenate([v[..., half:], v[..., :half]], axis=-1)


def _layout_params(l, attn_norm_g, w_in, mla_q_norm_g, mla_w_uq, mla_kv_norm_g, mla_w_ukv, mla_q_qk_g,
                   mla_k_qk_g, ret_gn_g, w_out, cross_norm_g, mem_norm_g, cross_w_q, cross_w_kv, cross_q_qk_g,
                   cross_k_qk_g, cross_w_o, moe_norm_g, router_w_group, router_b_group, router_w_expert,
                   router_b_expert):
    p = {}
    d = w_in.shape[1]
    wi = w_in[l]
    c_q, c_kv, k_pe, rq, rk, rv, rg = jnp.split(
        wi, np.cumsum([MLA_Q_RANK, MLA_KV_RANK, MLA_ROPE_DIM, RET_WIDTH, RET_WIDTH, RET_WIDTH]).tolist(), axis=1)
    p['w_in'] = jnp.concatenate([c_q, c_kv, rq, rk, rv, rg, _pad_lanes(k_pe), _pad_lanes(_swap_halves(k_pe))],
                                axis=1).astype(BF16)
    wq = mla_w_uq[l].reshape(MLA_Q_RANK, MLA_HEADS, MLA_QK_DIM)
    wq_rope = wq[:, :, MLA_NOPE_DIM:]
    p['w_uq'] = jnp.concatenate([wq[:, :, :MLA_NOPE_DIM], _pad_lanes(wq_rope), _pad_lanes(_swap_halves(wq_rope))],
                                axis=2).reshape(MLA_Q_RANK, MLA_HEADS * Q_HEAD_COLS).astype(BF16)
    wkv = mla_w_ukv[l].reshape(MLA_KV_RANK, MLA_HEADS, MLA_NOPE_DIM + MLA_V_DIM)
    p['w_ukv'] = jnp.concatenate([wkv[:, :, :MLA_NOPE_DIM].reshape(MLA_KV_RANK, -1),
                                  wkv[:, :, MLA_NOPE_DIM:].reshape(MLA_KV_RANK, -1)], axis=1).astype(BF16)
    p['attn_norm_g'] = _row(attn_norm_g[l])
    p['q_norm_g'] = _row(mla_q_norm_g[l])
    p['kv_norm_g'] = _row(mla_kv_norm_g[l])
    for name, g in (('q', mla_q_qk_g[l]), ('k', mla_k_qk_g[l])):
        p[name + '_g_nope'] = _row(g[:MLA_NOPE_DIM])
        p[name + '_g_rope'] = _row(_pad_lanes(g[MLA_NOPE_DIM:]))
        p[name + '_g_rope_sw'] = _row(_pad_lanes(_swap_halves(g[MLA_NOPE_DIM:])))
    hm_ = MLA_ROPE_DIM // 2
    f_m = ROPE_BASE ** (-jnp.arange(hm_, dtype=F32) / hm_)
    p['freq_m'] = _row(_pad_lanes(jnp.concatenate([f_m, f_m])))
    p['sign_m'] = _row(_pad_lanes(jnp.concatenate([-jnp.ones(hm_, F32), jnp.ones(hm_, F32)])))
    hr_ = RET_HEAD_DIM // 2
    f_r = ROPE_BASE ** (-jnp.arange(hr_, dtype=F32) / hr_)
    p['freq_r'] = _row(jnp.concatenate([f_r, f_r]))
    p['sign_r'] = _row(jnp.concatenate([-jnp.ones(hr_, F32), jnp.ones(hr_, F32)]))
    p['ret_gn_g'] = _row(ret_gn_g[l])
    half = MLA_HEADS * MLA_V_DIM
    p['w_out_a'] = w_out[l][:half].astype(BF16)
    p['w_out_r'] = w_out[l][half:].astype(BF16)
    p['cross_norm_g'] = _row(cross_norm_g[l])
    p['mem_norm_g'] = _row(mem_norm_g[l])
    p['cross_w_q'] = cross_w_q[l].astype(BF16)
    p['cross_w_kv'] = cross_w_kv[l].astype(BF16)
    p['cross_q_g'] = _row(cross_q_qk_g[l])
    p['cross_k_g'] = _row(cross_k_qk_g[l])
    p['cross_w_o'] = cross_w_o[l].astype(BF16)
    p['moe_norm_g'] = _row(moe_norm_g[l])
    w_r = _pad_lanes(jnp.concatenate([router_w_expert[l], router_w_group[l]], axis=1)).astype(F32)
    w_r_hi = w_r.astype(BF16)
    w_r_lo = (w_r - w_r_hi.astype(F32)).astype(BF16)
    p['w_router'] = jnp.concatenate([w_r_hi, w_r_lo], axis=1)
    p['b_router'] = _row(_pad_lanes(jnp.concatenate([router_b_expert[l], router_b_group[l]])))
    return p


PREP_TILE = 1024
PREP_CHAINS = 1
FLASH_TQ = 1024
FLASH_TK = 1024
FLASH_HEADS_PER_STEP = 4
SORT_TILE = 512
MIX_CHAINS = 2
MIX_TILE = SORT_TILE * MIX_CHAINS
SORT_ALIGN = 16
MOE_CHUNK = 144
MOE_SMALL_CHUNK = 128
SORT_SLOTS = -(-(SORT_TILE + N_GROUPS * (SORT_ALIGN - 1) + MOE_CHUNK) // LANES) * LANES
MOE_TILES_PER_BLOCK = 4
MOE_EXPERTS_PER_STEP = 8


def kernel(x, mem, positions, attn_norm_g, w_in, mla_q_norm_g, mla_w_uq, mla_kv_norm_g, mla_w_ukv, mla_q_qk_g, mla_k_qk_g, ret_gn_g, w_out, cross_norm_g, mem_norm_g, cross_w_q, cross_w_kv, cross_q_qk_g, cross_k_qk_g, cross_w_o, moe_norm_g, router_w_group, router_b_group, router_w_expert, router_b_expert, expert_w_gate, expert_w_up, expert_w_down):
    batch, seq, d = x.shape
    depth = w_in.shape[0]
    pos2d = positions.reshape(batch * seq, 1).astype(jnp.int32)
    x2d = x.reshape(batch * seq, d)
    for l in range(depth):
        p = _layout_params(l, attn_norm_g, w_in, mla_q_norm_g, mla_w_uq, mla_kv_norm_g, mla_w_ukv, mla_q_qk_g,
                           mla_k_qk_g, ret_gn_g, w_out, cross_norm_g, mem_norm_g, cross_w_q, cross_w_kv,
                           cross_q_qk_g, cross_k_qk_g, cross_w_o, moe_norm_g, router_w_group, router_b_group,
                           router_w_expert, router_b_expert)
        w_gu = jnp.concatenate([expert_w_gate[l], expert_w_up[l]], axis=-1).astype(BF16)
        w_d = expert_w_down[l].reshape(-1, d).astype(BF16)

        q, k, v, rq, rk, rv, rg = _prep(x2d, pos2d, p, batch, seq, PREP_TILE)
        a_out = _flash(q, k, v, FLASH_TQ, FLASH_TK, FLASH_HEADS_PER_STEP)
        shp = (batch, seq, RET_WIDTH)
        r_out = _retention(rq.reshape(shp), rk.reshape(shp), rv.reshape(shp), rg.reshape(shp), p['ret_gn_g'])
        kc, vc = _memkv(mem, p['mem_norm_g'], p['cross_w_kv'], p['cross_k_g'])
        x2, hms, pos, meta = _mix(x2d, a_out.reshape(batch * seq, -1), r_out.reshape(batch * seq, -1), kc, vc, p,
                                  batch, seq, MIX_TILE)
        starts = meta[:, 0, :N_GROUPS].reshape(-1)
        counts = meta[:, 1, :N_GROUPS].reshape(-1)
        ys = _moe(hms, starts, counts, w_gu, w_d, MOE_TILES_PER_BLOCK, MOE_EXPERTS_PER_STEP, MOE_CHUNK)
        x2d = _unsort(x2, pos, ys, SORT_TILE)
    return x2d.reshape(batch, seq, d)
```
